```python
import math
import jax
import jax.numpy as jnp
from jax import lax
import numpy as np

D_MODEL = 1024
BATCH = 4
SEQ = 4096
DEPTH = 2

D_INNER = 2 * D_MODEL
N_BRANCH = 4
BRANCH_W = D_INNER // N_BRANCH
HGRN_HEADS = 4
RET_HEADS = 4
RET_DECAY_EXP0 = 5.0
ROPE_BASE = 10000.0
SSM_HEAD_DIM = 64
SSM_HEADS = BRANCH_W // SSM_HEAD_DIM
SSM_GROUPS = 2
SSM_STATE = 128
SSM_CONV = 4
SSM_CONV_CH = BRANCH_W + 2 * SSM_GROUPS * SSM_STATE
SSM_DT_MIN = 0.001
SSM_DT_MAX = 0.1
GLA_HEADS = 4
GLA_KEY_W = BRANCH_W // 2
GLA_LOWRANK = 16
GLA_GATE_TEMP = 16.0
CHUNK = 64
EPS = 1e-6
IN_SPLITS = (
    BRANCH_W, BRANCH_W, BRANCH_W, BRANCH_W,
    BRANCH_W, BRANCH_W, BRANCH_W, BRANCH_W,
    BRANCH_W, SSM_CONV_CH, SSM_HEADS,
    GLA_KEY_W, GLA_KEY_W, BRANCH_W, BRANCH_W, GLA_LOWRANK,
)
IN_PROJ_W = 8 * BRANCH_W + BRANCH_W + SSM_CONV_CH + SSM_HEADS + 2 * GLA_KEY_W + 2 * BRANCH_W + GLA_LOWRANK

kernel_name = 'hybrid_parallel_heads_decoder'


def rmsnorm(x, g):
    xf = x.astype(jnp.float32)
    y = xf * lax.rsqrt(jnp.mean(xf * xf, axis=-1, keepdims=True) + EPS)
    return (y * g.astype(jnp.float32)).astype(x.dtype)


def group_rmsnorm(y, g, n_groups):
    b, l, w = y.shape
    yf = y.astype(jnp.float32).reshape(b, l, n_groups, w // n_groups)
    yf = yf * lax.rsqrt(jnp.mean(yf * yf, axis=-1, keepdims=True) + EPS)
    return (yf.reshape(b, l, w) * g.astype(jnp.float32)).astype(y.dtype)


def to_heads(t, n_heads):
    b, l, w = t.shape
    return t.reshape(b, l, n_heads, w // n_heads).transpose(0, 2, 1, 3)


def from_heads(t):
    b, h, l, d = t.shape
    return t.transpose(0, 2, 1, 3).reshape(b, l, h * d)


def to_chunks(t):
    b, h, l = t.shape[:3]
    return jnp.moveaxis(t.reshape(b, h, l // CHUNK, CHUNK, *t.shape[3:]), 2, 0)


def from_chunks(t):
    n, b, h, c, d = t.shape
    return jnp.moveaxis(t, 0, 2).reshape(b, h, n * c, d)


def chunked_scalar_decay(q, k, v, log_a):
    out_dtype = v.dtype
    q, k, v, log_a = (t.astype(jnp.float32) for t in (q, k, v, log_a))
    b, h, _, dk = q.shape
    dv = v.shape[-1]
    causal = jnp.tril(jnp.ones((CHUNK, CHUNK), dtype=bool))

    def step(state, inp):
        qc, kc, vc, ac = inp
        cum = jnp.cumsum(ac, axis=-1)
        rel = jnp.where(causal, cum[..., :, None] - cum[..., None, :], -jnp.inf)
        scores = jnp.einsum('bhtd,bhsd->bhts', qc, kc) * jnp.exp(rel)
        o = (jnp.einsum('bhts,bhsv->bhtv', scores, vc)
             + jnp.einsum('bhtd,bhdv->bhtv', qc * jnp.exp(cum)[..., None], state))
        k_to_end = kc * jnp.exp(cum[..., -1:] - cum)[..., None]
        state = (state * jnp.exp(cum[..., -1])[..., None, None]
                 + jnp.einsum('bhsd,bhsv->bhdv', k_to_end, vc))
        return state, o

    state0 = jnp.zeros((b, h, dk, dv), jnp.float32)
    _, o = lax.scan(step, state0, (to_chunks(q), to_chunks(k), to_chunks(v), to_chunks(log_a)))
    return from_chunks(o).astype(out_dtype)


def chunked_vector_decay(q, k, v, log_f):
    out_dtype = v.dtype
    q, k, v, log_f = (t.astype(jnp.float32) for t in (q, k, v, log_f))
    b, h, _, dk = q.shape
    dv = v.shape[-1]
    causal = jnp.tril(jnp.ones((CHUNK, CHUNK), dtype=bool))[..., None]

    def step(state, inp):
        qc, kc, vc, fc = inp
        cum = jnp.cumsum(fc, axis=-2)
        rel = jnp.where(causal, cum[..., :, None, :] - cum[..., None, :, :], -jnp.inf)
        scores = jnp.einsum('bhtd,bhsd,bhtsd->bhts', qc, kc, jnp.exp(rel))
        o = (jnp.einsum('bhts,bhsv->bhtv', scores, vc)
             + jnp.einsum('bhtd,bhdv->bhtv', qc * jnp.exp(cum), state))
        k_to_end = kc * jnp.exp(cum[..., -1:, :] - cum)
        state = (state * jnp.exp(cum[..., -1, :])[..., None]
                 + jnp.einsum('bhsd,bhsv->bhdv', k_to_end, vc))
        return state, o

    state0 = jnp.zeros((b, h, dk, dv), jnp.float32)
    _, o = lax.scan(step, state0, (to_chunks(q), to_chunks(k), to_chunks(v), to_chunks(log_f)))
    return from_chunks(o).astype(out_dtype)


def causal_depthwise_conv(u, w, bias):
    y = lax.conv_general_dilated(
        u, w[:, None, :].astype(u.dtype), window_strides=(1,),
        padding=[(SSM_CONV - 1, 0)], dimension_numbers=('NWC', 'WIO', 'NWC'),
        feature_group_count=u.shape[-1])
    return y + bias.astype(u.dtype)


def rotary(t):
    d = t.shape[-1]
    l = t.shape[2]
    inv_freq = ROPE_BASE ** (-jnp.arange(0, d, 2, dtype=jnp.float32) / d)
    ang = jnp.arange(l, dtype=jnp.float32)[:, None] * inv_freq[None, :]
    cos, sin = jnp.cos(ang), jnp.sin(ang)
    tf = t.astype(jnp.float32)
    t1, t2 = tf[..., : d // 2], tf[..., d // 2:]
    return jnp.concatenate([t1 * cos - t2 * sin, t1 * sin + t2 * cos], axis=-1).astype(t.dtype)


def hgrn2_branch(q_in, f_in, i_in, g_in, lower_bound, onorm_g):
    q = to_heads(jax.nn.silu(q_in), HGRN_HEADS)
    lb = lower_bound.astype(jnp.float32)
    log_f = jnp.logaddexp(jnp.log(lb), jnp.log1p(-lb) + jax.nn.log_sigmoid(f_in.astype(jnp.float32)))
    k = -jnp.expm1(log_f)
    o = chunked_vector_decay(q, to_heads(k, HGRN_HEADS), to_heads(i_in, HGRN_HEADS),
                             to_heads(log_f, HGRN_HEADS))
    return group_rmsnorm(from_heads(o), onorm_g, HGRN_HEADS) * jax.nn.silu(g_in)


def retention_branch(q_in, k_in, v_in, g_in, onorm_g):
    dk = BRANCH_W // RET_HEADS
    q = rotary(to_heads(q_in, RET_HEADS))
    k = rotary(to_heads(k_in, RET_HEADS)) * (dk ** -0.5)
    b, _, l, _ = q.shape
    log_gamma = jnp.log1p(-jnp.exp2(-(RET_DECAY_EXP0 + jnp.arange(RET_HEADS, dtype=jnp.float32))))
    log_a = jnp.broadcast_to(log_gamma[None, :, None], (b, RET_HEADS, l))
    o = chunked_scalar_decay(q, k, to_heads(v_in, RET_HEADS), log_a)
    return group_rmsnorm(from_heads(o), onorm_g, RET_HEADS) * jax.nn.silu(g_in)


def ssd_branch(z_in, xbc_in, dt_in, conv_w, conv_b, dt_bias, a_log, d_skip, norm_g):
    xbc = jax.nn.silu(causal_depthwise_conv(xbc_in, conv_w, conv_b))
    xs, bmat, cmat = jnp.split(xbc, [BRANCH_W, BRANCH_W + SSM_GROUPS * SSM_STATE], axis=-1)
    b, l, _ = xs.shape
    heads_per_group = SSM_HEADS // SSM_GROUPS

    def group_to_heads(m):
        m = m.reshape(b, l, SSM_GROUPS, SSM_STATE).transpose(0, 2, 1, 3)
        return jnp.repeat(m, heads_per_group, axis=1)

    dt = jax.nn.softplus(dt_in.astype(jnp.float32) + dt_bias.astype(jnp.float32)).transpose(0, 2, 1)
    a = -jnp.exp(a_log.astype(jnp.float32))
    xh = to_heads(xs, SSM_HEADS).astype(jnp.float32)
    y = chunked_scalar_decay(group_to_heads(cmat), group_to_heads(bmat),
                             xh * dt[..., None], dt * a[None, :, None])
    y = y + d_skip.astype(jnp.float32)[None, :, None, None] * xh
    y = from_heads(y).astype(z_in.dtype) * jax.nn.silu(z_in)
    return group_rmsnorm(y, norm_g, SSM_GROUPS)


def gla_branch(q_in, k_in, v_in, g_in, lr_in, w_gk2, b_gk2, onorm_g):
    dk = GLA_KEY_W // GLA_HEADS
    gk = jnp.einsum('blr,rk->blk', lr_in, w_gk2) + b_gk2
    log_f = jax.nn.log_sigmoid(gk.astype(jnp.float32)) / GLA_GATE_TEMP
    q = to_heads(q_in, GLA_HEADS) * (dk ** -0.5)
    o = chunked_vector_decay(q, to_heads(k_in, GLA_HEADS), to_heads(v_in, GLA_HEADS),
                             to_heads(log_f, GLA_HEADS))
    return group_rmsnorm(from_heads(o), onorm_g, GLA_HEADS) * jax.nn.silu(g_in)


def setup_inputs(seed: int = 0) -> dict:
    key = jax.random.key(seed)
    ks = jax.random.split(key, 20)
    f32 = jnp.float32

    def nrm(k, shape, scale):
        return scale * jax.random.normal(k, shape, f32)

    def gain(k, shape):
        return 1.0 + 0.01 * jax.random.normal(k, shape, f32)

    dt = jnp.exp(jax.random.uniform(ks[11], (DEPTH, SSM_HEADS), f32,
                                    math.log(SSM_DT_MIN), math.log(SSM_DT_MAX)))
    return {
        'x': jax.random.normal(ks[0], (BATCH, SEQ, D_MODEL), f32),
        'c': jax.random.normal(ks[1], (BATCH, D_MODEL), f32),
        'w_ada': nrm(ks[2], (DEPTH, D_MODEL, 3 * D_MODEL), 0.5 * D_MODEL ** -0.5),
        'b_ada': nrm(ks[3], (DEPTH, 3 * D_MODEL), 0.01),
        'norm_g': gain(ks[4], (DEPTH, D_MODEL)),
        'w_in': nrm(ks[5], (DEPTH, D_MODEL, IN_PROJ_W), D_MODEL ** -0.5),
        'hgrn_lb_logits': nrm(ks[6], (DEPTH, BRANCH_W), 0.1),
        'hgrn_onorm_g': gain(ks[7], (DEPTH, BRANCH_W)),
        'ret_onorm_g': gain(ks[8], (DEPTH, BRANCH_W)),
        'ssm_conv_w': nrm(ks[9], (DEPTH, SSM_CONV, SSM_CONV_CH), SSM_CONV ** -0.5),
        'ssm_conv_b': nrm(ks[10], (DEPTH, SSM_CONV_CH), 0.01),
        'ssm_dt_bias': dt + jnp.log(-jnp.expm1(-dt)),
        'ssm_a_log': jnp.log(jax.random.uniform(ks[12], (DEPTH, SSM_HEADS), f32, 1.0, 16.0)),
        'ssm_d': gain(ks[13], (DEPTH, SSM_HEADS)),
        'ssm_norm_g': gain(ks[14], (DEPTH, BRANCH_W)),
        'gla_w_gk2': nrm(ks[15], (DEPTH, GLA_LOWRANK, GLA_KEY_W), GLA_LOWRANK ** -0.5),
        'gla_b_gk2': nrm(ks[16], (DEPTH, GLA_KEY_W), 0.01),
        'gla_onorm_g': gain(ks[17], (DEPTH, BRANCH_W)),
        'w_out': nrm(ks[18], (DEPTH, D_INNER, D_MODEL), D_INNER ** -0.5),
        'final_g': gain(ks[19], (D_MODEL,)),
    }


def reference(x, c, w_ada, b_ada, norm_g, w_in, hgrn_lb_logits, hgrn_onorm_g, ret_onorm_g,
              ssm_conv_w, ssm_conv_b, ssm_dt_bias, ssm_a_log, ssm_d, ssm_norm_g,
              gla_w_gk2, gla_b_gk2, gla_onorm_g, w_out, final_g):
    c_act = jax.nn.silu(c)
    lb_cum = jnp.cumsum(jax.nn.softmax(hgrn_lb_logits.astype(jnp.float32), axis=0), axis=0)
    lower_bounds = lb_cum - lb_cum[0]
    split_at = [int(s) for s in np.cumsum(IN_SPLITS)[:-1]]
    for layer in range(DEPTH):
        mod = c_act @ w_ada[layer] + b_ada[layer]
        shift, scale, gate = jnp.split(mod[:, None, :], 3, axis=-1)
        h = rmsnorm(x, norm_g[layer]) * (1.0 + scale) + shift
        proj = jnp.einsum('bld,de->ble', h, w_in[layer])
        (aq, af, ai, ag, rq, rk, rv, rg, mz, mxbc, mdt,
         gq, gk, gv, gg, glr) = jnp.split(proj, split_at, axis=-1)
        y_a = hgrn2_branch(aq, af, ai, ag, lower_bounds[layer], hgrn_onorm_g[layer])
        y_b = retention_branch(rq, rk, rv, rg, ret_onorm_g[layer])
        y_c = ssd_branch(mz, mxbc, mdt, ssm_conv_w[layer], ssm_conv_b[layer], ssm_dt_bias[layer],
                         ssm_a_log[layer], ssm_d[layer], ssm_norm_g[layer])
        y_d = gla_branch(gq, gk, gv, gg, glr, gla_w_gk2[layer], gla_b_gk2[layer], gla_onorm_g[layer])
        y = jnp.concatenate([y_a, y_b, y_c, y_d], axis=-1)
        x = x + gate * jnp.einsum('ble,ed->bld', y, w_out[layer])
    return rmsnorm(x, final_g)
```

```python
import functools
import math

import numpy as np
import jax
import jax.numpy as jnp
from jax import lax
from jax.experimental import pallas as pl
from jax.experimental.pallas import tpu as pltpu

F32 = jnp.float32
BF16 = jnp.bfloat16

D_MODEL = 1024
DEPTH = 2
BRANCH_W = 512
D_INNER = 4 * BRANCH_W
HEAD_W = 128
N_HEADS = 4
RET_DECAY_EXP0 = 5.0
ROPE_BASE = 10000.0
SSM_HEADS = 8
SSM_HEAD_DIM = 64
SSM_GROUPS = 2
SSM_STATE = 128
SSM_CONV = 4
SSM_CONV_CH = BRANCH_W + 2 * SSM_GROUPS * SSM_STATE
GLA_KEY_W = 256
GLA_LOWRANK = 16
GLA_GATE_TEMP = 16.0
EPS = 1e-6

LANES = 128
SUBLANES = 8

C_AQ, C_AF, C_AI, C_AG = 0, 512, 1024, 1536
C_RQ, C_RK, C_RV, C_RG = 2048, 2560, 3072, 3584
C_MZ, C_XBC = 4096, 4608
C_GQ, C_GK, C_GV, C_GG = 5632, 5888, 6144, 6656
C_TAIL = 7168
PROJ_W = C_TAIL + LANES
ORIG_DT = 5632
ORIG_GQ = ORIG_DT + SSM_HEADS
ORIG_LR = ORIG_GQ + 2 * GLA_KEY_W + 2 * BRANCH_W
TAIL_LR = SSM_HEADS

TILE = 256
CHUNK = 64
SUB = 16
N_SUB = CHUNK // SUB
VMEM_LIMIT_BYTES = 56 * 1024 * 1024


def _dot(a, b):
    return jnp.dot(a, b, preferred_element_type=F32)


def _dot_nt(a, b):
    return lax.dot_general(a, b, (((1,), (1,)), ((), ())), preferred_element_type=F32)


def _dot_tn(a, b):
    return lax.dot_general(a, b, (((0,), (0,)), ((), ())), preferred_element_type=F32)


def _dot3(m_bf16, x):
    hi = x.astype(BF16)
    r1 = x - hi.astype(F32)
    mid = r1.astype(BF16)
    lo = (r1 - mid.astype(F32)).astype(BF16)
    return _dot(m_bf16, hi) + _dot(m_bf16, mid) + _dot(m_bf16, lo)


def _silu(x):
    return x / (1.0 + jnp.exp(-x))


def _log1p_exp_neg_abs(x):
    return jnp.log1p(jnp.exp(-jnp.abs(x)))


def _log_sigmoid(x):
    return jnp.minimum(x, 0.0) - _log1p_exp_neg_abs(x)


def _softplus(x):
    return jnp.maximum(x, 0.0) + _log1p_exp_neg_abs(x)


def _logaddexp(a, b):
    return jnp.maximum(a, b) + _log1p_exp_neg_abs(a - b)


def _rms(x, width):
    return x * lax.rsqrt(jnp.sum(x * x, axis=-1, keepdims=True) * (1.0 / width) + EPS)


def _iota(shape, dim):
    return lax.broadcasted_iota(jnp.int32, shape, dim)


def _div_pow2(x, d):
    shift = d.bit_length() - 1
    assert 1 << shift == d
    return lax.shift_right_logical(x, shift)


def _lower_tri(n, block):
    r = _iota((n, n), 0)
    c = _iota((n, n), 1)
    keep = (c <= r) & (_div_pow2(r, block) == _div_pow2(c, block))
    return jnp.where(keep, 1.0, 0.0).astype(BF16)


def _ada_kernel(c_ref, w_ref, b_ref, o_ref):
    c_act = _silu(c_ref[...]).astype(BF16)
    o_ref[...] = _dot(c_act, w_ref[...].astype(BF16)) + b_ref[...]


def _ada_mod(c_pad, w_ada, b_ada):
    n_blk = 3
    return pl.pallas_call(
        _ada_kernel,
        grid=(DEPTH, n_blk),
        in_specs=[
            pl.BlockSpec((SUBLANES, D_MODEL), lambda l, j: (0, 0)),
            pl.BlockSpec((None, D_MODEL, D_MODEL), lambda l, j: (l, 0, j)),
            pl.BlockSpec((None, 1, D_MODEL), lambda l, j: (l, 0, j)),
        ],
        out_specs=pl.BlockSpec((None, SUBLANES, D_MODEL), lambda l, j: (l, 0, j)),
        out_shape=jax.ShapeDtypeStruct((DEPTH, SUBLANES, 3 * D_MODEL), F32),
        name="ada_mod",
    )(c_pad, w_ada, b_ada.reshape(DEPTH, 1, 3 * D_MODEL))


def _vector_decay_chunk(q, k, v_of, cum, st_refs, lane_ranges):
    n_heads = len(lane_ranges)

    def keep_lanes(hd, x):
        if lane_ranges[hd] is None:
            return x
        lane = _iota(x.shape, 1)
        return jnp.where((lane >= lane_ranges[hd][0]) & (lane < lane_ranges[hd][1]), x, 0.0)

    cum_last = cum[CHUNK - 1:CHUNK, :]
    q_exp = q * jnp.exp(cum)
    k_end = k * jnp.exp(cum_last - cum)
    dec_all = jnp.exp(cum_last)

    qs_parts = [jnp.zeros((SUB, LANES), F32)]
    ks_parts = []
    for i in range(1, N_SUB):
        ref_i = cum[i * SUB - 1:i * SUB, :]
        qs_parts.append(q[i * SUB:(i + 1) * SUB, :] * jnp.exp(cum[i * SUB:(i + 1) * SUB, :] - ref_i))
        ks_parts.append(k[:i * SUB, :] * jnp.exp(ref_i - cum[:i * SUB, :]))
    qs = jnp.concatenate(qs_parts, axis=0)
    n_used = sum(p.shape[0] for p in ks_parts)
    n_stack = -(-n_used // LANES) * LANES
    ks_parts.append(jnp.zeros((n_stack - n_used, LANES), F32))
    ks = jnp.concatenate(ks_parts, axis=0).astype(BF16)
    row_blk = _div_pow2(_iota((CHUNK, n_stack), 0), SUB)
    col = _iota((CHUNK, n_stack), 1)
    lo = lax.shift_right_logical(row_blk * (row_blk - 1), 1) * SUB
    off_mask = (col >= lo) & (col < lo + row_blk * SUB)

    lane_c = _iota((SUB, CHUNK), 1)
    row_c = _iota((SUB, CHUNK), 0)
    diag_parts = [[] for _ in range(n_heads)]
    for i in range(N_SUB):
        acc = [jnp.zeros((SUB, CHUNK), F32) for _ in range(n_heads)]
        q_i = q[i * SUB:(i + 1) * SUB, :]
        cum_i = cum[i * SUB:(i + 1) * SUB, :]
        for j in range(SUB):
            s = i * SUB + j
            w = q_i * jnp.exp(jnp.minimum(cum_i - cum[s:s + 1, :], 0.0)) * k[s:s + 1, :]
            for hd in range(n_heads):
                a_col = jnp.sum(keep_lanes(hd, w), axis=-1, keepdims=True)
                acc[hd] = jnp.where(lane_c == s, a_col, acc[hd])
        causal = (row_c + i * SUB) >= lane_c
        for hd in range(n_heads):
            diag_parts[hd].append(jnp.where(causal, acc[hd], 0.0))

    outs = []
    for hd in range(n_heads):
        st_ref, st_idx = st_refs[hd]
        v = v_of(hd)
        v_bf = v.astype(BF16)
        st = st_ref[st_idx]
        qs_h = keep_lanes(hd, qs)
        qe_h = keep_lanes(hd, q_exp)
        ke_h = keep_lanes(hd, k_end)
        p_off = jnp.where(off_mask, _dot_nt(qs_h.astype(BF16), ks), 0.0)
        v_stack = jnp.concatenate([v_bf[:i * SUB, :] for i in range(1, N_SUB)]
                                  + [jnp.zeros((n_stack - n_used, LANES), BF16)], axis=0)
        a_diag = jnp.concatenate(diag_parts[hd], axis=0)
        o = (_dot(p_off.astype(BF16), v_stack)
             + _dot(a_diag.astype(BF16), v_bf)
             + _dot_nt(qe_h.astype(BF16), st.astype(BF16)))
        upd = _dot_tn(v_bf, ke_h.astype(BF16))
        dec_h = dec_all
        if lane_ranges[hd] is not None:
            lane = _iota(dec_all.shape, 1)
            dec_h = jnp.where((lane >= lane_ranges[hd][0]) & (lane < lane_ranges[hd][1]), dec_all, 1.0)
        st_ref[st_idx] = st * dec_h + upd
        outs.append(o)
    return outs


def _layer_kernel(layer, final,
                  x_ref, mod_ref, normg_ref, w_in_ref, w_out_ref, cos_ref, sin_ref,
                  lb_logits_ref, hgrn_g_ref, ret_g_ref, conv_w_ref, conv_b_ref, dt_bias_ref,
                  a_log_ref, d_skip_ref, ssm_g_ref, w_gk2_ref, b_gk2_ref, gla_g_ref, final_g_ref,
                  o_ref,
                  proj_ref, y_ref, conv_ref, hk_ref, gcum_ref,
                  st_hgrn, st_ret, st_ssd, st_gla):
    t_idx = pl.program_id(1)

    @pl.when(t_idx == 0)
    def _():
        st_hgrn[...] = jnp.zeros_like(st_hgrn)
        st_ret[...] = jnp.zeros_like(st_ret)
        st_ssd[...] = jnp.zeros_like(st_ssd)
        st_gla[...] = jnp.zeros_like(st_gla)
        conv_ref[0:SUBLANES, :] = jnp.zeros((SUBLANES, SSM_CONV_CH), F32)

    x = x_ref[...]
    shift = mod_ref[0:1, 0:D_MODEL]
    scale = mod_ref[0:1, D_MODEL:2 * D_MODEL]
    gate = mod_ref[0:1, 2 * D_MODEL:3 * D_MODEL]
    h = (_rms(x, D_MODEL) * normg_ref[...]) * (1.0 + scale) + shift
    h_bf = h.astype(BF16)
    n_col_blk = 1024
    for c0 in range(0, PROJ_W, n_col_blk):
        c1 = min(c0 + n_col_blk, PROJ_W)
        proj_ref[:, c0:c1] = _dot(h_bf, w_in_ref[:, c0:c1])

    tri_chunk = _lower_tri(TILE, CHUNK)
    tri_tile = _lower_tri(TILE, TILE)

    lg = [lb_logits_ref[i:i + 1, :] for i in range(DEPTH)]
    lg_max = functools.reduce(jnp.maximum, lg)
    lg_exp = [jnp.exp(r - lg_max) for r in lg]
    lg_den = functools.reduce(lambda a, b: a + b, lg_exp)
    lower = jnp.zeros((1, BRANCH_W), F32)
    for i in range(1, layer + 1):
        lower = lower + lg_exp[i] / lg_den
    log_lb = jnp.log(lower)
    log1m_lb = jnp.log1p(-lower)
    proj_ref[:, C_AQ:C_AQ + BRANCH_W] = _silu(proj_ref[:, C_AQ:C_AQ + BRANCH_W])
    log_f = _logaddexp(log_lb, log1m_lb + _log_sigmoid(proj_ref[:, C_AF:C_AF + BRANCH_W]))
    hk_ref[...] = 1.0 - jnp.exp(log_f)
    proj_ref[:, C_AF:C_AF + BRANCH_W] = _dot3(tri_chunk, log_f)

    tail = proj_ref[:, C_TAIL:C_TAIL + LANES]
    gk_gate = _dot(tail.astype(BF16), w_gk2_ref[...].astype(BF16)) + b_gk2_ref[...]
    gcum_ref[...] = _dot3(tri_chunk, _log_sigmoid(gk_gate) * (1.0 / GLA_GATE_TEMP))
    gla_dk = GLA_KEY_W // N_HEADS
    proj_ref[:, C_GQ:C_GQ + GLA_KEY_W] = proj_ref[:, C_GQ:C_GQ + GLA_KEY_W] * (gla_dk ** -0.5)

    pair_lanes = [(0, LANES // 2), (LANES // 2, LANES)]

    def chunk_body(ci, carry):
        r0 = pl.multiple_of(ci * CHUNK, CHUNK)
        rows = pl.ds(r0, CHUNK)
        for hd in range(N_HEADS):
            cs = slice(hd * HEAD_W, (hd + 1) * HEAD_W)
            o, = _vector_decay_chunk(
                proj_ref[rows, C_AQ + cs.start:C_AQ + cs.stop],
                hk_ref[rows, cs],
                lambda _i, cs=cs: proj_ref[rows, C_AI + cs.start:C_AI + cs.stop],
                proj_ref[rows, C_AF + cs.start:C_AF + cs.stop],
                [(st_hgrn, hd)], [None])
            g_act = _silu(proj_ref[rows, C_AG + cs.start:C_AG + cs.stop])
            y_ref[rows, cs] = (_rms(o, HEAD_W) * hgrn_g_ref[:, cs] * g_act).astype(BF16)
        for pr in range(N_HEADS // 2):
            ks_ = slice(pr * LANES, (pr + 1) * LANES)
            outs = _vector_decay_chunk(
                proj_ref[rows, C_GQ + ks_.start:C_GQ + ks_.stop],
                proj_ref[rows, C_GK + ks_.start:C_GK + ks_.stop],
                lambda i, pr=pr: proj_ref[rows, C_GV + (2 * pr + i) * HEAD_W:C_GV + (2 * pr + i + 1) * HEAD_W],
                gcum_ref[rows, ks_],
                [(st_gla, pr), (st_gla, pr)], pair_lanes)
            for i, o in enumerate(outs):
                cs = slice((2 * pr + i) * HEAD_W, (2 * pr + i + 1) * HEAD_W)
                g_act = _silu(proj_ref[rows, C_GG + cs.start:C_GG + cs.stop])
                y_ref[rows, 3 * BRANCH_W + cs.start:3 * BRANCH_W + cs.stop] = (
                    _rms(o, HEAD_W) * gla_g_ref[:, cs] * g_act).astype(BF16)
        return carry

    lax.fori_loop(0, TILE // CHUNK, chunk_body, 0)

    t_col = _iota((TILE, TILE), 0)
    s_row = _iota((TILE, TILE), 1)
    t_minus_s = (t_col - s_row).astype(F32)
    causal_tile = t_col >= s_row
    t_plus1 = (_iota((TILE, HEAD_W), 0) + 1).astype(F32)
    s_to_end = (TILE - 1 - _iota((TILE, HEAD_W), 0)).astype(F32)
    cos_t = cos_ref[...]
    sin_t = sin_ref[...]
    for hd in range(N_HEADS):
        cs = slice(hd * HEAD_W, (hd + 1) * HEAD_W)
        log_gamma = math.log1p(-(2.0 ** -(RET_DECAY_EXP0 + hd)))
        rq = proj_ref[:, C_RQ + cs.start:C_RQ + cs.stop]
        rk = proj_ref[:, C_RK + cs.start:C_RK + cs.stop]
        q = rq * cos_t + pltpu.roll(rq, HEAD_W // 2, 1) * sin_t
        k = (rk * cos_t + pltpu.roll(rk, HEAD_W // 2, 1) * sin_t) * (HEAD_W ** -0.5)
        v_bf = proj_ref[:, C_RV + cs.start:C_RV + cs.stop].astype(BF16)
        q_bf = q.astype(BF16)
        decay = jnp.where(causal_tile, jnp.exp(jnp.where(causal_tile, t_minus_s, 0.0) * log_gamma), 0.0)
        scores = _dot_nt(q_bf, k.astype(BF16)) * decay
        st = st_ret[hd]
        o = (_dot(scores.astype(BF16), v_bf)
             + jnp.exp(t_plus1 * log_gamma) * _dot(q_bf, st.astype(BF16)))
        k_end = k * jnp.exp(s_to_end * log_gamma)
        st_ret[hd] = st * math.exp(TILE * log_gamma) + _dot_tn(k_end.astype(BF16), v_bf)
        g_act = _silu(proj_ref[:, C_RG + cs.start:C_RG + cs.stop])
        y_ref[:, BRANCH_W + cs.start:BRANCH_W + cs.stop] = (
            _rms(o, HEAD_W) * ret_g_ref[:, cs] * g_act).astype(BF16)

    conv_ref[SUBLANES:SUBLANES + TILE, :] = proj_ref[:, C_XBC:C_XBC + SSM_CONV_CH]
    xbc = conv_b_ref[...] + conv_w_ref[SSM_CONV - 1:SSM_CONV, :] * conv_ref[SUBLANES:SUBLANES + TILE, :]
    for j in range(SSM_CONV - 1):
        off = SUBLANES - (SSM_CONV - 1) + j
        xbc = xbc + conv_w_ref[j:j + 1, :] * conv_ref[off:off + TILE, :]
    conv_ref[0:SUBLANES, :] = conv_ref[TILE:TILE + SUBLANES, :]
    xbc = _silu(xbc)
    xs = xbc[:, 0:BRANCH_W]
    dt = _softplus(tail + dt_bias_ref[...])
    log_a = dt * (-jnp.exp(a_log_ref[...]))
    cum = _dot3(tri_tile, log_a)
    cum_t = cum.T
    e_r = _iota((LANES, BRANCH_W), 0)
    e_c = _iota((LANES, BRANCH_W), 1)
    expand = jnp.where(e_r == _div_pow2(e_c, SSM_HEAD_DIM), 1.0, 0.0).astype(BF16)
    dt_e = _dot3_rhs(dt, expand)
    cum_e = _dot3_rhs(cum, expand)
    cum_last_e = cum_e[TILE - 1:TILE, :]
    d_e = _dot3_rhs(jnp.broadcast_to(d_skip_ref[...], (SUBLANES, LANES)), expand)[0:1, :]
    xdt = xs * dt_e
    x_end = (xdt * jnp.exp(cum_last_e - cum_e)).astype(BF16)
    hpg = SSM_HEADS // SSM_GROUPS
    gw = hpg * SSM_HEAD_DIM
    o_parts = []
    for g in range(SSM_GROUPS):
        b_bf = xbc[:, BRANCH_W + g * SSM_STATE:BRANCH_W + (g + 1) * SSM_STATE].astype(BF16)
        c_bf = xbc[:, BRANCH_W + (SSM_GROUPS + g) * SSM_STATE:
                   BRANCH_W + (SSM_GROUPS + g + 1) * SSM_STATE].astype(BF16)
        cb = _dot_nt(c_bf, b_bf)
        st = st_ssd[g]
        inter = _dot(c_bf, st.astype(BF16)) * jnp.exp(cum_e[:, g * gw:(g + 1) * gw])
        for hh in range(hpg):
            hd = g * hpg + hh
            rel = cum[:, hd:hd + 1] - cum_t[hd:hd + 1, :]
            w = jnp.where(causal_tile, cb * jnp.exp(jnp.minimum(rel, 0.0)), 0.0)
            o_parts.append(_dot(w.astype(BF16), xdt[:, hd * SSM_HEAD_DIM:(hd + 1) * SSM_HEAD_DIM].astype(BF16))
                           + inter[:, hh * SSM_HEAD_DIM:(hh + 1) * SSM_HEAD_DIM])
        st_ssd[g] = (st * jnp.exp(cum_last_e[:, g * gw:(g + 1) * gw])
                     + _dot_tn(b_bf, x_end[:, g * gw:(g + 1) * gw]))
    o_ssd = jnp.concatenate(o_parts, axis=1)
    y_ssd = (o_ssd + d_e * xs) * _silu(proj_ref[:, C_MZ:C_MZ + BRANCH_W])
    gn = BRANCH_W // SSM_GROUPS
    for g in range(SSM_GROUPS):
        cs = slice(g * gn, (g + 1) * gn)
        y_ref[:, 2 * BRANCH_W + cs.start:2 * BRANCH_W + cs.stop] = (
            _rms(y_ssd[:, cs], gn) * ssm_g_ref[:, cs]).astype(BF16)

    x_new = x + gate * _dot(y_ref[...], w_out_ref[...])
    if final:
        x_new = _rms(x_new, D_MODEL) * final_g_ref[...]
    o_ref[...] = x_new


def _dot3_rhs(x, m_bf16):
    hi = x.astype(BF16)
    r1 = x - hi.astype(F32)
    mid = r1.astype(BF16)
    lo = (r1 - mid.astype(F32)).astype(BF16)
    return _dot(hi, m_bf16) + _dot(mid, m_bf16) + _dot(lo, m_bf16)


def _resident(shape):
    nd = len(shape)
    return pl.BlockSpec(shape, lambda b, t, _nd=nd: (0,) * _nd, pipeline_mode=pl.Buffered(1))


def _layer_call(layer, final, x, mod, params):
    batch, seq, _ = x.shape
    n_tiles = seq // TILE
    small = [params[k] for k in (
        "lb_logits", "hgrn_g", "ret_g", "conv_w", "conv_b", "dt_bias", "a_log", "d_skip",
        "ssm_g", "w_gk2", "b_gk2", "gla_g", "final_g")]
    in_specs = [
        pl.BlockSpec((None, TILE, D_MODEL), lambda b, t: (b, t, 0)),
        pl.BlockSpec((None, SUBLANES, 3 * D_MODEL), lambda b, t: (b, 0, 0)),
        _resident((1, D_MODEL)),
        _resident((D_MODEL, PROJ_W)),
        _resident((D_INNER, D_MODEL)),
        pl.BlockSpec((TILE, HEAD_W), lambda b, t: (t, 0)),
        pl.BlockSpec((TILE, HEAD_W), lambda b, t: (t, 0)),
    ] + [_resident(a.shape) for a in small]
    scratch = [
        pltpu.VMEM((TILE, PROJ_W), F32),
        pltpu.VMEM((TILE, D_INNER), BF16),
        pltpu.VMEM((TILE + SUBLANES, SSM_CONV_CH), F32),
        pltpu.VMEM((TILE, BRANCH_W), F32),
        pltpu.VMEM((TILE, GLA_KEY_W), F32),
        pltpu.VMEM((N_HEADS, HEAD_W, HEAD_W), F32),
        pltpu.VMEM((N_HEADS, HEAD_W, HEAD_W), F32),
        pltpu.VMEM((SSM_GROUPS, SSM_STATE, BRANCH_W // SSM_GROUPS), F32),
        pltpu.VMEM((N_HEADS // 2, HEAD_W, LANES), F32),
    ]
    return pl.pallas_call(
        functools.partial(_layer_kernel, layer, final),
        grid=(batch, n_tiles),
        in_specs=in_specs,
        out_specs=pl.BlockSpec((None, TILE, D_MODEL), lambda b, t: (b, t, 0)),
        out_shape=jax.ShapeDtypeStruct(x.shape, F32),
        scratch_shapes=scratch,
        compiler_params=pltpu.CompilerParams(
            dimension_semantics=("arbitrary", "arbitrary"),
            vmem_limit_bytes=VMEM_LIMIT_BYTES),
        name=f"mixer_layer{layer}",
    )(x, mod, params["norm_g"], params["w_in"], params["w_out"], params["cos"], params["sin"], *small)


def _pad_lanes(v, lane0=0):
    return jnp.zeros((1, LANES), F32).at[0, lane0:lane0 + v.shape[0]].set(v)


def kernel(x, c, w_ada, b_ada, norm_g, w_in, hgrn_lb_logits, hgrn_onorm_g, ret_onorm_g, ssm_conv_w,
           ssm_conv_b, ssm_dt_bias, ssm_a_log, ssm_d, ssm_norm_g, gla_w_gk2, gla_b_gk2, gla_onorm_g,
           w_out, final_g):
    batch, seq, _ = x.shape
    c_pad = jnp.zeros((SUBLANES, D_MODEL), F32).at[:batch].set(c)
    mod_all = _ada_mod(c_pad, w_ada, b_ada)

    inv_freq = ROPE_BASE ** (-jnp.arange(0, HEAD_W, 2, dtype=F32) / HEAD_W)
    ang = jnp.arange(seq, dtype=F32)[:, None] * inv_freq[None, :]
    cos_tab = jnp.concatenate([jnp.cos(ang), jnp.cos(ang)], axis=-1)
    sin_tab = jnp.concatenate([-jnp.sin(ang), jnp.sin(ang)], axis=-1)

    for layer in range(DEPTH):
        wl = w_in[layer]
        w_in_p = jnp.concatenate([
            wl[:, :ORIG_DT], wl[:, ORIG_GQ:ORIG_LR], wl[:, ORIG_DT:ORIG_GQ], wl[:, ORIG_LR:],
            jnp.zeros((D_MODEL, LANES - SSM_HEADS - GLA_LOWRANK), F32)], axis=1).astype(BF16)
        w_gk2_p = jnp.zeros((LANES, GLA_KEY_W), F32).at[TAIL_LR:TAIL_LR + GLA_LOWRANK].set(gla_w_gk2[layer])
        params = dict(
            norm_g=norm_g[layer][None, :], w_in=w_in_p, w_out=w_out[layer].astype(BF16),
            cos=cos_tab, sin=sin_tab,
            lb_logits=hgrn_lb_logits, hgrn_g=hgrn_onorm_g[layer][None, :],
            ret_g=ret_onorm_g[layer][None, :], conv_w=ssm_conv_w[layer],
            conv_b=ssm_conv_b[layer][None, :], dt_bias=_pad_lanes(ssm_dt_bias[layer]),
            a_log=_pad_lanes(ssm_a_log[layer]), d_skip=_pad_lanes(ssm_d[layer]),
            ssm_g=ssm_norm_g[layer][None, :], w_gk2=w_gk2_p, b_gk2=gla_b_gk2[layer][None, :],
            gla_g=gla_onorm_g[layer][None, :], final_g=final_g[None, :])
        mod = jnp.broadcast_to(mod_all[layer][:batch, None, :], (batch, SUBLANES, 3 * D_MODEL))
        x = _layer_call(layer, layer == DEPTH - 1, x, mod, params)
    return x
```

```python
import functools
import math

import numpy as np
import jax
import jax.numpy as jnp
from jax import lax
from jax.experimental import pallas as pl
from jax.experimental.pallas import tpu as pltpu

F32 = jnp.float32
BF16 = jnp.bfloat16

D_MODEL = 1024
DEPTH = 2
BRANCH_W = 512
D_INNER = 4 * BRANCH_W
HEAD_W = 128
N_HEADS = 4
RET_DECAY_EXP0 = 5.0
ROPE_BASE = 10000.0
SSM_HEADS = 8
SSM_HEAD_DIM = 64
SSM_GROUPS = 2
SSM_STATE = 128
SSM_CONV = 4
SSM_CONV_CH = BRANCH_W + 2 * SSM_GROUPS * SSM_STATE
GLA_KEY_W = 256
GLA_LOWRANK = 16
GLA_GATE_TEMP = 16.0
EPS = 1e-6

LANES = 128
SUBLANES = 8

C_AQ, C_AF, C_AI, C_AG = 0, 512, 1024, 1536
C_RQ, C_RK, C_RV, C_RG = 2048, 2560, 3072, 3584
C_MZ, C_XBC = 4096, 4608
C_GQ, C_GK, C_GV, C_GG = 5632, 5888, 6144, 6656
C_TAIL = 7168
PROJ_W = C_TAIL + LANES
ORIG_DT = 5632
ORIG_GQ = ORIG_DT + SSM_HEADS
ORIG_LR = ORIG_GQ + 2 * GLA_KEY_W + 2 * BRANCH_W
TAIL_LR = SSM_HEADS

TILE = 256
CHUNK = 64
SUB = 8
N_SUB = CHUNK // SUB
N_KEY_GROUPS = N_HEADS + N_HEADS // 2
VMEM_LIMIT_BYTES = 56 * 1024 * 1024


def _dot(a, b):
    return jnp.dot(a, b, preferred_element_type=F32)


def _dot_nt(a, b):
    return lax.dot_general(a, b, (((1,), (1,)), ((), ())), preferred_element_type=F32)


def _dot_tn(a, b):
    return lax.dot_general(a, b, (((0,), (0,)), ((), ())), preferred_element_type=F32)


def _dot3(m_bf16, x):
    hi = x.astype(BF16)
    r1 = x - hi.astype(F32)
    mid = r1.astype(BF16)
    lo = (r1 - mid.astype(F32)).astype(BF16)
    return _dot(m_bf16, hi) + _dot(m_bf16, mid) + _dot(m_bf16, lo)


def _silu(x):
    return x / (1.0 + jnp.exp(-x))


def _log1p_exp_neg_abs(x):
    return jnp.log1p(jnp.exp(-jnp.abs(x)))


def _log_sigmoid(x):
    return jnp.minimum(x, 0.0) - _log1p_exp_neg_abs(x)


def _softplus(x):
    return jnp.maximum(x, 0.0) + _log1p_exp_neg_abs(x)


def _logaddexp(a, b):
    return jnp.maximum(a, b) + _log1p_exp_neg_abs(a - b)


def _rms(x, width):
    return x * lax.rsqrt(jnp.sum(x * x, axis=-1, keepdims=True) * (1.0 / width) + EPS)


def _iota(shape, dim):
    return lax.broadcasted_iota(jnp.int32, shape, dim)


def _div_pow2(x, d):
    shift = d.bit_length() - 1
    assert 1 << shift == d
    return lax.shift_right_logical(x, shift)


def _lower_tri(n, block):
    r = _iota((n, n), 0)
    c = _iota((n, n), 1)
    keep = (c <= r) & (_div_pow2(r, block) == _div_pow2(c, block))
    return jnp.where(keep, 1.0, 0.0).astype(BF16)


def _ada_kernel(c_ref, w_ref, b_ref, o_ref):
    c_act = _silu(c_ref[...]).astype(BF16)
    o_ref[...] = _dot(c_act, w_ref[...].astype(BF16)) + b_ref[...]


def _ada_mod(c_pad, w_ada, b_ada):
    n_blk = 3
    return pl.pallas_call(
        _ada_kernel,
        grid=(DEPTH, n_blk),
        in_specs=[
            pl.BlockSpec((SUBLANES, D_MODEL), lambda l, j: (0, 0)),
            pl.BlockSpec((None, D_MODEL, D_MODEL), lambda l, j: (l, 0, j)),
            pl.BlockSpec((None, 1, D_MODEL), lambda l, j: (l, 0, j)),
        ],
        out_specs=pl.BlockSpec((None, SUBLANES, D_MODEL), lambda l, j: (l, 0, j)),
        out_shape=jax.ShapeDtypeStruct((DEPTH, SUBLANES, 3 * D_MODEL), F32),
        name="ada_mod",
    )(c_pad, w_ada, b_ada.reshape(DEPTH, 1, 3 * D_MODEL))


def _vector_decay_chunk(q, k, v_of, cum, st_refs, lane_ranges, cum_sc, k_sc):
    n_heads = len(lane_ranges)

    def keep_lanes(hd, x):
        if lane_ranges[hd] is None:
            return x
        lane = _iota(x.shape, 1)
        return jnp.where((lane >= lane_ranges[hd][0]) & (lane < lane_ranges[hd][1]), x, 0.0)

    cum_last = cum[CHUNK - 1:CHUNK, :]
    q_exp = q * jnp.exp(cum)
    k_end = k * jnp.exp(cum_last - cum)
    dec_all = jnp.exp(cum_last)

    qs_parts = [jnp.zeros((SUB, LANES), F32)]
    ks_parts = []
    for i in range(1, N_SUB):
        ref_i = cum[i * SUB - 1:i * SUB, :]
        qs_parts.append(q[i * SUB:(i + 1) * SUB, :] * jnp.exp(cum[i * SUB:(i + 1) * SUB, :] - ref_i))
        ks_parts.append(k[:i * SUB, :] * jnp.exp(ref_i - cum[:i * SUB, :]))
    qs = jnp.concatenate(qs_parts, axis=0)
    n_used = sum(p.shape[0] for p in ks_parts)
    n_stack = -(-n_used // LANES) * LANES
    ks_parts.append(jnp.zeros((n_stack - n_used, LANES), F32))
    ks = jnp.concatenate(ks_parts, axis=0).astype(BF16)
    row_blk = _div_pow2(_iota((CHUNK, n_stack), 0), SUB)
    col = _iota((CHUNK, n_stack), 1)
    lo = lax.shift_right_logical(row_blk * (row_blk - 1), 1) * SUB
    off_mask = (col >= lo) & (col < lo + row_blk * SUB)

    cum_sc[...] = cum
    k_sc[...] = k
    lane_c = _iota((SUB, CHUNK), 1)
    row_c = _iota((SUB, CHUNK), 0)
    diag_parts = [[] for _ in range(n_heads)]
    for i in range(N_SUB):
        acc = [jnp.zeros((SUB, CHUNK), F32) for _ in range(n_heads)]
        q_i = q[i * SUB:(i + 1) * SUB, :]
        cum_i = cum[i * SUB:(i + 1) * SUB, :]
        for j in range(SUB):
            s = i * SUB + j
            w = q_i * jnp.exp(cum_i - cum_sc[s:s + 1, :]) * k_sc[s:s + 1, :]
            for hd in range(n_heads):
                a_col = jnp.sum(keep_lanes(hd, w), axis=-1, keepdims=True)
                acc[hd] = jnp.where(lane_c == s, a_col, acc[hd])
        causal = (row_c + i * SUB) >= lane_c
        for hd in range(n_heads):
            diag_parts[hd].append(jnp.where(causal, acc[hd], 0.0))

    outs = []
    for hd in range(n_heads):
        st_ref, st_idx = st_refs[hd]
        v = v_of(hd)
        v_bf = v.astype(BF16)
        st = st_ref[st_idx]
        qs_h = keep_lanes(hd, qs)
        qe_h = keep_lanes(hd, q_exp)
        ke_h = keep_lanes(hd, k_end)
        p_off = jnp.where(off_mask, _dot_nt(qs_h.astype(BF16), ks), 0.0)
        v_stack = jnp.concatenate([v[:i * SUB, :] for i in range(1, N_SUB)]
                                  + [jnp.zeros((n_stack - n_used, LANES), F32)], axis=0).astype(BF16)
        a_diag = jnp.concatenate(diag_parts[hd], axis=0)
        o = (_dot(p_off.astype(BF16), v_stack)
             + _dot(a_diag.astype(BF16), v_bf)
             + _dot_nt(qe_h.astype(BF16), st.astype(BF16)))
        upd = _dot_tn(v_bf, ke_h.astype(BF16))
        dec_h = dec_all
        if lane_ranges[hd] is not None:
            lane = _iota(dec_all.shape, 1)
            dec_h = jnp.where((lane >= lane_ranges[hd][0]) & (lane < lane_ranges[hd][1]), dec_all, 1.0)
        st_ref[st_idx] = st * dec_h + upd
        outs.append(o)
    return outs


def _layer_kernel(layer, final,
                  x_ref, mod_ref, normg_ref, w_in_ref, w_out_ref, cos_ref, sin_ref,
                  lb_logits_ref, hgrn_g_ref, ret_g_ref, conv_w_ref, conv_b_ref, dt_bias_ref,
                  a_log_ref, d_skip_ref, ssm_g_ref, w_gk2_ref, b_gk2_ref, gla_g_ref, final_g_ref,
                  o_ref,
                  proj_ref, y_ref, conv_ref, hk_ref, gcum_ref, cum_sc, k_sc,
                  st_hgrn, st_ret, st_ssd, st_gla):
    t_idx = pl.program_id(1)

    @pl.when(t_idx == 0)
    def _():
        st_hgrn[...] = jnp.zeros_like(st_hgrn)
        st_ret[...] = jnp.zeros_like(st_ret)
        st_ssd[...] = jnp.zeros_like(st_ssd)
        st_gla[...] = jnp.zeros_like(st_gla)
        conv_ref[0:SUBLANES, :] = jnp.zeros((SUBLANES, SSM_CONV_CH), F32)

    x = x_ref[...]
    shift = mod_ref[0:1, 0:D_MODEL]
    scale = mod_ref[0:1, D_MODEL:2 * D_MODEL]
    gate = mod_ref[0:1, 2 * D_MODEL:3 * D_MODEL]
    h = (_rms(x, D_MODEL) * normg_ref[...]) * (1.0 + scale) + shift
    h_bf = h.astype(BF16)
    n_col_blk = 1024
    for c0 in range(0, PROJ_W, n_col_blk):
        c1 = min(c0 + n_col_blk, PROJ_W)
        proj_ref[:, c0:c1] = _dot(h_bf, w_in_ref[:, c0:c1])

    tri_chunk = _lower_tri(TILE, CHUNK)
    tri_tile = _lower_tri(TILE, TILE)

    lg = [lb_logits_ref[i:i + 1, :] for i in range(DEPTH)]
    lg_max = functools.reduce(jnp.maximum, lg)
    lg_exp = [jnp.exp(r - lg_max) for r in lg]
    lg_den = functools.reduce(lambda a, b: a + b, lg_exp)
    lower = jnp.zeros((1, BRANCH_W), F32)
    for i in range(1, layer + 1):
        lower = lower + lg_exp[i] / lg_den
    log_lb = jnp.log(lower)
    log1m_lb = jnp.log1p(-lower)
    proj_ref[:, C_AQ:C_AQ + BRANCH_W] = _silu(proj_ref[:, C_AQ:C_AQ + BRANCH_W])
    log_f = _logaddexp(log_lb, log1m_lb + _log_sigmoid(proj_ref[:, C_AF:C_AF + BRANCH_W]))
    hk_ref[...] = 1.0 - jnp.exp(log_f)
    proj_ref[:, C_AF:C_AF + BRANCH_W] = _dot3(tri_chunk, log_f)

    tail = proj_ref[:, C_TAIL:C_TAIL + LANES]
    gk_gate = _dot(tail.astype(BF16), w_gk2_ref[...].astype(BF16)) + b_gk2_ref[...]
    gcum_ref[...] = _dot3(tri_chunk, _log_sigmoid(gk_gate) * (1.0 / GLA_GATE_TEMP))
    gla_dk = GLA_KEY_W // N_HEADS
    proj_ref[:, C_GQ:C_GQ + GLA_KEY_W] = proj_ref[:, C_GQ:C_GQ + GLA_KEY_W] * (gla_dk ** -0.5)

    pair_lanes = [(0, LANES // 2), (LANES // 2, LANES)]

    def chunk_body(ci, carry):
        r0 = pl.multiple_of(ci * CHUNK, CHUNK)
        rows = pl.ds(r0, CHUNK)
        for hd in range(N_HEADS):
            cs = slice(hd * HEAD_W, (hd + 1) * HEAD_W)
            o, = _vector_decay_chunk(
                proj_ref[rows, C_AQ + cs.start:C_AQ + cs.stop],
                hk_ref[rows, cs],
                lambda _i, cs=cs: proj_ref[rows, C_AI + cs.start:C_AI + cs.stop],
                proj_ref[rows, C_AF + cs.start:C_AF + cs.stop],
                [(st_hgrn, hd)], [None], cum_sc.at[hd], k_sc.at[hd])
            g_act = _silu(proj_ref[rows, C_AG + cs.start:C_AG + cs.stop])
            y_ref[rows, cs] = (_rms(o, HEAD_W) * hgrn_g_ref[:, cs] * g_act).astype(BF16)
        for pr in range(N_HEADS // 2):
            ks_ = slice(pr * LANES, (pr + 1) * LANES)
            outs = _vector_decay_chunk(
                proj_ref[rows, C_GQ + ks_.start:C_GQ + ks_.stop],
                proj_ref[rows, C_GK + ks_.start:C_GK + ks_.stop],
                lambda i, pr=pr: proj_ref[rows, C_GV + (2 * pr + i) * HEAD_W:C_GV + (2 * pr + i + 1) * HEAD_W],
                gcum_ref[rows, ks_],
                [(st_gla, pr), (st_gla, pr)], pair_lanes,
                cum_sc.at[N_HEADS + pr], k_sc.at[N_HEADS + pr])
            for i, o in enumerate(outs):
                cs = slice((2 * pr + i) * HEAD_W, (2 * pr + i + 1) * HEAD_W)
                g_act = _silu(proj_ref[rows, C_GG + cs.start:C_GG + cs.stop])
                y_ref[rows, 3 * BRANCH_W + cs.start:3 * BRANCH_W + cs.stop] = (
                    _rms(o, HEAD_W) * gla_g_ref[:, cs] * g_act).astype(BF16)
        return carry

    lax.fori_loop(0, TILE // CHUNK, chunk_body, 0)

    t_col = _iota((TILE, TILE), 0)
    s_row = _iota((TILE, TILE), 1)
    t_minus_s = (t_col - s_row).astype(F32)
    causal_tile = t_col >= s_row
    t_plus1 = (_iota((TILE, HEAD_W), 0) + 1).astype(F32)
    s_to_end = (TILE - 1 - _iota((TILE, HEAD_W), 0)).astype(F32)
    cos_t = cos_ref[...]
    sin_t = sin_ref[...]
    for hd in range(N_HEADS):
        cs = slice(hd * HEAD_W, (hd + 1) * HEAD_W)
        log_gamma = math.log1p(-(2.0 ** -(RET_DECAY_EXP0 + hd)))
        rq = proj_ref[:, C_RQ + cs.start:C_RQ + cs.stop]
        rk = proj_ref[:, C_RK + cs.start:C_RK + cs.stop]
        q = rq * cos_t + pltpu.roll(rq, HEAD_W // 2, 1) * sin_t
        k = (rk * cos_t + pltpu.roll(rk, HEAD_W // 2, 1) * sin_t) * (HEAD_W ** -0.5)
        v_bf = proj_ref[:, C_RV + cs.start:C_RV + cs.stop].astype(BF16)
        q_bf = q.astype(BF16)
        decay = jnp.where(causal_tile, jnp.exp(jnp.where(causal_tile, t_minus_s, 0.0) * log_gamma), 0.0)
        scores = _dot_nt(q_bf, k.astype(BF16)) * decay
        st = st_ret[hd]
        o = (_dot(scores.astype(BF16), v_bf)
             + jnp.exp(t_plus1 * log_gamma) * _dot(q_bf, st.astype(BF16)))
        k_end = k * jnp.exp(s_to_end * log_gamma)
        st_ret[hd] = st * math.exp(TILE * log_gamma) + _dot_tn(k_end.astype(BF16), v_bf)
        g_act = _silu(proj_ref[:, C_RG + cs.start:C_RG + cs.stop])
        y_ref[:, BRANCH_W + cs.start:BRANCH_W + cs.stop] = (
            _rms(o, HEAD_W) * ret_g_ref[:, cs] * g_act).astype(BF16)

    conv_ref[SUBLANES:SUBLANES + TILE, :] = proj_ref[:, C_XBC:C_XBC + SSM_CONV_CH]
    xbc = conv_b_ref[...] + conv_w_ref[SSM_CONV - 1:SSM_CONV, :] * conv_ref[SUBLANES:SUBLANES + TILE, :]
    for j in range(SSM_CONV - 1):
        off = SUBLANES - (SSM_CONV - 1) + j
        xbc = xbc + conv_w_ref[j:j + 1, :] * conv_ref[off:off + TILE, :]
    conv_ref[0:SUBLANES, :] = conv_ref[TILE:TILE + SUBLANES, :]
    xbc = _silu(xbc)
    xs = xbc[:, 0:BRANCH_W]
    dt = _softplus(tail + dt_bias_ref[...])
    log_a = dt * (-jnp.exp(a_log_ref[...]))
    cum = _dot3(tri_tile, log_a)
    cum_t = cum.T
    e_r = _iota((LANES, BRANCH_W), 0)
    e_c = _iota((LANES, BRANCH_W), 1)
    expand = jnp.where(e_r == _div_pow2(e_c, SSM_HEAD_DIM), 1.0, 0.0).astype(BF16)
    dt_e = _dot3_rhs(dt, expand)
    cum_e = _dot3_rhs(cum, expand)
    cum_last_e = cum_e[TILE - 1:TILE, :]
    d_e = _dot3_rhs(jnp.broadcast_to(d_skip_ref[...], (SUBLANES, LANES)), expand)[0:1, :]
    xdt = xs * dt_e
    x_end = (xdt * jnp.exp(cum_last_e - cum_e)).astype(BF16)
    hpg = SSM_HEADS // SSM_GROUPS
    gw = hpg * SSM_HEAD_DIM
    o_parts = []
    for g in range(SSM_GROUPS):
        b_bf = xbc[:, BRANCH_W + g * SSM_STATE:BRANCH_W + (g + 1) * SSM_STATE].astype(BF16)
        c_bf = xbc[:, BRANCH_W + (SSM_GROUPS + g) * SSM_STATE:
                   BRANCH_W + (SSM_GROUPS + g + 1) * SSM_STATE].astype(BF16)
        cb = _dot_nt(c_bf, b_bf)
        st = st_ssd[g]
        inter = _dot(c_bf, st.astype(BF16)) * jnp.exp(cum_e[:, g * gw:(g + 1) * gw])
        for hh in range(hpg):
            hd = g * hpg + hh
            rel = cum[:, hd:hd + 1] - cum_t[hd:hd + 1, :]
            w = jnp.where(causal_tile, cb * jnp.exp(jnp.minimum(rel, 0.0)), 0.0)
            o_parts.append(_dot(w.astype(BF16), xdt[:, hd * SSM_HEAD_DIM:(hd + 1) * SSM_HEAD_DIM].astype(BF16))
                           + inter[:, hh * SSM_HEAD_DIM:(hh + 1) * SSM_HEAD_DIM])
        st_ssd[g] = (st * jnp.exp(cum_last_e[:, g * gw:(g + 1) * gw])
                     + _dot_tn(b_bf, x_end[:, g * gw:(g + 1) * gw]))
    o_ssd = jnp.concatenate(o_parts, axis=1)
    y_ssd = (o_ssd + d_e * xs) * _silu(proj_ref[:, C_MZ:C_MZ + BRANCH_W])
    gn = BRANCH_W // SSM_GROUPS
    for g in range(SSM_GROUPS):
        cs = slice(g * gn, (g + 1) * gn)
        y_ref[:, 2 * BRANCH_W + cs.start:2 * BRANCH_W + cs.stop] = (
            _rms(y_ssd[:, cs], gn) * ssm_g_ref[:, cs]).astype(BF16)

    x_new = x + gate * _dot(y_ref[...], w_out_ref[...])
    if final:
        x_new = _rms(x_new, D_MODEL) * final_g_ref[...]
    o_ref[...] = x_new


def _dot3_rhs(x, m_bf16):
    hi = x.astype(BF16)
    r1 = x - hi.astype(F32)
    mid = r1.astype(BF16)
    lo = (r1 - mid.astype(F32)).astype(BF16)
    return _dot(hi, m_bf16) + _dot(mid, m_bf16) + _dot(lo, m_bf16)


def _resident(shape):
    nd = len(shape)
    return pl.BlockSpec(shape, lambda b, t, _nd=nd: (0,) * _nd, pipeline_mode=pl.Buffered(1))


def _layer_call(layer, final, x, mod, params):
    batch, seq, _ = x.shape
    n_tiles = seq // TILE
    small = [params[k] for k in (
        "lb_logits", "hgrn_g", "ret_g", "conv_w", "conv_b", "dt_bias", "a_log", "d_skip",
        "ssm_g", "w_gk2", "b_gk2", "gla_g", "final_g")]
    in_specs = [
        pl.BlockSpec((None, TILE, D_MODEL), lambda b, t: (b, t, 0)),
        pl.BlockSpec((None, SUBLANES, 3 * D_MODEL), lambda b, t: (b, 0, 0)),
        _resident((1, D_MODEL)),
        _resident((D_MODEL, PROJ_W)),
        _resident((D_INNER, D_MODEL)),
        pl.BlockSpec((TILE, HEAD_W), lambda b, t: (t, 0)),
        pl.BlockSpec((TILE, HEAD_W), lambda b, t: (t, 0)),
    ] + [_resident(a.shape) for a in small]
    scratch = [
        pltpu.VMEM((TILE, PROJ_W), F32),
        pltpu.VMEM((TILE, D_INNER), BF16),
        pltpu.VMEM((TILE + SUBLANES, SSM_CONV_CH), F32),
        pltpu.VMEM((TILE, BRANCH_W), F32),
        pltpu.VMEM((TILE, GLA_KEY_W), F32),
        pltpu.VMEM((N_KEY_GROUPS, CHUNK, LANES), F32),
        pltpu.VMEM((N_KEY_GROUPS, CHUNK, LANES), F32),
        pltpu.VMEM((N_HEADS, HEAD_W, HEAD_W), F32),
        pltpu.VMEM((N_HEADS, HEAD_W, HEAD_W), F32),
        pltpu.VMEM((SSM_GROUPS, SSM_STATE, BRANCH_W // SSM_GROUPS), F32),
        pltpu.VMEM((N_HEADS // 2, HEAD_W, LANES), F32),
    ]
    return pl.pallas_call(
        functools.partial(_layer_kernel, layer, final),
        grid=(batch, n_tiles),
        in_specs=in_specs,
        out_specs=pl.BlockSpec((None, TILE, D_MODEL), lambda b, t: (b, t, 0)),
        out_shape=jax.ShapeDtypeStruct(x.shape, F32),
        scratch_shapes=scratch,
        compiler_params=pltpu.CompilerParams(
            dimension_semantics=("arbitrary", "arbitrary"),
            vmem_limit_bytes=VMEM_LIMIT_BYTES),
        name=f"mixer_layer{layer}",
    )(x, mod, params["norm_g"], params["w_in"], params["w_out"], params["cos"], params["sin"], *small)


def _pad_lanes(v, lane0=0):
    return jnp.zeros((1, LANES), F32).at[0, lane0:lane0 + v.shape[0]].set(v)


def kernel(x, c, w_ada, b_ada, norm_g, w_in, hgrn_lb_logits, hgrn_onorm_g, ret_onorm_g, ssm_conv_w,
           ssm_conv_b, ssm_dt_bias, ssm_a_log, ssm_d, ssm_norm_g, gla_w_gk2, gla_b_gk2, gla_onorm_g,
           w_out, final_g):
    batch, seq, _ = x.shape
    c_pad = jnp.zeros((SUBLANES, D_MODEL), F32).at[:batch].set(c)
    mod_all = _ada_mod(c_pad, w_ada, b_ada)

    inv_freq = ROPE_BASE ** (-jnp.arange(0, HEAD_W, 2, dtype=F32) / HEAD_W)
    ang = jnp.arange(seq, dtype=F32)[:, None] * inv_freq[None, :]
    cos_tab = jnp.concatenate([jnp.cos(ang), jnp.cos(ang)], axis=-1)
    sin_tab = jnp.concatenate([-jnp.sin(ang), jnp.sin(ang)], axis=-1)

    for layer in range(DEPTH):
        wl = w_in[layer]
        w_in_p = jnp.concatenate([
            wl[:, :ORIG_DT], wl[:, ORIG_GQ:ORIG_LR], wl[:, ORIG_DT:ORIG_GQ], wl[:, ORIG_LR:],
            jnp.zeros((D_MODEL, LANES - SSM_HEADS - GLA_LOWRANK), F32)], axis=1).astype(BF16)
        w_gk2_p = jnp.zeros((LANES, GLA_KEY_W), F32).at[TAIL_LR:TAIL_LR + GLA_LOWRANK].set(gla_w_gk2[layer])
        params = dict(
            norm_g=norm_g[layer][None, :], w_in=w_in_p, w_out=w_out[layer].astype(BF16),
            cos=cos_tab, sin=sin_tab,
            lb_logits=hgrn_lb_logits, hgrn_g=hgrn_onorm_g[layer][None, :],
            ret_g=ret_onorm_g[layer][None, :], conv_w=ssm_conv_w[layer],
            conv_b=ssm_conv_b[layer][None, :], dt_bias=_pad_lanes(ssm_dt_bias[layer]),
            a_log=_pad_lanes(ssm_a_log[layer]), d_skip=_pad_lanes(ssm_d[layer]),
            ssm_g=ssm_norm_g[layer][None, :], w_gk2=w_gk2_p, b_gk2=gla_b_gk2[layer][None, :],
            gla_g=gla_onorm_g[layer][None, :], final_g=final_g[None, :])
        mod = jnp.broadcast_to(mod_all[layer][:batch, None, :], (batch, SUBLANES, 3 * D_MODEL))
        x = _layer_call(layer, layer == DEPTH - 1, x, mod, params)
    return x
```

```python
import functools
import math

import numpy as np
import jax
import jax.numpy as jnp
from jax import lax
from jax.experimental import pallas as pl
from jax.experimental.pallas import tpu as pltpu

F32 = jnp.float32
BF16 = jnp.bfloat16

D_MODEL = 1024
DEPTH = 2
BRANCH_W = 512
D_INNER = 4 * BRANCH_W
HEAD_W = 128
N_HEADS = 4
RET_DECAY_EXP0 = 5.0
ROPE_BASE = 10000.0
SSM_HEADS = 8
SSM_HEAD_DIM = 64
SSM_GROUPS = 2
SSM_STATE = 128
SSM_CONV = 4
SSM_CONV_CH = BRANCH_W + 2 * SSM_GROUPS * SSM_STATE
GLA_KEY_W = 256
GLA_LOWRANK = 16
GLA_GATE_TEMP = 16.0
EPS = 1e-6
LOG2_E = math.log2(math.e)

LANES = 128
SUBLANES = 8

C_AQ, C_AF, C_AI, C_AG = 0, 512, 1024, 1536
C_RQ, C_RK, C_RV, C_RG = 2048, 2560, 3072, 3584
C_MZ, C_XBC = 4096, 4608
C_GQ, C_GK, C_GV, C_GG = 5632, 5888, 6144, 6656
C_TAIL = 7168
PROJ_W = C_TAIL + LANES
ORIG_DT = 5632
ORIG_GQ = ORIG_DT + SSM_HEADS
ORIG_LR = ORIG_GQ + 2 * GLA_KEY_W + 2 * BRANCH_W
TAIL_LR = SSM_HEADS

TILE = 256
CHUNK = 64
SUB = 8
N_SUB = CHUNK // SUB
N_KEY_GROUPS = N_HEADS + N_HEADS // 2
VMEM_LIMIT_BYTES = 56 * 1024 * 1024


def _dot(a, b):
    return jnp.dot(a, b, preferred_element_type=F32)


def _dot_nt(a, b):
    return lax.dot_general(a, b, (((1,), (1,)), ((), ())), preferred_element_type=F32)


def _dot_tn(a, b):
    return lax.dot_general(a, b, (((0,), (0,)), ((), ())), preferred_element_type=F32)


def _dot3(m_bf16, x):
    hi = x.astype(BF16)
    r1 = x - hi.astype(F32)
    mid = r1.astype(BF16)
    lo = (r1 - mid.astype(F32)).astype(BF16)
    return _dot(m_bf16, hi) + _dot(m_bf16, mid) + _dot(m_bf16, lo)


def _silu(x):
    hx = 0.5 * x
    return hx + hx * jnp.tanh(hx)


def _log1p_exp_neg_abs(x):
    return jnp.log(1.0 + jnp.exp(-jnp.abs(x)))


def _log_sigmoid(x):
    return jnp.minimum(x, 0.0) - _log1p_exp_neg_abs(x)


def _softplus(x):
    return jnp.maximum(x, 0.0) + _log1p_exp_neg_abs(x)


def _logaddexp(a, b):
    return jnp.maximum(a, b) + _log1p_exp_neg_abs(a - b)


def _rms(x, width):
    return x * lax.rsqrt(jnp.sum(x * x, axis=-1, keepdims=True) * (1.0 / width) + EPS)


def _iota(shape, dim):
    return lax.broadcasted_iota(jnp.int32, shape, dim)


def _div_pow2(x, d):
    shift = d.bit_length() - 1
    assert 1 << shift == d
    return lax.shift_right_logical(x, shift)


def _lower_tri(n, block):
    r = _iota((n, n), 0)
    c = _iota((n, n), 1)
    keep = (c <= r) & (_div_pow2(r, block) == _div_pow2(c, block))
    return jnp.where(keep, 1.0, 0.0).astype(BF16)


def _ada_kernel(c_ref, w_ref, b_ref, o_ref):
    c_act = _silu(c_ref[...]).astype(BF16)
    o_ref[...] = _dot(c_act, w_ref[...].astype(BF16)) + b_ref[...]


def _ada_mod(c_pad, w_ada, b_ada):
    n_blk = 3
    return pl.pallas_call(
        _ada_kernel,
        grid=(DEPTH, n_blk),
        in_specs=[
            pl.BlockSpec((SUBLANES, D_MODEL), lambda l, j: (0, 0)),
            pl.BlockSpec((None, D_MODEL, D_MODEL), lambda l, j: (l, 0, j)),
            pl.BlockSpec((None, 1, D_MODEL), lambda l, j: (l, 0, j)),
        ],
        out_specs=pl.BlockSpec((None, SUBLANES, D_MODEL), lambda l, j: (l, 0, j)),
        out_shape=jax.ShapeDtypeStruct((DEPTH, SUBLANES, 3 * D_MODEL), F32),
        name="ada_mod",
    )(c_pad, w_ada, b_ada.reshape(DEPTH, 1, 3 * D_MODEL))


def _vector_decay_chunk(q, k, v_of, cum, st_refs, lane_ranges, cum_sc, k_sc):
    n_heads = len(lane_ranges)

    def keep_lanes(hd, x):
        if lane_ranges[hd] is None:
            return x
        lane = _iota(x.shape, 1)
        return jnp.where((lane >= lane_ranges[hd][0]) & (lane < lane_ranges[hd][1]), x, 0.0)

    cum_sc[...] = cum
    k_sc[...] = k
    cum_last = cum_sc[CHUNK - 1:CHUNK, :]
    q_exp = q * jnp.exp2(cum)
    k_end = k * jnp.exp2(cum_last - cum)
    dec_all = jnp.exp2(cum_last)

    qs_parts = [jnp.zeros((SUB, LANES), F32)]
    ks_parts = []
    for i in range(1, N_SUB):
        ref_i = cum_sc[i * SUB - 1:i * SUB, :]
        qs_parts.append(q[i * SUB:(i + 1) * SUB, :] * jnp.exp2(cum[i * SUB:(i + 1) * SUB, :] - ref_i))
        ks_parts.append(k[:i * SUB, :] * jnp.exp2(ref_i - cum[:i * SUB, :]))
    qs = jnp.concatenate(qs_parts, axis=0)
    n_used = sum(p.shape[0] for p in ks_parts)
    n_stack = -(-n_used // LANES) * LANES
    ks_parts.append(jnp.zeros((n_stack - n_used, LANES), F32))
    ks = jnp.concatenate(ks_parts, axis=0).astype(BF16)
    row_blk = _div_pow2(_iota((CHUNK, n_stack), 0), SUB)
    col = _iota((CHUNK, n_stack), 1)
    lo = lax.shift_right_logical(row_blk * (row_blk - 1), 1) * SUB
    off_mask = (col >= lo) & (col < lo + row_blk * SUB)

    lane_c = _iota((SUB, CHUNK), 1)
    row_c = _iota((SUB, CHUNK), 0)
    diag_parts = [[] for _ in range(n_heads)]
    for i in range(N_SUB):
        acc = [jnp.zeros((SUB, CHUNK), F32) for _ in range(n_heads)]
        q_i = q[i * SUB:(i + 1) * SUB, :]
        cum_i = cum[i * SUB:(i + 1) * SUB, :]
        for j in range(SUB):
            s = i * SUB + j
            w = q_i * jnp.exp2(cum_i - cum_sc[s:s + 1, :]) * k_sc[s:s + 1, :]
            for hd in range(n_heads):
                a_col = jnp.sum(keep_lanes(hd, w), axis=-1, keepdims=True)
                acc[hd] = jnp.where(lane_c == s, a_col, acc[hd])
        causal = (row_c + i * SUB) >= lane_c
        for hd in range(n_heads):
            diag_parts[hd].append(jnp.where(causal, acc[hd], 0.0))

    outs = []
    for hd in range(n_heads):
        st_ref, st_idx = st_refs[hd]
        v = v_of(hd)
        v_bf = v.astype(BF16)
        st = st_ref[st_idx]
        qs_h = keep_lanes(hd, qs)
        qe_h = keep_lanes(hd, q_exp)
        ke_h = keep_lanes(hd, k_end)
        p_off = jnp.where(off_mask, _dot_nt(qs_h.astype(BF16), ks), 0.0)
        v_stack = jnp.concatenate([v[:i * SUB, :] for i in range(1, N_SUB)]
                                  + [jnp.zeros((n_stack - n_used, LANES), F32)], axis=0).astype(BF16)
        a_diag = jnp.concatenate(diag_parts[hd], axis=0)
        o = (_dot(p_off.astype(BF16), v_stack)
             + _dot(a_diag.astype(BF16), v_bf)
             + _dot_nt(qe_h.astype(BF16), st.astype(BF16)))
        upd = _dot_tn(v_bf, ke_h.astype(BF16))
        dec_h = dec_all
        if lane_ranges[hd] is not None:
            lane = _iota(dec_all.shape, 1)
            dec_h = jnp.where((lane >= lane_ranges[hd][0]) & (lane < lane_ranges[hd][1]), dec_all, 1.0)
        st_ref[st_idx] = st * dec_h + upd
        outs.append(o)
    return outs


def _layer_kernel(layer, final,
                  x_ref, mod_ref, normg_ref, w_in_ref, w_out_ref, cos_ref, sin_ref,
                  lb_logits_ref, hgrn_g_ref, ret_g_ref, conv_w_ref, conv_b_ref, dt_bias_ref,
                  a_log_ref, d_skip_ref, ssm_g_ref, w_gk2_ref, b_gk2_ref, gla_g_ref, final_g_ref,
                  o_ref,
                  proj_ref, y_ref, conv_ref, hk_ref, gcum_ref, cum_sc, k_sc,
                  st_hgrn, st_ret, st_ssd, st_gla):
    t_idx = pl.program_id(1)

    @pl.when(t_idx == 0)
    def _():
        st_hgrn[...] = jnp.zeros_like(st_hgrn)
        st_ret[...] = jnp.zeros_like(st_ret)
        st_ssd[...] = jnp.zeros_like(st_ssd)
        st_gla[...] = jnp.zeros_like(st_gla)
        conv_ref[...] = jnp.zeros_like(conv_ref)

    x = x_ref[...]
    shift = mod_ref[0:1, 0:D_MODEL]
    scale = mod_ref[0:1, D_MODEL:2 * D_MODEL]
    gate = mod_ref[0:1, 2 * D_MODEL:3 * D_MODEL]
    h = (_rms(x, D_MODEL) * normg_ref[...]) * (1.0 + scale) + shift
    h_bf = h.astype(BF16)
    n_col_blk = 1024
    for c0 in range(0, PROJ_W, n_col_blk):
        c1 = min(c0 + n_col_blk, PROJ_W)
        proj_ref[:, c0:c1] = _dot(h_bf, w_in_ref[:, c0:c1])

    tri_chunk = _lower_tri(TILE, CHUNK)
    tri_tile = _lower_tri(TILE, TILE)

    lg = [lb_logits_ref[i:i + 1, :] for i in range(DEPTH)]
    lg_max = functools.reduce(jnp.maximum, lg)
    lg_exp = [jnp.exp(r - lg_max) for r in lg]
    lg_den = functools.reduce(lambda a, b: a + b, lg_exp)
    lower = jnp.zeros((1, BRANCH_W), F32)
    for i in range(1, layer + 1):
        lower = lower + lg_exp[i] / lg_den
    log_lb = jnp.log(lower)
    log1m_lb = jnp.log1p(-lower)
    proj_ref[:, C_AQ:C_AQ + BRANCH_W] = _silu(proj_ref[:, C_AQ:C_AQ + BRANCH_W])
    log_f = _logaddexp(log_lb, log1m_lb + _log_sigmoid(proj_ref[:, C_AF:C_AF + BRANCH_W]))
    hk_ref[...] = 1.0 - jnp.exp(log_f)
    proj_ref[:, C_AF:C_AF + BRANCH_W] = _dot3(tri_chunk, log_f * LOG2_E)

    tail = proj_ref[:, C_TAIL:C_TAIL + LANES]
    gk_gate = _dot(tail.astype(BF16), w_gk2_ref[...].astype(BF16)) + b_gk2_ref[...]
    gcum_ref[...] = _dot3(tri_chunk, _log_sigmoid(gk_gate) * (LOG2_E / GLA_GATE_TEMP))
    gla_dk = GLA_KEY_W // N_HEADS
    proj_ref[:, C_GQ:C_GQ + GLA_KEY_W] = proj_ref[:, C_GQ:C_GQ + GLA_KEY_W] * (gla_dk ** -0.5)

    pair_lanes = [(0, LANES // 2), (LANES // 2, LANES)]

    def chunk_body(ci, carry):
        r0 = ci * CHUNK
        rows = slice(r0, r0 + CHUNK)
        for hd in range(N_HEADS):
            cs = slice(hd * HEAD_W, (hd + 1) * HEAD_W)
            o, = _vector_decay_chunk(
                proj_ref[rows, C_AQ + cs.start:C_AQ + cs.stop],
                hk_ref[rows, cs],
                lambda _i, cs=cs: proj_ref[rows, C_AI + cs.start:C_AI + cs.stop],
                proj_ref[rows, C_AF + cs.start:C_AF + cs.stop],
                [(st_hgrn, hd)], [None], cum_sc.at[hd], k_sc.at[hd])
            g_act = _silu(proj_ref[rows, C_AG + cs.start:C_AG + cs.stop])
            y_ref[rows, cs] = (_rms(o, HEAD_W) * hgrn_g_ref[:, cs] * g_act).astype(BF16)
        for pr in range(N_HEADS // 2):
            ks_ = slice(pr * LANES, (pr + 1) * LANES)
            outs = _vector_decay_chunk(
                proj_ref[rows, C_GQ + ks_.start:C_GQ + ks_.stop],
                proj_ref[rows, C_GK + ks_.start:C_GK + ks_.stop],
                lambda i, pr=pr: proj_ref[rows, C_GV + (2 * pr + i) * HEAD_W:C_GV + (2 * pr + i + 1) * HEAD_W],
                gcum_ref[rows, ks_],
                [(st_gla, pr), (st_gla, pr)], pair_lanes,
                cum_sc.at[N_HEADS + pr], k_sc.at[N_HEADS + pr])
            for i, o in enumerate(outs):
                cs = slice((2 * pr + i) * HEAD_W, (2 * pr + i + 1) * HEAD_W)
                g_act = _silu(proj_ref[rows, C_GG + cs.start:C_GG + cs.stop])
                y_ref[rows, 3 * BRANCH_W + cs.start:3 * BRANCH_W + cs.stop] = (
                    _rms(o, HEAD_W) * gla_g_ref[:, cs] * g_act).astype(BF16)
        return carry

    for ci in range(TILE // CHUNK):
        chunk_body(ci, 0)

    t_col = _iota((TILE, TILE), 0)
    s_row = _iota((TILE, TILE), 1)
    t_minus_s = (t_col - s_row).astype(F32)
    causal_tile = t_col >= s_row
    t_plus1 = (_iota((TILE, HEAD_W), 0) + 1).astype(F32)
    s_to_end = (TILE - 1 - _iota((TILE, HEAD_W), 0)).astype(F32)
    cos_t = cos_ref[...]
    sin_t = sin_ref[...]
    for hd in range(N_HEADS):
        cs = slice(hd * HEAD_W, (hd + 1) * HEAD_W)
        log2_gamma = math.log1p(-(2.0 ** -(RET_DECAY_EXP0 + hd))) * LOG2_E
        rq = proj_ref[:, C_RQ + cs.start:C_RQ + cs.stop]
        rk = proj_ref[:, C_RK + cs.start:C_RK + cs.stop]
        q = rq * cos_t + pltpu.roll(rq, HEAD_W // 2, 1) * sin_t
        k = (rk * cos_t + pltpu.roll(rk, HEAD_W // 2, 1) * sin_t) * (HEAD_W ** -0.5)
        v_bf = proj_ref[:, C_RV + cs.start:C_RV + cs.stop].astype(BF16)
        q_bf = q.astype(BF16)
        decay = jnp.where(causal_tile, jnp.exp2(t_minus_s * log2_gamma), 0.0)
        scores = _dot_nt(q_bf, k.astype(BF16)) * decay
        st = st_ret[hd]
        o = (_dot(scores.astype(BF16), v_bf)
             + jnp.exp2(t_plus1 * log2_gamma) * _dot(q_bf, st.astype(BF16)))
        k_end = k * jnp.exp2(s_to_end * log2_gamma)
        st_ret[hd] = st * (2.0 ** (TILE * log2_gamma)) + _dot_tn(k_end.astype(BF16), v_bf)
        g_act = _silu(proj_ref[:, C_RG + cs.start:C_RG + cs.stop])
        y_ref[:, BRANCH_W + cs.start:BRANCH_W + cs.stop] = (
            _rms(o, HEAD_W) * ret_g_ref[:, cs] * g_act).astype(BF16)

    u = proj_ref[:, C_XBC:C_XBC + SSM_CONV_CH]
    prev = conv_ref[...]
    row8 = _iota((SUBLANES, SSM_CONV_CH), 0)
    xbc = conv_b_ref[...] + conv_w_ref[SSM_CONV - 1:SSM_CONV, :] * u
    for d in range(1, SSM_CONV):
        u_d = pltpu.roll(u, d, 0)
        head = jnp.where(row8 < d, pltpu.roll(prev, d, 0), u_d[0:SUBLANES, :])
        u_d = jnp.concatenate([head, u_d[SUBLANES:, :]], axis=0)
        xbc = xbc + conv_w_ref[SSM_CONV - 1 - d:SSM_CONV - d, :] * u_d
    conv_ref[...] = u[TILE - SUBLANES:TILE, :]
    xbc = _silu(xbc)
    xs = xbc[:, 0:BRANCH_W]
    dt = _softplus(tail + dt_bias_ref[...])
    log_a = dt * (-LOG2_E * jnp.exp(a_log_ref[...]))
    cum = _dot3(tri_tile, log_a)
    cum_t = cum.T
    e_r = _iota((LANES, BRANCH_W), 0)
    e_c = _iota((LANES, BRANCH_W), 1)
    expand = jnp.where(e_r == _div_pow2(e_c, SSM_HEAD_DIM), 1.0, 0.0).astype(BF16)
    dt_e = _dot3_rhs(dt, expand)
    cum_e = _dot3_rhs(cum, expand)
    cum_last_e = cum_e[TILE - 1:TILE, :]
    d_e = _dot3_rhs(jnp.broadcast_to(d_skip_ref[...], (SUBLANES, LANES)), expand)[0:1, :]
    xdt = xs * dt_e
    x_end = (xdt * jnp.exp2(cum_last_e - cum_e)).astype(BF16)
    hpg = SSM_HEADS // SSM_GROUPS
    gw = hpg * SSM_HEAD_DIM
    o_parts = []
    for g in range(SSM_GROUPS):
        b_bf = xbc[:, BRANCH_W + g * SSM_STATE:BRANCH_W + (g + 1) * SSM_STATE].astype(BF16)
        c_bf = xbc[:, BRANCH_W + (SSM_GROUPS + g) * SSM_STATE:
                   BRANCH_W + (SSM_GROUPS + g + 1) * SSM_STATE].astype(BF16)
        cb = _dot_nt(c_bf, b_bf)
        st = st_ssd[g]
        inter = _dot(c_bf, st.astype(BF16)) * jnp.exp2(cum_e[:, g * gw:(g + 1) * gw])
        for hh in range(hpg):
            hd = g * hpg + hh
            rel = cum[:, hd:hd + 1] - cum_t[hd:hd + 1, :]
            w = jnp.where(causal_tile, cb * jnp.exp2(rel), 0.0)
            o_parts.append(_dot(w.astype(BF16), xdt[:, hd * SSM_HEAD_DIM:(hd + 1) * SSM_HEAD_DIM].astype(BF16))
                           + inter[:, hh * SSM_HEAD_DIM:(hh + 1) * SSM_HEAD_DIM])
        st_ssd[g] = (st * jnp.exp2(cum_last_e[:, g * gw:(g + 1) * gw])
                     + _dot_tn(b_bf, x_end[:, g * gw:(g + 1) * gw]))
    o_ssd = jnp.concatenate(o_parts, axis=1)
    y_ssd = (o_ssd + d_e * xs) * _silu(proj_ref[:, C_MZ:C_MZ + BRANCH_W])
    gn = BRANCH_W // SSM_GROUPS
    for g in range(SSM_GROUPS):
        cs = slice(g * gn, (g + 1) * gn)
        y_ref[:, 2 * BRANCH_W + cs.start:2 * BRANCH_W + cs.stop] = (
            _rms(y_ssd[:, cs], gn) * ssm_g_ref[:, cs]).astype(BF16)

    x_new = x + gate * _dot(y_ref[...], w_out_ref[...])
    if final:
        x_new = _rms(x_new, D_MODEL) * final_g_ref[...]
    o_ref[...] = x_new


def _dot3_rhs(x, m_bf16):
    hi = x.astype(BF16)
    r1 = x - hi.astype(F32)
    mid = r1.astype(BF16)
    lo = (r1 - mid.astype(F32)).astype(BF16)
    return _dot(hi, m_bf16) + _dot(mid, m_bf16) + _dot(lo, m_bf16)


def _resident(shape):
    nd = len(shape)
    return pl.BlockSpec(shape, lambda b, t, _nd=nd: (0,) * _nd, pipeline_mode=pl.Buffered(1))


def _layer_call(layer, final, x, mod, params):
    batch, seq, _ = x.shape
    n_tiles = seq // TILE
    small = [params[k] for k in (
        "lb_logits", "hgrn_g", "ret_g", "conv_w", "conv_b", "dt_bias", "a_log", "d_skip",
        "ssm_g", "w_gk2", "b_gk2", "gla_g", "final_g")]
    in_specs = [
        pl.BlockSpec((None, TILE, D_MODEL), lambda b, t: (b, t, 0)),
        pl.BlockSpec((None, SUBLANES, 3 * D_MODEL), lambda b, t: (b, 0, 0)),
        _resident((1, D_MODEL)),
        _resident((D_MODEL, PROJ_W)),
        _resident((D_INNER, D_MODEL)),
        pl.BlockSpec((TILE, HEAD_W), lambda b, t: (t, 0)),
        pl.BlockSpec((TILE, HEAD_W), lambda b, t: (t, 0)),
    ] + [_resident(a.shape) for a in small]
    scratch = [
        pltpu.VMEM((TILE, PROJ_W), F32),
        pltpu.VMEM((TILE, D_INNER), BF16),
        pltpu.VMEM((SUBLANES, SSM_CONV_CH), F32),
        pltpu.VMEM((TILE, BRANCH_W), F32),
        pltpu.VMEM((TILE, GLA_KEY_W), F32),
        pltpu.VMEM((N_KEY_GROUPS, CHUNK, LANES), F32),
        pltpu.VMEM((N_KEY_GROUPS, CHUNK, LANES), F32),
        pltpu.VMEM((N_HEADS, HEAD_W, HEAD_W), F32),
        pltpu.VMEM((N_HEADS, HEAD_W, HEAD_W), F32),
        pltpu.VMEM((SSM_GROUPS, SSM_STATE, BRANCH_W // SSM_GROUPS), F32),
        pltpu.VMEM((N_HEADS // 2, HEAD_W, LANES), F32),
    ]
    return pl.pallas_call(
        functools.partial(_layer_kernel, layer, final),
        grid=(batch, n_tiles),
        in_specs=in_specs,
        out_specs=pl.BlockSpec((None, TILE, D_MODEL), lambda b, t: (b, t, 0)),
        out_shape=jax.ShapeDtypeStruct(x.shape, F32),
        scratch_shapes=scratch,
        compiler_params=pltpu.CompilerParams(
            dimension_semantics=("arbitrary", "arbitrary"),
            vmem_limit_bytes=VMEM_LIMIT_BYTES),
        name=f"mixer_layer{layer}",
    )(x, mod, params["norm_g"], params["w_in"], params["w_out"], params["cos"], params["sin"], *small)


def _pad_lanes(v, lane0=0):
    return jnp.zeros((1, LANES), F32).at[0, lane0:lane0 + v.shape[0]].set(v)


def kernel(x, c, w_ada, b_ada, norm_g, w_in, hgrn_lb_logits, hgrn_onorm_g, ret_onorm_g, ssm_conv_w,
           ssm_conv_b, ssm_dt_bias, ssm_a_log, ssm_d, ssm_norm_g, gla_w_gk2, gla_b_gk2, gla_onorm_g,
           w_out, final_g):
    batch, seq, _ = x.shape
    c_pad = jnp.zeros((SUBLANES, D_MODEL), F32).at[:batch].set(c)
    mod_all = _ada_mod(c_pad, w_ada, b_ada)

    inv_freq = ROPE_BASE ** (-jnp.arange(0, HEAD_W, 2, dtype=F32) / HEAD_W)
    ang = jnp.arange(seq, dtype=F32)[:, None] * inv_freq[None, :]
    cos_tab = jnp.concatenate([jnp.cos(ang), jnp.cos(ang)], axis=-1)
    sin_tab = jnp.concatenate([-jnp.sin(ang), jnp.sin(ang)], axis=-1)

    for layer in range(DEPTH):
        wl = w_in[layer]
        w_in_p = jnp.concatenate([
            wl[:, :ORIG_DT], wl[:, ORIG_GQ:ORIG_LR], wl[:, ORIG_DT:ORIG_GQ], wl[:, ORIG_LR:],
            jnp.zeros((D_MODEL, LANES - SSM_HEADS - GLA_LOWRANK), F32)], axis=1).astype(BF16)
        w_gk2_p = jnp.zeros((LANES, GLA_KEY_W), F32).at[TAIL_LR:TAIL_LR + GLA_LOWRANK].set(gla_w_gk2[layer])
        params = dict(
            norm_g=norm_g[layer][None, :], w_in=w_in_p, w_out=w_out[layer].astype(BF16),
            cos=cos_tab, sin=sin_tab,
            lb_logits=hgrn_lb_logits, hgrn_g=hgrn_onorm_g[layer][None, :],
            ret_g=ret_onorm_g[layer][None, :], conv_w=ssm_conv_w[layer],
            conv_b=ssm_conv_b[layer][None, :], dt_bias=_pad_lanes(ssm_dt_bias[layer]),
            a_log=_pad_lanes(ssm_a_log[layer]), d_skip=_pad_lanes(ssm_d[layer]),
            ssm_g=ssm_norm_g[layer][None, :], w_gk2=w_gk2_p, b_gk2=gla_b_gk2[layer][None, :],
            gla_g=gla_onorm_g[layer][None, :], final_g=final_g[None, :])
        mod = jnp.broadcast_to(mod_all[layer][:batch, None, :], (batch, SUBLANES, 3 * D_MODEL))
        x = _layer_call(layer, layer == DEPTH - 1, x, mod, params)
    return x
```

```python
import functools
import math

import numpy as np
import jax
import jax.numpy as jnp
from jax import lax
from jax.experimental import pallas as pl
from jax.experimental.pallas import tpu as pltpu

F32 = jnp.float32
BF16 = jnp.bfloat16

D_MODEL = 1024
DEPTH = 2
BRANCH_W = 512
D_INNER = 4 * BRANCH_W
HEAD_W = 128
N_HEADS = 4
RET_DECAY_EXP0 = 5.0
ROPE_BASE = 10000.0
SSM_HEADS = 8
SSM_HEAD_DIM = 64
SSM_GROUPS = 2
SSM_STATE = 128
SSM_CONV = 4
SSM_CONV_CH = BRANCH_W + 2 * SSM_GROUPS * SSM_STATE
GLA_KEY_W = 256
GLA_LOWRANK = 16
GLA_GATE_TEMP = 16.0
EPS = 1e-6
LOG2_E = math.log2(math.e)

LANES = 128
SUBLANES = 8

C_AQ, C_AF, C_AI, C_AG = 0, 512, 1024, 1536
C_GQ, C_GK, C_GV, C_GG = 2048, 2304, 2560, 3072
C_LR = 3584
VEC_W = C_LR + LANES
C_RQ, C_RK, C_RV, C_RG = 3712, 4224, 4736, 5248
C_MZ, C_XBC = 5760, 6272
C_DT = 7296
PROJ_W = C_DT + LANES
ORIG_RQ = 2048
ORIG_DT = 5632
ORIG_GQ = ORIG_DT + SSM_HEADS
ORIG_LR = ORIG_GQ + 2 * GLA_KEY_W + 2 * BRANCH_W

TILE = 256
CHUNK = 64
SUB = 8
N_SUB = CHUNK // SUB
N_KEY_GROUPS = N_HEADS + N_HEADS // 2
VMEM_LIMIT_BYTES = 56 * 1024 * 1024


def _dot(a, b):
    return jnp.dot(a, b, preferred_element_type=F32)


def _dot_nt(a, b):
    return lax.dot_general(a, b, (((1,), (1,)), ((), ())), preferred_element_type=F32)


def _dot_tn(a, b):
    return lax.dot_general(a, b, (((0,), (0,)), ((), ())), preferred_element_type=F32)


def _dot3(m_bf16, x):
    hi = x.astype(BF16)
    r1 = x - hi.astype(F32)
    mid = r1.astype(BF16)
    lo = (r1 - mid.astype(F32)).astype(BF16)
    return _dot(m_bf16, hi) + _dot(m_bf16, mid) + _dot(m_bf16, lo)


def _silu(x):
    hx = 0.5 * x
    return hx + hx * jnp.tanh(hx)


def _log1p_exp_neg_abs(x):
    return jnp.log(1.0 + jnp.exp(-jnp.abs(x)))


def _log_sigmoid(x):
    return jnp.minimum(x, 0.0) - _log1p_exp_neg_abs(x)


def _softplus(x):
    return jnp.maximum(x, 0.0) + _log1p_exp_neg_abs(x)


def _logaddexp(a, b):
    return jnp.maximum(a, b) + _log1p_exp_neg_abs(a - b)


def _rms(x, width):
    return x * lax.rsqrt(jnp.sum(x * x, axis=-1, keepdims=True) * (1.0 / width) + EPS)


def _iota(shape, dim):
    return lax.broadcasted_iota(jnp.int32, shape, dim)


def _div_pow2(x, d):
    shift = d.bit_length() - 1
    assert 1 << shift == d
    return lax.shift_right_logical(x, shift)


def _lower_tri(n, block):
    r = _iota((n, n), 0)
    c = _iota((n, n), 1)
    keep = (c <= r) & (_div_pow2(r, block) == _div_pow2(c, block))
    return jnp.where(keep, 1.0, 0.0).astype(BF16)


def _ada_kernel(c_ref, w_ref, b_ref, o_ref):
    c_act = _silu(c_ref[...]).astype(BF16)
    o_ref[...] = _dot(c_act, w_ref[...].astype(BF16)) + b_ref[...]


def _ada_mod(c_pad, w_ada, b_ada):
    n_blk = 3
    return pl.pallas_call(
        _ada_kernel,
        grid=(DEPTH, n_blk),
        in_specs=[
            pl.BlockSpec((SUBLANES, D_MODEL), lambda l, j: (0, 0)),
            pl.BlockSpec((None, D_MODEL, D_MODEL), lambda l, j: (l, 0, j)),
            pl.BlockSpec((None, 1, D_MODEL), lambda l, j: (l, 0, j)),
        ],
        out_specs=pl.BlockSpec((None, SUBLANES, D_MODEL), lambda l, j: (l, 0, j)),
        out_shape=jax.ShapeDtypeStruct((DEPTH, SUBLANES, 3 * D_MODEL), F32),
        name="ada_mod",
    )(c_pad, w_ada, b_ada.reshape(DEPTH, 1, 3 * D_MODEL))


def _vector_decay_chunk(q, k, v_of, cum, st_refs, lane_ranges, cum_sc, k_sc):
    n_heads = len(lane_ranges)

    def keep_lanes(hd, x):
        if lane_ranges[hd] is None:
            return x
        lane = _iota(x.shape, 1)
        return jnp.where((lane >= lane_ranges[hd][0]) & (lane < lane_ranges[hd][1]), x, 0.0)

    cum_sc[...] = cum
    k_sc[...] = k
    cum_last = cum_sc[CHUNK - 1:CHUNK, :]
    q_exp = q * jnp.exp2(cum)
    k_end = k * jnp.exp2(cum_last - cum)
    dec_all = jnp.exp2(cum_last)

    qs_parts = [jnp.zeros((SUB, LANES), F32)]
    ks_parts = []
    for i in range(1, N_SUB):
        ref_i = cum_sc[i * SUB - 1:i * SUB, :]
        qs_parts.append(q[i * SUB:(i + 1) * SUB, :] * jnp.exp2(cum[i * SUB:(i + 1) * SUB, :] - ref_i))
        ks_parts.append(k[:i * SUB, :] * jnp.exp2(ref_i - cum[:i * SUB, :]))
    qs = jnp.concatenate(qs_parts, axis=0)
    n_used = sum(p.shape[0] for p in ks_parts)
    n_stack = -(-n_used // LANES) * LANES
    ks_parts.append(jnp.zeros((n_stack - n_used, LANES), F32))
    ks = jnp.concatenate(ks_parts, axis=0).astype(BF16)
    row_blk = _div_pow2(_iota((CHUNK, n_stack), 0), SUB)
    col = _iota((CHUNK, n_stack), 1)
    lo = lax.shift_right_logical(row_blk * (row_blk - 1), 1) * SUB
    off_mask = (col >= lo) & (col < lo + row_blk * SUB)

    lane_c = _iota((SUB, CHUNK), 1)
    row_c = _iota((SUB, CHUNK), 0)
    diag_parts = [[] for _ in range(n_heads)]
    for i in range(N_SUB):
        acc = [jnp.zeros((SUB, CHUNK), F32) for _ in range(n_heads)]
        q_i = q[i * SUB:(i + 1) * SUB, :]
        cum_i = cum[i * SUB:(i + 1) * SUB, :]
        for j in range(SUB):
            s = i * SUB + j
            w = q_i * jnp.exp2(cum_i - cum_sc[s:s + 1, :]) * k_sc[s:s + 1, :]
            for hd in range(n_heads):
                a_col = jnp.sum(keep_lanes(hd, w), axis=-1, keepdims=True)
                acc[hd] = jnp.where(lane_c == s, a_col, acc[hd])
        causal = (row_c + i * SUB) >= lane_c
        for hd in range(n_heads):
            diag_parts[hd].append(jnp.where(causal, acc[hd], 0.0))

    outs = []
    for hd in range(n_heads):
        st_ref, st_idx = st_refs[hd]
        v = v_of(hd)
        v_bf = v.astype(BF16)
        st = st_ref[st_idx]
        qs_h = keep_lanes(hd, qs)
        qe_h = keep_lanes(hd, q_exp)
        ke_h = keep_lanes(hd, k_end)
        p_off = jnp.where(off_mask, _dot_nt(qs_h.astype(BF16), ks), 0.0)
        v_stack = jnp.concatenate([v[:i * SUB, :] for i in range(1, N_SUB)]
                                  + [jnp.zeros((n_stack - n_used, LANES), F32)], axis=0).astype(BF16)
        a_diag = jnp.concatenate(diag_parts[hd], axis=0)
        o = (_dot(p_off.astype(BF16), v_stack)
             + _dot(a_diag.astype(BF16), v_bf)
             + _dot_nt(qe_h.astype(BF16), st.astype(BF16)))
        upd = _dot_tn(v_bf, ke_h.astype(BF16))
        dec_h = dec_all
        if lane_ranges[hd] is not None:
            lane = _iota(dec_all.shape, 1)
            dec_h = jnp.where((lane >= lane_ranges[hd][0]) & (lane < lane_ranges[hd][1]), dec_all, 1.0)
        st_ref[st_idx] = st * dec_h + upd
        outs.append(o)
    return outs


def _modulated_norm(x, mod_ref, normg_ref):
    shift = mod_ref[0:1, 0:D_MODEL]
    scale = mod_ref[0:1, D_MODEL:2 * D_MODEL]
    h = (_rms(x, D_MODEL) * normg_ref[...]) * (1.0 + scale) + shift
    return h.astype(BF16)


def _in_proj(h_ref, w_in_ref, proj_ref, col0, col1):
    n_col_blk = 1024
    for c0 in range(col0, col1, n_col_blk):
        c1 = min(c0 + n_col_blk, col1)
        proj_ref[:, c0:c1] = _dot(h_ref[...], w_in_ref[:, c0:c1])


def _mixer_prep(layer, proj_ref, hk_ref, gcum_ref, lb_logits_ref, w_gk2_ref, b_gk2_ref):
    tri_chunk = _lower_tri(TILE, CHUNK)

    lg = [lb_logits_ref[i:i + 1, :] for i in range(DEPTH)]
    lg_max = functools.reduce(jnp.maximum, lg)
    lg_exp = [jnp.exp(r - lg_max) for r in lg]
    lg_den = functools.reduce(lambda a, b: a + b, lg_exp)
    lower = jnp.zeros((1, BRANCH_W), F32)
    for i in range(1, layer + 1):
        lower = lower + lg_exp[i] / lg_den
    log_lb = jnp.log(lower)
    log1m_lb = jnp.log1p(-lower)
    proj_ref[:, C_AQ:C_AQ + BRANCH_W] = _silu(proj_ref[:, C_AQ:C_AQ + BRANCH_W])
    log_f = _logaddexp(log_lb, log1m_lb + _log_sigmoid(proj_ref[:, C_AF:C_AF + BRANCH_W]))
    hk_ref[...] = 1.0 - jnp.exp(log_f)
    proj_ref[:, C_AF:C_AF + BRANCH_W] = _dot3(tri_chunk, log_f * LOG2_E)

    low_rank = proj_ref[:, C_LR:C_LR + LANES]
    gk_gate = _dot(low_rank.astype(BF16), w_gk2_ref[...].astype(BF16)) + b_gk2_ref[...]
    gcum_ref[...] = _dot3(tri_chunk, _log_sigmoid(gk_gate) * (LOG2_E / GLA_GATE_TEMP))
    gla_dk = GLA_KEY_W // N_HEADS
    proj_ref[:, C_GQ:C_GQ + GLA_KEY_W] = proj_ref[:, C_GQ:C_GQ + GLA_KEY_W] * (gla_dk ** -0.5)


def _vector_decay_mixers(proj_ref, y_ref, hk_ref, gcum_ref, cum_sc, k_sc, st_hgrn, st_gla,
                         hgrn_g_ref, gla_g_ref):
    pair_lanes = [(0, LANES // 2), (LANES // 2, LANES)]

    def chunk_body(ci):
        r0 = ci * CHUNK
        rows = slice(r0, r0 + CHUNK)
        for hd in range(N_HEADS):
            cs = slice(hd * HEAD_W, (hd + 1) * HEAD_W)
            o, = _vector_decay_chunk(
                proj_ref[rows, C_AQ + cs.start:C_AQ + cs.stop],
                hk_ref[rows, cs],
                lambda _i, cs=cs: proj_ref[rows, C_AI + cs.start:C_AI + cs.stop],
                proj_ref[rows, C_AF + cs.start:C_AF + cs.stop],
                [(st_hgrn, hd)], [None], cum_sc.at[ci, hd], k_sc.at[ci, hd])
            g_act = _silu(proj_ref[rows, C_AG + cs.start:C_AG + cs.stop])
            y_ref[rows, cs] = (_rms(o, HEAD_W) * hgrn_g_ref[:, cs] * g_act).astype(BF16)
        for pr in range(N_HEADS // 2):
            ks_ = slice(pr * LANES, (pr + 1) * LANES)
            outs = _vector_decay_chunk(
                proj_ref[rows, C_GQ + ks_.start:C_GQ + ks_.stop],
                proj_ref[rows, C_GK + ks_.start:C_GK + ks_.stop],
                lambda i, pr=pr: proj_ref[rows, C_GV + (2 * pr + i) * HEAD_W:C_GV + (2 * pr + i + 1) * HEAD_W],
                gcum_ref[rows, ks_],
                [(st_gla, pr), (st_gla, pr)], pair_lanes,
                cum_sc.at[ci, N_HEADS + pr], k_sc.at[ci, N_HEADS + pr])
            for i, o in enumerate(outs):
                cs = slice((2 * pr + i) * HEAD_W, (2 * pr + i + 1) * HEAD_W)
                g_act = _silu(proj_ref[rows, C_GG + cs.start:C_GG + cs.stop])
                y_ref[rows, 3 * BRANCH_W + cs.start:3 * BRANCH_W + cs.stop] = (
                    _rms(o, HEAD_W) * gla_g_ref[:, cs] * g_act).astype(BF16)

    for ci in range(TILE // CHUNK):
        chunk_body(ci)


def _scalar_decay_mixers(proj_ref, y_ref, conv_ref, st_ret, st_ssd, cos_ref, sin_ref,
                         ret_g_ref, conv_w_ref, conv_b_ref, dt_bias_ref, a_log_ref, d_skip_ref,
                         ssm_g_ref):
    tri_tile = _lower_tri(TILE, TILE)
    tail = proj_ref[:, C_DT:C_DT + LANES]

    t_col = _iota((TILE, TILE), 0)
    s_row = _iota((TILE, TILE), 1)
    t_minus_s = (t_col - s_row).astype(F32)
    causal_tile = t_col >= s_row
    t_plus1 = (_iota((TILE, HEAD_W), 0) + 1).astype(F32)
    s_to_end = (TILE - 1 - _iota((TILE, HEAD_W), 0)).astype(F32)
    cos_t = cos_ref[...]
    sin_t = sin_ref[...]
    for hd in range(N_HEADS):
        cs = slice(hd * HEAD_W, (hd + 1) * HEAD_W)
        log2_gamma = math.log1p(-(2.0 ** -(RET_DECAY_EXP0 + hd))) * LOG2_E
        rq = proj_ref[:, C_RQ + cs.start:C_RQ + cs.stop]
        rk = proj_ref[:, C_RK + cs.start:C_RK + cs.stop]
        q = rq * cos_t + pltpu.roll(rq, HEAD_W // 2, 1) * sin_t
        k = (rk * cos_t + pltpu.roll(rk, HEAD_W // 2, 1) * sin_t) * (HEAD_W ** -0.5)
        v_bf = proj_ref[:, C_RV + cs.start:C_RV + cs.stop].astype(BF16)
        q_bf = q.astype(BF16)
        decay = jnp.where(causal_tile, jnp.exp2(t_minus_s * log2_gamma), 0.0)
        scores = _dot_nt(q_bf, k.astype(BF16)) * decay
        st = st_ret[hd]
        o = (_dot(scores.astype(BF16), v_bf)
             + jnp.exp2(t_plus1 * log2_gamma) * _dot(q_bf, st.astype(BF16)))
        k_end = k * jnp.exp2(s_to_end * log2_gamma)
        st_ret[hd] = st * (2.0 ** (TILE * log2_gamma)) + _dot_tn(k_end.astype(BF16), v_bf)
        g_act = _silu(proj_ref[:, C_RG + cs.start:C_RG + cs.stop])
        y_ref[:, BRANCH_W + cs.start:BRANCH_W + cs.stop] = (
            _rms(o, HEAD_W) * ret_g_ref[:, cs] * g_act).astype(BF16)

    u = proj_ref[:, C_XBC:C_XBC + SSM_CONV_CH]
    prev = conv_ref[...]
    row8 = _iota((SUBLANES, SSM_CONV_CH), 0)
    xbc = conv_b_ref[...] + conv_w_ref[SSM_CONV - 1:SSM_CONV, :] * u
    for d in range(1, SSM_CONV):
        u_d = pltpu.roll(u, d, 0)
        head = jnp.where(row8 < d, pltpu.roll(prev, d, 0), u_d[0:SUBLANES, :])
        u_d = jnp.concatenate([head, u_d[SUBLANES:, :]], axis=0)
        xbc = xbc + conv_w_ref[SSM_CONV - 1 - d:SSM_CONV - d, :] * u_d
    conv_ref[...] = u[TILE - SUBLANES:TILE, :]
    xbc = _silu(xbc)
    xs = xbc[:, 0:BRANCH_W]
    dt = _softplus(tail + dt_bias_ref[...])
    log_a = dt * (-LOG2_E * jnp.exp(a_log_ref[...]))
    cum = _dot3(tri_tile, log_a)
    cum_t = cum.T
    e_r = _iota((LANES, BRANCH_W), 0)
    e_c = _iota((LANES, BRANCH_W), 1)
    expand = jnp.where(e_r == _div_pow2(e_c, SSM_HEAD_DIM), 1.0, 0.0).astype(BF16)
    dt_e = _dot3_rhs(dt, expand)
    cum_e = _dot3_rhs(cum, expand)
    cum_last_e = cum_e[TILE - 1:TILE, :]
    d_e = _dot3_rhs(jnp.broadcast_to(d_skip_ref[...], (SUBLANES, LANES)), expand)[0:1, :]
    xdt = xs * dt_e
    x_end = (xdt * jnp.exp2(cum_last_e - cum_e)).astype(BF16)
    hpg = SSM_HEADS // SSM_GROUPS
    gw = hpg * SSM_HEAD_DIM
    o_parts = []
    for g in range(SSM_GROUPS):
        b_bf = xbc[:, BRANCH_W + g * SSM_STATE:BRANCH_W + (g + 1) * SSM_STATE].astype(BF16)
        c_bf = xbc[:, BRANCH_W + (SSM_GROUPS + g) * SSM_STATE:
                   BRANCH_W + (SSM_GROUPS + g + 1) * SSM_STATE].astype(BF16)
        cb = _dot_nt(c_bf, b_bf)
        st = st_ssd[g]
        inter = _dot(c_bf, st.astype(BF16)) * jnp.exp2(cum_e[:, g * gw:(g + 1) * gw])
        for hh in range(hpg):
            hd = g * hpg + hh
            rel = cum[:, hd:hd + 1] - cum_t[hd:hd + 1, :]
            w = jnp.where(causal_tile, cb * jnp.exp2(rel), 0.0)
            o_parts.append(_dot(w.astype(BF16), xdt[:, hd * SSM_HEAD_DIM:(hd + 1) * SSM_HEAD_DIM].astype(BF16))
                           + inter[:, hh * SSM_HEAD_DIM:(hh + 1) * SSM_HEAD_DIM])
        st_ssd[g] = (st * jnp.exp2(cum_last_e[:, g * gw:(g + 1) * gw])
                     + _dot_tn(b_bf, x_end[:, g * gw:(g + 1) * gw]))
    o_ssd = jnp.concatenate(o_parts, axis=1)
    y_ssd = (o_ssd + d_e * xs) * _silu(proj_ref[:, C_MZ:C_MZ + BRANCH_W])
    gn = BRANCH_W // SSM_GROUPS
    for g in range(SSM_GROUPS):
        cs = slice(g * gn, (g + 1) * gn)
        y_ref[:, 2 * BRANCH_W + cs.start:2 * BRANCH_W + cs.stop] = (
            _rms(y_ssd[:, cs], gn) * ssm_g_ref[:, cs]).astype(BF16)


def _layer_kernel(layer, final,
                  x_ref, xn_ref, mod_ref, normg_ref, w_in_ref, w_out_ref, cos_ref, sin_ref,
                  lb_logits_ref, hgrn_g_ref, ret_g_ref, conv_w_ref, conv_b_ref, dt_bias_ref,
                  a_log_ref, d_skip_ref, ssm_g_ref, w_gk2_ref, b_gk2_ref, gla_g_ref, final_g_ref,
                  o_ref,
                  proj_ref, h_ref, y_ref, conv_ref, hk_ref, gcum_ref, cum_sc, k_sc,
                  st_hgrn, st_ret, st_ssd, st_gla):
    t_idx = pl.program_id(1)

    def project_vector_part(x_tile):
        h_ref[...] = _modulated_norm(x_tile, mod_ref, normg_ref)
        _in_proj(h_ref, w_in_ref, proj_ref, 0, VEC_W)
        _mixer_prep(layer, proj_ref, hk_ref, gcum_ref, lb_logits_ref, w_gk2_ref, b_gk2_ref)

    @pl.when(t_idx == 0)
    def _():
        st_hgrn[...] = jnp.zeros_like(st_hgrn)
        st_ret[...] = jnp.zeros_like(st_ret)
        st_ssd[...] = jnp.zeros_like(st_ssd)
        st_gla[...] = jnp.zeros_like(st_gla)
        conv_ref[...] = jnp.zeros_like(conv_ref)
        project_vector_part(x_ref[...])

    _in_proj(h_ref, w_in_ref, proj_ref, VEC_W, PROJ_W)
    _vector_decay_mixers(proj_ref, y_ref, hk_ref, gcum_ref, cum_sc, k_sc, st_hgrn, st_gla,
                         hgrn_g_ref, gla_g_ref)
    project_vector_part(xn_ref[...])
    _scalar_decay_mixers(proj_ref, y_ref, conv_ref, st_ret, st_ssd, cos_ref, sin_ref,
                         ret_g_ref, conv_w_ref, conv_b_ref, dt_bias_ref, a_log_ref, d_skip_ref,
                         ssm_g_ref)
    gate = mod_ref[0:1, 2 * D_MODEL:3 * D_MODEL]
    x_new = x_ref[...] + gate * _dot(y_ref[...], w_out_ref[...])
    if final:
        x_new = _rms(x_new, D_MODEL) * final_g_ref[...]
    o_ref[...] = x_new


def _dot3_rhs(x, m_bf16):
    hi = x.astype(BF16)
    r1 = x - hi.astype(F32)
    mid = r1.astype(BF16)
    lo = (r1 - mid.astype(F32)).astype(BF16)
    return _dot(hi, m_bf16) + _dot(mid, m_bf16) + _dot(lo, m_bf16)


def _resident(shape):
    nd = len(shape)
    return pl.BlockSpec(shape, lambda b, t, _nd=nd: (0,) * _nd, pipeline_mode=pl.Buffered(1))


def _layer_call(layer, final, x, mod, params):
    batch, seq, _ = x.shape
    n_tiles = seq // TILE
    n_chunks = TILE // CHUNK
    small = [params[k] for k in (
        "lb_logits", "hgrn_g", "ret_g", "conv_w", "conv_b", "dt_bias", "a_log", "d_skip",
        "ssm_g", "w_gk2", "b_gk2", "gla_g", "final_g")]
    in_specs = [
        pl.BlockSpec((None, TILE, D_MODEL), lambda b, t: (b, t, 0)),
        pl.BlockSpec((None, TILE, D_MODEL), lambda b, t: (b, jnp.minimum(t + 1, n_tiles - 1), 0)),
        pl.BlockSpec((None, SUBLANES, 3 * D_MODEL), lambda b, t: (b, 0, 0)),
        _resident((1, D_MODEL)),
        pl.BlockSpec((None, D_MODEL, PROJ_W), lambda b, t: (layer, 0, 0), pipeline_mode=pl.Buffered(1)),
        pl.BlockSpec((None, D_INNER, D_MODEL), lambda b, t: (layer, 0, 0), pipeline_mode=pl.Buffered(1)),
        pl.BlockSpec((TILE, HEAD_W), lambda b, t: (t, 0)),
        pl.BlockSpec((TILE, HEAD_W), lambda b, t: (t, 0)),
    ] + [_resident(a.shape) for a in small]
    scratch = [
        pltpu.VMEM((TILE, PROJ_W), F32),
        pltpu.VMEM((TILE, D_MODEL), BF16),
        pltpu.VMEM((TILE, D_INNER), BF16),
        pltpu.VMEM((SUBLANES, SSM_CONV_CH), F32),
        pltpu.VMEM((TILE, BRANCH_W), F32),
        pltpu.VMEM((TILE, GLA_KEY_W), F32),
        pltpu.VMEM((n_chunks, N_KEY_GROUPS, CHUNK, LANES), F32),
        pltpu.VMEM((n_chunks, N_KEY_GROUPS, CHUNK, LANES), F32),
        pltpu.VMEM((N_HEADS, HEAD_W, HEAD_W), F32),
        pltpu.VMEM((N_HEADS, HEAD_W, HEAD_W), F32),
        pltpu.VMEM((SSM_GROUPS, SSM_STATE, BRANCH_W // SSM_GROUPS), F32),
        pltpu.VMEM((N_HEADS // 2, HEAD_W, LANES), F32),
    ]
    return pl.pallas_call(
        functools.partial(_layer_kernel, layer, final),
        grid=(batch, n_tiles),
        in_specs=in_specs,
        out_specs=pl.BlockSpec((None, TILE, D_MODEL), lambda b, t: (b, t, 0)),
        out_shape=jax.ShapeDtypeStruct(x.shape, F32),
        scratch_shapes=scratch,
        compiler_params=pltpu.CompilerParams(
            dimension_semantics=("arbitrary", "arbitrary"),
            vmem_limit_bytes=VMEM_LIMIT_BYTES),
        name=f"mixer_layer{layer}",
    )(x, x, mod, params["norm_g"], params["w_in"], params["w_out"], params["cos"], params["sin"], *small)


def _pad_lanes(v, lane0=0):
    return jnp.zeros((1, LANES), F32).at[0, lane0:lane0 + v.shape[0]].set(v)


def kernel(x, c, w_ada, b_ada, norm_g, w_in, hgrn_lb_logits, hgrn_onorm_g, ret_onorm_g, ssm_conv_w,
           ssm_conv_b, ssm_dt_bias, ssm_a_log, ssm_d, ssm_norm_g, gla_w_gk2, gla_b_gk2, gla_onorm_g,
           w_out, final_g):
    batch, seq, _ = x.shape
    c_pad = jnp.zeros((SUBLANES, D_MODEL), F32).at[:batch].set(c)
    mod_all = _ada_mod(c_pad, w_ada, b_ada)

    inv_freq = ROPE_BASE ** (-np.arange(0, HEAD_W, 2, dtype=np.float64) / HEAD_W)
    ang = np.arange(seq, dtype=np.float64)[:, None] * inv_freq[None, :]
    cos_tab = jnp.asarray(np.concatenate([np.cos(ang), np.cos(ang)], axis=-1), F32)
    sin_tab = jnp.asarray(np.concatenate([-np.sin(ang), np.sin(ang)], axis=-1), F32)

    w_in_p = jnp.concatenate([
        w_in[:, :, :ORIG_RQ], w_in[:, :, ORIG_GQ:ORIG_LR],
        w_in[:, :, ORIG_LR:], jnp.zeros((DEPTH, D_MODEL, LANES - GLA_LOWRANK), F32),
        w_in[:, :, ORIG_RQ:ORIG_DT],
        w_in[:, :, ORIG_DT:ORIG_GQ], jnp.zeros((DEPTH, D_MODEL, LANES - SSM_HEADS), F32),
    ], axis=2).astype(BF16)
    w_out_bf = w_out.astype(BF16)

    for layer in range(DEPTH):
        w_gk2_p = jnp.zeros((LANES, GLA_KEY_W), F32).at[:GLA_LOWRANK].set(gla_w_gk2[layer])
        params = dict(
            norm_g=norm_g[layer][None, :], w_in=w_in_p, w_out=w_out_bf,
            cos=cos_tab, sin=sin_tab,
            lb_logits=hgrn_lb_logits, hgrn_g=hgrn_onorm_g[layer][None, :],
            ret_g=ret_onorm_g[layer][None, :], conv_w=ssm_conv_w[layer],
            conv_b=ssm_conv_b[layer][None, :], dt_bias=_pad_lanes(ssm_dt_bias[layer]),
            a_log=_pad_lanes(ssm_a_log[layer]), d_skip=_pad_lanes(ssm_d[layer]),
            ssm_g=ssm_norm_g[layer][None, :], w_gk2=w_gk2_p, b_gk2=gla_b_gk2[layer][None, :],
            gla_g=gla_onorm_g[layer][None, :], final_g=final_g[None, :])
        mod = jnp.broadcast_to(mod_all[layer][:batch, None, :], (batch, SUBLANES, 3 * D_MODEL))
        x = _layer_call(layer, layer == DEPTH - 1, x, mod, params)
    return x
```

```python
import functools
import math

import numpy as np
import jax
import jax.numpy as jnp
from jax import lax
from jax.experimental import pallas as pl
from jax.experimental.pallas import tpu as pltpu

F32 = jnp.float32
BF16 = jnp.bfloat16

D_MODEL = 1024
DEPTH = 2
BRANCH_W = 512
D_INNER = 4 * BRANCH_W
HEAD_W = 128
N_HEADS = 4
RET_DECAY_EXP0 = 5.0
ROPE_BASE = 10000.0
SSM_HEADS = 8
SSM_HEAD_DIM = 64
SSM_GROUPS = 2
SSM_STATE = 128
SSM_CONV = 4
SSM_CONV_CH = BRANCH_W + 2 * SSM_GROUPS * SSM_STATE
GLA_KEY_W = 256
GLA_LOWRANK = 16
GLA_GATE_TEMP = 16.0
EPS = 1e-6
LOG2_E = math.log2(math.e)

LANES = 128
SUBLANES = 8

C_AQ, C_AF, C_AI, C_AG = 0, 512, 1024, 1536
C_GQ, C_GK, C_GV, C_GG = 2048, 2304, 2560, 3072
C_LR = 3584
VEC_W = C_LR + LANES
C_RQ, C_RK, C_RV, C_RG = 3712, 4224, 4736, 5248
C_MZ, C_XBC = 5760, 6272
C_DT = 7296
PROJ_W = C_DT + LANES
ORIG_RQ = 2048
ORIG_DT = 5632
ORIG_GQ = ORIG_DT + SSM_HEADS
ORIG_LR = ORIG_GQ + 2 * GLA_KEY_W + 2 * BRANCH_W

TILE = 256
CHUNK = 64
SUB = 8
N_SUB = CHUNK // SUB
N_KEY_GROUPS = N_HEADS + N_HEADS // 2
VMEM_LIMIT_BYTES = 56 * 1024 * 1024


def _dot(a, b):
    return jnp.dot(a, b, preferred_element_type=F32)


def _dot_nt(a, b):
    return lax.dot_general(a, b, (((1,), (1,)), ((), ())), preferred_element_type=F32)


def _dot_tn(a, b):
    return lax.dot_general(a, b, (((0,), (0,)), ((), ())), preferred_element_type=F32)


def _dot3(m_bf16, x):
    hi = x.astype(BF16)
    r1 = x - hi.astype(F32)
    mid = r1.astype(BF16)
    lo = (r1 - mid.astype(F32)).astype(BF16)
    return _dot(m_bf16, hi) + _dot(m_bf16, mid) + _dot(m_bf16, lo)


def _silu(x):
    hx = 0.5 * x
    return hx + hx * jnp.tanh(hx)


def _log1p_exp_neg_abs(x):
    return jnp.log(1.0 + jnp.exp(-jnp.abs(x)))


def _log_sigmoid(x):
    return jnp.minimum(x, 0.0) - _log1p_exp_neg_abs(x)


def _softplus(x):
    return jnp.maximum(x, 0.0) + _log1p_exp_neg_abs(x)


def _logaddexp(a, b):
    return jnp.maximum(a, b) + _log1p_exp_neg_abs(a - b)


def _rms(x, width):
    return x * lax.rsqrt(jnp.sum(x * x, axis=-1, keepdims=True) * (1.0 / width) + EPS)


def _iota(shape, dim):
    return lax.broadcasted_iota(jnp.int32, shape, dim)


def _div_pow2(x, d):
    shift = d.bit_length() - 1
    assert 1 << shift == d
    return lax.shift_right_logical(x, shift)


def _lower_tri(n, block):
    r = _iota((n, n), 0)
    c = _iota((n, n), 1)
    keep = (c <= r) & (_div_pow2(r, block) == _div_pow2(c, block))
    return jnp.where(keep, 1.0, 0.0).astype(BF16)


def _ada_kernel(c_ref, w_ref, b_ref, o_ref):
    c_act = _silu(c_ref[...]).astype(BF16)
    o_ref[...] = _dot(c_act, w_ref[...].astype(BF16)) + b_ref[...]


def _ada_mod(c_pad, w_ada, b_ada):
    n_blk = 3
    return pl.pallas_call(
        _ada_kernel,
        grid=(DEPTH, n_blk),
        in_specs=[
            pl.BlockSpec((SUBLANES, D_MODEL), lambda l, j: (0, 0)),
            pl.BlockSpec((None, D_MODEL, D_MODEL), lambda l, j: (l, 0, j)),
            pl.BlockSpec((None, 1, D_MODEL), lambda l, j: (l, 0, j)),
        ],
        out_specs=pl.BlockSpec((None, SUBLANES, D_MODEL), lambda l, j: (l, 0, j)),
        out_shape=jax.ShapeDtypeStruct((DEPTH, SUBLANES, 3 * D_MODEL), F32),
        name="ada_mod",
    )(c_pad, w_ada, b_ada.reshape(DEPTH, 1, 3 * D_MODEL))


def _w_in_layout_kernel(w_ref, o_ref):
    def put(dst, src0, src1):
        o_ref[:, dst:dst + (src1 - src0)] = w_ref[:, src0:src1].astype(BF16)

    put(C_AQ, 0, ORIG_RQ)
    put(C_GQ, ORIG_GQ, ORIG_LR)
    put(C_RQ, ORIG_RQ, ORIG_DT)
    rows = w_ref.shape[0]
    low_rank = w_ref[:, ORIG_LR:ORIG_LR + GLA_LOWRANK]
    o_ref[:, C_LR:C_LR + LANES] = jnp.concatenate(
        [low_rank, jnp.zeros((rows, LANES - GLA_LOWRANK), F32)], axis=1).astype(BF16)
    dt = w_ref[:, ORIG_DT:ORIG_GQ]
    o_ref[:, C_DT:C_DT + LANES] = jnp.concatenate(
        [dt, jnp.zeros((rows, LANES - SSM_HEADS), F32)], axis=1).astype(BF16)


def _w_in_layout(w_in):
    row_blk = 128
    orig_w = w_in.shape[2]
    return pl.pallas_call(
        _w_in_layout_kernel,
        grid=(DEPTH, D_MODEL // row_blk),
        in_specs=[pl.BlockSpec((None, row_blk, orig_w), lambda l, r: (l, r, 0))],
        out_specs=pl.BlockSpec((None, row_blk, PROJ_W), lambda l, r: (l, r, 0)),
        out_shape=jax.ShapeDtypeStruct((DEPTH, D_MODEL, PROJ_W), BF16),
        name="w_in_layout",
    )(w_in)


def _vector_decay_chunk(q, k, v_of, cum, st_refs, lane_ranges, cum_sc, k_sc):
    n_heads = len(lane_ranges)

    def keep_lanes(hd, x):
        if lane_ranges[hd] is None:
            return x
        lane = _iota(x.shape, 1)
        return jnp.where((lane >= lane_ranges[hd][0]) & (lane < lane_ranges[hd][1]), x, 0.0)

    cum_sc[...] = cum
    k_sc[...] = k
    cum_last = cum_sc[CHUNK - 1:CHUNK, :]
    q_exp = q * jnp.exp2(cum)
    k_end = k * jnp.exp2(cum_last - cum)
    dec_all = jnp.exp2(cum_last)

    qs_parts = [jnp.zeros((SUB, LANES), F32)]
    ks_parts = []
    for i in range(1, N_SUB):
        ref_i = cum_sc[i * SUB - 1:i * SUB, :]
        qs_parts.append(q[i * SUB:(i + 1) * SUB, :] * jnp.exp2(cum[i * SUB:(i + 1) * SUB, :] - ref_i))
        ks_parts.append(k[:i * SUB, :] * jnp.exp2(ref_i - cum[:i * SUB, :]))
    qs = jnp.concatenate(qs_parts, axis=0)
    n_used = sum(p.shape[0] for p in ks_parts)
    n_stack = -(-n_used // LANES) * LANES
    ks_parts.append(jnp.zeros((n_stack - n_used, LANES), F32))
    ks = jnp.concatenate(ks_parts, axis=0).astype(BF16)
    row_blk = _div_pow2(_iota((CHUNK, n_stack), 0), SUB)
    col = _iota((CHUNK, n_stack), 1)
    lo = lax.shift_right_logical(row_blk * (row_blk - 1), 1) * SUB
    off_mask = (col >= lo) & (col < lo + row_blk * SUB)

    lane_c = _iota((SUB, CHUNK), 1)
    row_c = _iota((SUB, CHUNK), 0)
    diag_parts = [[] for _ in range(n_heads)]
    for i in range(N_SUB):
        acc = [jnp.zeros((SUB, CHUNK), F32) for _ in range(n_heads)]
        q_i = q[i * SUB:(i + 1) * SUB, :]
        cum_i = cum[i * SUB:(i + 1) * SUB, :]
        for j in range(SUB):
            s = i * SUB + j
            w = q_i * jnp.exp2(cum_i - cum_sc[s:s + 1, :]) * k_sc[s:s + 1, :]
            for hd in range(n_heads):
                a_col = jnp.sum(keep_lanes(hd, w), axis=-1, keepdims=True)
                acc[hd] = jnp.where(lane_c == s, a_col, acc[hd])
        causal = (row_c + i * SUB) >= lane_c
        for hd in range(n_heads):
            diag_parts[hd].append(jnp.where(causal, acc[hd], 0.0))

    outs = []
    for hd in range(n_heads):
        st_ref, st_idx = st_refs[hd]
        v = v_of(hd)
        v_bf = v.astype(BF16)
        st = st_ref[st_idx]
        qs_h = keep_lanes(hd, qs)
        qe_h = keep_lanes(hd, q_exp)
        ke_h = keep_lanes(hd, k_end)
        p_off = jnp.where(off_mask, _dot_nt(qs_h.astype(BF16), ks), 0.0)
        v_stack = jnp.concatenate([v[:i * SUB, :] for i in range(1, N_SUB)]
                                  + [jnp.zeros((n_stack - n_used, LANES), F32)], axis=0).astype(BF16)
        a_diag = jnp.concatenate(diag_parts[hd], axis=0)
        o = (_dot(p_off.astype(BF16), v_stack)
             + _dot(a_diag.astype(BF16), v_bf)
             + _dot_nt(qe_h.astype(BF16), st.astype(BF16)))
        upd = _dot_tn(v_bf, ke_h.astype(BF16))
        dec_h = dec_all
        if lane_ranges[hd] is not None:
            lane = _iota(dec_all.shape, 1)
            dec_h = jnp.where((lane >= lane_ranges[hd][0]) & (lane < lane_ranges[hd][1]), dec_all, 1.0)
        st_ref[st_idx] = st * dec_h + upd
        outs.append(o)
    return outs


def _modulated_norm(x, mod_ref, normg_ref):
    shift = mod_ref[0:1, 0:D_MODEL]
    scale = mod_ref[0:1, D_MODEL:2 * D_MODEL]
    h = (_rms(x, D_MODEL) * normg_ref[...]) * (1.0 + scale) + shift
    return h.astype(BF16)


def _in_proj(h_ref, w_in_ref, proj_ref, col0, col1):
    n_col_blk = 1024
    for c0 in range(col0, col1, n_col_blk):
        c1 = min(c0 + n_col_blk, col1)
        proj_ref[:, c0:c1] = _dot(h_ref[...], w_in_ref[:, c0:c1])


def _mixer_prep(layer, proj_ref, hk_ref, gcum_ref, lb_logits_ref, w_gk2_ref, b_gk2_ref):
    tri_chunk = _lower_tri(TILE, CHUNK)

    lg = [lb_logits_ref[i:i + 1, :] for i in range(DEPTH)]
    lg_max = functools.reduce(jnp.maximum, lg)
    lg_exp = [jnp.exp(r - lg_max) for r in lg]
    lg_den = functools.reduce(lambda a, b: a + b, lg_exp)
    lower = jnp.zeros((1, BRANCH_W), F32)
    for i in range(1, layer + 1):
        lower = lower + lg_exp[i] / lg_den
    log_lb = jnp.log(lower)
    log1m_lb = jnp.log1p(-lower)
    proj_ref[:, C_AQ:C_AQ + BRANCH_W] = _silu(proj_ref[:, C_AQ:C_AQ + BRANCH_W])
    log_f = _logaddexp(log_lb, log1m_lb + _log_sigmoid(proj_ref[:, C_AF:C_AF + BRANCH_W]))
    hk_ref[...] = 1.0 - jnp.exp(log_f)
    proj_ref[:, C_AF:C_AF + BRANCH_W] = _dot3(tri_chunk, log_f * LOG2_E)

    low_rank = proj_ref[:, C_LR:C_LR + LANES]
    gk_gate = _dot(low_rank.astype(BF16), w_gk2_ref[...].astype(BF16)) + b_gk2_ref[...]
    gcum_ref[...] = _dot3(tri_chunk, _log_sigmoid(gk_gate) * (LOG2_E / GLA_GATE_TEMP))
    gla_dk = GLA_KEY_W // N_HEADS
    proj_ref[:, C_GQ:C_GQ + GLA_KEY_W] = proj_ref[:, C_GQ:C_GQ + GLA_KEY_W] * (gla_dk ** -0.5)


def _vector_decay_mixers(proj_ref, y_ref, hk_ref, gcum_ref, cum_sc, k_sc, st_hgrn, st_gla,
                         hgrn_g_ref, gla_g_ref):
    pair_lanes = [(0, LANES // 2), (LANES // 2, LANES)]

    def chunk_body(ci):
        r0 = ci * CHUNK
        rows = slice(r0, r0 + CHUNK)
        for hd in range(N_HEADS):
            cs = slice(hd * HEAD_W, (hd + 1) * HEAD_W)
            o, = _vector_decay_chunk(
                proj_ref[rows, C_AQ + cs.start:C_AQ + cs.stop],
                hk_ref[rows, cs],
                lambda _i, cs=cs: proj_ref[rows, C_AI + cs.start:C_AI + cs.stop],
                proj_ref[rows, C_AF + cs.start:C_AF + cs.stop],
                [(st_hgrn, hd)], [None], cum_sc.at[ci, hd], k_sc.at[ci, hd])
            g_act = _silu(proj_ref[rows, C_AG + cs.start:C_AG + cs.stop])
            y_ref[rows, cs] = (_rms(o, HEAD_W) * hgrn_g_ref[:, cs] * g_act).astype(BF16)
        for pr in range(N_HEADS // 2):
            ks_ = slice(pr * LANES, (pr + 1) * LANES)
            outs = _vector_decay_chunk(
                proj_ref[rows, C_GQ + ks_.start:C_GQ + ks_.stop],
                proj_ref[rows, C_GK + ks_.start:C_GK + ks_.stop],
                lambda i, pr=pr: proj_ref[rows, C_GV + (2 * pr + i) * HEAD_W:C_GV + (2 * pr + i + 1) * HEAD_W],
                gcum_ref[rows, ks_],
                [(st_gla, pr), (st_gla, pr)], pair_lanes,
                cum_sc.at[ci, N_HEADS + pr], k_sc.at[ci, N_HEADS + pr])
            for i, o in enumerate(outs):
                cs = slice((2 * pr + i) * HEAD_W, (2 * pr + i + 1) * HEAD_W)
                g_act = _silu(proj_ref[rows, C_GG + cs.start:C_GG + cs.stop])
                y_ref[rows, 3 * BRANCH_W + cs.start:3 * BRANCH_W + cs.stop] = (
                    _rms(o, HEAD_W) * gla_g_ref[:, cs] * g_act).astype(BF16)

    for ci in range(TILE // CHUNK):
        chunk_body(ci)


def _scalar_decay_mixers(proj_ref, y_ref, conv_ref, st_ret, st_ssd, cos_ref, sin_ref,
                         ret_g_ref, conv_w_ref, conv_b_ref, dt_bias_ref, a_log_ref, d_skip_ref,
                         ssm_g_ref):
    tri_tile = _lower_tri(TILE, TILE)
    tail = proj_ref[:, C_DT:C_DT + LANES]

    t_col = _iota((TILE, TILE), 0)
    s_row = _iota((TILE, TILE), 1)
    t_minus_s = (t_col - s_row).astype(F32)
    causal_tile = t_col >= s_row
    t_plus1 = (_iota((TILE, HEAD_W), 0) + 1).astype(F32)
    s_to_end = (TILE - 1 - _iota((TILE, HEAD_W), 0)).astype(F32)
    cos_t = cos_ref[...]
    sin_t = sin_ref[...]
    for hd in range(N_HEADS):
        cs = slice(hd * HEAD_W, (hd + 1) * HEAD_W)
        log2_gamma = math.log1p(-(2.0 ** -(RET_DECAY_EXP0 + hd))) * LOG2_E
        rq = proj_ref[:, C_RQ + cs.start:C_RQ + cs.stop]
        rk = proj_ref[:, C_RK + cs.start:C_RK + cs.stop]
        q = rq * cos_t + pltpu.roll(rq, HEAD_W // 2, 1) * sin_t
        k = (rk * cos_t + pltpu.roll(rk, HEAD_W // 2, 1) * sin_t) * (HEAD_W ** -0.5)
        v_bf = proj_ref[:, C_RV + cs.start:C_RV + cs.stop].astype(BF16)
        q_bf = q.astype(BF16)
        decay = jnp.where(causal_tile, jnp.exp2(t_minus_s * log2_gamma), 0.0)
        scores = _dot_nt(q_bf, k.astype(BF16)) * decay
        st = st_ret[hd]
        o = (_dot(scores.astype(BF16), v_bf)
             + jnp.exp2(t_plus1 * log2_gamma) * _dot(q_bf, st.astype(BF16)))
        k_end = k * jnp.exp2(s_to_end * log2_gamma)
        st_ret[hd] = st * (2.0 ** (TILE * log2_gamma)) + _dot_tn(k_end.astype(BF16), v_bf)
        g_act = _silu(proj_ref[:, C_RG + cs.start:C_RG + cs.stop])
        y_ref[:, BRANCH_W + cs.start:BRANCH_W + cs.stop] = (
            _rms(o, HEAD_W) * ret_g_ref[:, cs] * g_act).astype(BF16)

    u = proj_ref[:, C_XBC:C_XBC + SSM_CONV_CH]
    prev = conv_ref[...]
    row8 = _iota((SUBLANES, SSM_CONV_CH), 0)
    xbc = conv_b_ref[...] + conv_w_ref[SSM_CONV - 1:SSM_CONV, :] * u
    for d in range(1, SSM_CONV):
        u_d = pltpu.roll(u, d, 0)
        head = jnp.where(row8 < d, pltpu.roll(prev, d, 0), u_d[0:SUBLANES, :])
        u_d = jnp.concatenate([head, u_d[SUBLANES:, :]], axis=0)
        xbc = xbc + conv_w_ref[SSM_CONV - 1 - d:SSM_CONV - d, :] * u_d
    conv_ref[...] = u[TILE - SUBLANES:TILE, :]
    xbc = _silu(xbc)
    xs = xbc[:, 0:BRANCH_W]
    dt = _softplus(tail + dt_bias_ref[...])
    log_a = dt * (-LOG2_E * jnp.exp(a_log_ref[...]))
    cum = _dot3(tri_tile, log_a)
    cum_t = cum.T
    e_r = _iota((LANES, BRANCH_W), 0)
    e_c = _iota((LANES, BRANCH_W), 1)
    expand = jnp.where(e_r == _div_pow2(e_c, SSM_HEAD_DIM), 1.0, 0.0).astype(BF16)
    dt_e = _dot(dt.astype(BF16), expand)
    cum_e = _dot3_rhs(cum, expand)
    cum_last_e = cum_e[TILE - 1:TILE, :]
    d_e = _dot3_rhs(jnp.broadcast_to(d_skip_ref[...], (SUBLANES, LANES)), expand)[0:1, :]
    xdt = xs * dt_e
    x_end = (xdt * jnp.exp2(cum_last_e - cum_e)).astype(BF16)
    hpg = SSM_HEADS // SSM_GROUPS
    gw = hpg * SSM_HEAD_DIM
    o_parts = []
    for g in range(SSM_GROUPS):
        b_bf = xbc[:, BRANCH_W + g * SSM_STATE:BRANCH_W + (g + 1) * SSM_STATE].astype(BF16)
        c_bf = xbc[:, BRANCH_W + (SSM_GROUPS + g) * SSM_STATE:
                   BRANCH_W + (SSM_GROUPS + g + 1) * SSM_STATE].astype(BF16)
        cb = _dot_nt(c_bf, b_bf)
        st = st_ssd[g]
        inter = _dot(c_bf, st.astype(BF16)) * jnp.exp2(cum_e[:, g * gw:(g + 1) * gw])
        for hh in range(hpg):
            hd = g * hpg + hh
            rel = cum[:, hd:hd + 1] - cum_t[hd:hd + 1, :]
            w = jnp.where(causal_tile, cb * jnp.exp2(rel), 0.0)
            o_parts.append(_dot(w.astype(BF16), xdt[:, hd * SSM_HEAD_DIM:(hd + 1) * SSM_HEAD_DIM].astype(BF16))
                           + inter[:, hh * SSM_HEAD_DIM:(hh + 1) * SSM_HEAD_DIM])
        st_ssd[g] = (st * jnp.exp2(cum_last_e[:, g * gw:(g + 1) * gw])
                     + _dot_tn(b_bf, x_end[:, g * gw:(g + 1) * gw]))
    o_ssd = jnp.concatenate(o_parts, axis=1)
    y_ssd = (o_ssd + d_e * xs) * _silu(proj_ref[:, C_MZ:C_MZ + BRANCH_W])
    gn = BRANCH_W // SSM_GROUPS
    for g in range(SSM_GROUPS):
        cs = slice(g * gn, (g + 1) * gn)
        y_ref[:, 2 * BRANCH_W + cs.start:2 * BRANCH_W + cs.stop] = (
            _rms(y_ssd[:, cs], gn) * ssm_g_ref[:, cs]).astype(BF16)


def _layer_kernel(layer, final,
                  x_ref, xn_ref, mod_ref, modn_ref, normg_ref, w_in_ref, w_out_ref, cos_ref, sin_ref,
                  lb_logits_ref, hgrn_g_ref, ret_g_ref, conv_w_ref, conv_b_ref, dt_bias_ref,
                  a_log_ref, d_skip_ref, ssm_g_ref, w_gk2_ref, b_gk2_ref, gla_g_ref, final_g_ref,
                  o_ref,
                  proj_ref, h_ref, y_ref, conv_ref, hk_ref, gcum_ref, cum_sc, k_sc,
                  st_hgrn, st_ret, st_ssd, st_gla):
    b_idx = pl.program_id(0)
    t_idx = pl.program_id(1)

    def project_vector_part(x_tile, tile_mod_ref):
        h_ref[...] = _modulated_norm(x_tile, tile_mod_ref, normg_ref)
        _in_proj(h_ref, w_in_ref, proj_ref, 0, VEC_W)
        _mixer_prep(layer, proj_ref, hk_ref, gcum_ref, lb_logits_ref, w_gk2_ref, b_gk2_ref)

    @pl.when(t_idx == 0)
    def _():
        st_hgrn[...] = jnp.zeros_like(st_hgrn)
        st_ret[...] = jnp.zeros_like(st_ret)
        st_ssd[...] = jnp.zeros_like(st_ssd)
        st_gla[...] = jnp.zeros_like(st_gla)
        conv_ref[...] = jnp.zeros_like(conv_ref)

    @pl.when((t_idx == 0) & (b_idx == 0))
    def _():
        project_vector_part(x_ref[...], mod_ref)

    _in_proj(h_ref, w_in_ref, proj_ref, VEC_W, PROJ_W)
    _vector_decay_mixers(proj_ref, y_ref, hk_ref, gcum_ref, cum_sc, k_sc, st_hgrn, st_gla,
                         hgrn_g_ref, gla_g_ref)
    project_vector_part(xn_ref[...], modn_ref)
    _scalar_decay_mixers(proj_ref, y_ref, conv_ref, st_ret, st_ssd, cos_ref, sin_ref,
                         ret_g_ref, conv_w_ref, conv_b_ref, dt_bias_ref, a_log_ref, d_skip_ref,
                         ssm_g_ref)
    gate = mod_ref[0:1, 2 * D_MODEL:3 * D_MODEL]
    x_new = x_ref[...] + gate * _dot(y_ref[...], w_out_ref[...])
    if final:
        x_new = _rms(x_new, D_MODEL) * final_g_ref[...]
    o_ref[...] = x_new


def _dot3_rhs(x, m_bf16):
    hi = x.astype(BF16)
    r1 = x - hi.astype(F32)
    mid = r1.astype(BF16)
    lo = (r1 - mid.astype(F32)).astype(BF16)
    return _dot(hi, m_bf16) + _dot(mid, m_bf16) + _dot(lo, m_bf16)


def _resident(shape):
    nd = len(shape)
    return pl.BlockSpec(shape, lambda b, t, _nd=nd: (0,) * _nd, pipeline_mode=pl.Buffered(1))


def _layer_call(layer, final, x, mod, params):
    batch, seq, _ = x.shape
    n_tiles = seq // TILE
    n_chunks = TILE // CHUNK

    def next_tile(t):
        return jnp.where(t == n_tiles - 1, 0, t + 1)

    def next_batch(b, t):
        return jnp.where(t == n_tiles - 1, jnp.minimum(b + 1, batch - 1), b)

    small = [params[k] for k in (
        "lb_logits", "hgrn_g", "ret_g", "conv_w", "conv_b", "dt_bias", "a_log", "d_skip",
        "ssm_g", "w_gk2", "b_gk2", "gla_g", "final_g")]
    in_specs = [
        pl.BlockSpec((None, TILE, D_MODEL), lambda b, t: (b, t, 0)),
        pl.BlockSpec((None, TILE, D_MODEL), lambda b, t: (next_batch(b, t), next_tile(t), 0)),
        pl.BlockSpec((None, SUBLANES, 3 * D_MODEL), lambda b, t: (b, 0, 0)),
        pl.BlockSpec((None, SUBLANES, 3 * D_MODEL), lambda b, t: (next_batch(b, t), 0, 0)),
        _resident((1, D_MODEL)),
        pl.BlockSpec((None, D_MODEL, PROJ_W), lambda b, t: (layer, 0, 0), pipeline_mode=pl.Buffered(1)),
        pl.BlockSpec((None, D_INNER, D_MODEL), lambda b, t: (layer, 0, 0), pipeline_mode=pl.Buffered(1)),
        pl.BlockSpec((TILE, HEAD_W), lambda b, t: (t, 0)),
        pl.BlockSpec((TILE, HEAD_W), lambda b, t: (t, 0)),
    ] + [_resident(a.shape) for a in small]
    scratch = [
        pltpu.VMEM((TILE, PROJ_W), F32),
        pltpu.VMEM((TILE, D_MODEL), BF16),
        pltpu.VMEM((TILE, D_INNER), BF16),
        pltpu.VMEM((SUBLANES, SSM_CONV_CH), F32),
        pltpu.VMEM((TILE, BRANCH_W), F32),
        pltpu.VMEM((TILE, GLA_KEY_W), F32),
        pltpu.VMEM((n_chunks, N_KEY_GROUPS, CHUNK, LANES), F32),
        pltpu.VMEM((n_chunks, N_KEY_GROUPS, CHUNK, LANES), F32),
        pltpu.VMEM((N_HEADS, HEAD_W, HEAD_W), F32),
        pltpu.VMEM((N_HEADS, HEAD_W, HEAD_W), F32),
        pltpu.VMEM((SSM_GROUPS, SSM_STATE, BRANCH_W // SSM_GROUPS), F32),
        pltpu.VMEM((N_HEADS // 2, HEAD_W, LANES), F32),
    ]
    return pl.pallas_call(
        functools.partial(_layer_kernel, layer, final),
        grid=(batch, n_tiles),
        in_specs=in_specs,
        out_specs=pl.BlockSpec((None, TILE, D_MODEL), lambda b, t: (b, t, 0)),
        out_shape=jax.ShapeDtypeStruct(x.shape, F32),
        scratch_shapes=scratch,
        compiler_params=pltpu.CompilerParams(
            dimension_semantics=("arbitrary", "arbitrary"),
            vmem_limit_bytes=VMEM_LIMIT_BYTES),
        name=f"mixer_layer{layer}",
    )(x, x, mod, mod, params["norm_g"], params["w_in"], params["w_out"], params["cos"], params["sin"], *small)


def _pad_lanes(v, lane0=0):
    return jnp.zeros((1, LANES), F32).at[0, lane0:lane0 + v.shape[0]].set(v)


def kernel(x, c, w_ada, b_ada, norm_g, w_in, hgrn_lb_logits, hgrn_onorm_g, ret_onorm_g, ssm_conv_w,
           ssm_conv_b, ssm_dt_bias, ssm_a_log, ssm_d, ssm_norm_g, gla_w_gk2, gla_b_gk2, gla_onorm_g,
           w_out, final_g):
    batch, seq, _ = x.shape
    c_pad = jnp.zeros((SUBLANES, D_MODEL), F32).at[:batch].set(c)
    mod_all = _ada_mod(c_pad, w_ada, b_ada)

    inv_freq = ROPE_BASE ** (-np.arange(0, HEAD_W, 2, dtype=np.float64) / HEAD_W)
    ang = np.arange(seq, dtype=np.float64)[:, None] * inv_freq[None, :]
    cos_tab = jnp.asarray(np.concatenate([np.cos(ang), np.cos(ang)], axis=-1), F32)
    sin_tab = jnp.asarray(np.concatenate([-np.sin(ang), np.sin(ang)], axis=-1), F32)

    w_in_p = _w_in_layout(w_in)
    w_out_bf = w_out.astype(BF16)

    for layer in range(DEPTH):
        w_gk2_p = jnp.zeros((LANES, GLA_KEY_W), F32).at[:GLA_LOWRANK].set(gla_w_gk2[layer])
        params = dict(
            norm_g=norm_g[layer][None, :], w_in=w_in_p, w_out=w_out_bf,
            cos=cos_tab, sin=sin_tab,
            lb_logits=hgrn_lb_logits, hgrn_g=hgrn_onorm_g[layer][None, :],
            ret_g=ret_onorm_g[layer][None, :], conv_w=ssm_conv_w[layer],
            conv_b=ssm_conv_b[layer][None, :], dt_bias=_pad_lanes(ssm_dt_bias[layer]),
            a_log=_pad_lanes(ssm_a_log[layer]), d_skip=_pad_lanes(ssm_d[layer]),
            ssm_g=ssm_norm_g[layer][None, :], w_gk2=w_gk2_p, b_gk2=gla_b_gk2[layer][None, :],
            gla_g=gla_onorm_g[layer][None, :], final_g=final_g[None, :])
        mod = jnp.broadcast_to(mod_all[layer][:batch, None, :], (batch, SUBLANES, 3 * D_MODEL))
        x = _layer_call(layer, layer == DEPTH - 1, x, mod, params)
    return x
```

```python
import functools
import math

import numpy as np
import jax
import jax.numpy as jnp
from jax import lax
from jax.experimental import pallas as pl
from jax.experimental.pallas import tpu as pltpu

F32 = jnp.float32
BF16 = jnp.bfloat16

D_MODEL = 1024
DEPTH = 2
BRANCH_W = 512
D_INNER = 4 * BRANCH_W
HEAD_W = 128
N_HEADS = 4
RET_DECAY_EXP0 = 5.0
ROPE_BASE = 10000.0
SSM_HEADS = 8
SSM_HEAD_DIM = 64
SSM_GROUPS = 2
SSM_STATE = 128
SSM_CONV = 4
SSM_CONV_CH = BRANCH_W + 2 * SSM_GROUPS * SSM_STATE
GLA_KEY_W = 256
GLA_LOWRANK = 16
GLA_GATE_TEMP = 16.0
EPS = 1e-6
LOG2_E = math.log2(math.e)

LANES = 128
SUBLANES = 8

C_AQ, C_AF, C_AI, C_AG = 0, 512, 1024, 1536
C_GQ, C_GK, C_GV, C_GG = 2048, 2304, 2560, 3072
C_LR = 3584
VEC_W = C_LR + LANES
C_RQ, C_RK, C_RV, C_RG = 3712, 4224, 4736, 5248
C_MZ, C_XBC = 5760, 6272
C_DT = 7296
PROJ_W = C_DT + LANES
ORIG_RQ = 2048
ORIG_DT = 5632
ORIG_GQ = ORIG_DT + SSM_HEADS
ORIG_LR = ORIG_GQ + 2 * GLA_KEY_W + 2 * BRANCH_W

TILE = 256
CHUNK = 64
SUB = 8
N_SUB = CHUNK // SUB
N_KEY_GROUPS = N_HEADS + N_HEADS // 2
VMEM_LIMIT_BYTES = 56 * 1024 * 1024


def _dot(a, b):
    return jnp.dot(a, b, preferred_element_type=F32)


def _dot_nt(a, b):
    return lax.dot_general(a, b, (((1,), (1,)), ((), ())), preferred_element_type=F32)


def _dot_tn(a, b):
    return lax.dot_general(a, b, (((0,), (0,)), ((), ())), preferred_element_type=F32)


def _dot3(m_bf16, x):
    hi = x.astype(BF16)
    r1 = x - hi.astype(F32)
    mid = r1.astype(BF16)
    lo = (r1 - mid.astype(F32)).astype(BF16)
    return _dot(m_bf16, hi) + _dot(m_bf16, mid) + _dot(m_bf16, lo)


def _silu(x):
    hx = 0.5 * x
    return hx + hx * jnp.tanh(hx)


def _log1p_exp_neg_abs(x):
    return jnp.log(1.0 + jnp.exp(-jnp.abs(x)))


def _log_sigmoid(x):
    return jnp.minimum(x, 0.0) - _log1p_exp_neg_abs(x)


def _softplus(x):
    return jnp.maximum(x, 0.0) + _log1p_exp_neg_abs(x)


def _logaddexp(a, b):
    return jnp.maximum(a, b) + _log1p_exp_neg_abs(a - b)


def _rms(x, width):
    return x * lax.rsqrt(jnp.sum(x * x, axis=-1, keepdims=True) * (1.0 / width) + EPS)


def _iota(shape, dim):
    return lax.broadcasted_iota(jnp.int32, shape, dim)


def _div_pow2(x, d):
    shift = d.bit_length() - 1
    assert 1 << shift == d
    return lax.shift_right_logical(x, shift)


def _lower_tri(n, block):
    r = _iota((n, n), 0)
    c = _iota((n, n), 1)
    keep = (c <= r) & (_div_pow2(r, block) == _div_pow2(c, block))
    return jnp.where(keep, 1.0, 0.0).astype(BF16)


def _ada_kernel(c_ref, w_ref, b_ref, o_ref):
    c_act = _silu(c_ref[...]).astype(BF16)
    o_ref[...] = _dot(c_act, w_ref[...].astype(BF16)) + b_ref[...]


def _ada_mod(c_pad, w_ada, b_ada):
    n_blk = 3
    return pl.pallas_call(
        _ada_kernel,
        grid=(DEPTH, n_blk),
        in_specs=[
            pl.BlockSpec((SUBLANES, D_MODEL), lambda l, j: (0, 0)),
            pl.BlockSpec((None, D_MODEL, D_MODEL), lambda l, j: (l, 0, j)),
            pl.BlockSpec((None, 1, D_MODEL), lambda l, j: (l, 0, j)),
        ],
        out_specs=pl.BlockSpec((None, SUBLANES, D_MODEL), lambda l, j: (l, 0, j)),
        out_shape=jax.ShapeDtypeStruct((DEPTH, SUBLANES, 3 * D_MODEL), F32),
        name="ada_mod",
    )(c_pad, w_ada, b_ada.reshape(DEPTH, 1, 3 * D_MODEL))


def _w_in_layout_kernel(w_ref, o_ref):
    def put(dst, src0, src1):
        o_ref[:, dst:dst + (src1 - src0)] = w_ref[:, src0:src1].astype(BF16)

    put(C_AQ, 0, ORIG_RQ)
    put(C_GQ, ORIG_GQ, ORIG_LR)
    put(C_RQ, ORIG_RQ, ORIG_DT)
    rows = w_ref.shape[0]
    low_rank = w_ref[:, ORIG_LR:ORIG_LR + GLA_LOWRANK]
    o_ref[:, C_LR:C_LR + LANES] = jnp.concatenate(
        [low_rank, jnp.zeros((rows, LANES - GLA_LOWRANK), F32)], axis=1).astype(BF16)
    dt = w_ref[:, ORIG_DT:ORIG_GQ]
    o_ref[:, C_DT:C_DT + LANES] = jnp.concatenate(
        [dt, jnp.zeros((rows, LANES - SSM_HEADS), F32)], axis=1).astype(BF16)


def _w_in_layout(w_in):
    row_blk = 128
    orig_w = w_in.shape[2]
    return pl.pallas_call(
        _w_in_layout_kernel,
        grid=(DEPTH, D_MODEL // row_blk),
        in_specs=[pl.BlockSpec((None, row_blk, orig_w), lambda l, r: (l, r, 0))],
        out_specs=pl.BlockSpec((None, row_blk, PROJ_W), lambda l, r: (l, r, 0)),
        out_shape=jax.ShapeDtypeStruct((DEPTH, D_MODEL, PROJ_W), BF16),
        name="w_in_layout",
    )(w_in)


HEADS_PER_MATMUL = 2
N_OFF_USED = SUB * (N_SUB * (N_SUB - 1) // 2)
N_OFF = -(-N_OFF_USED // LANES) * LANES


def _key_group_terms(q, k, cum, cum_sc, k_sc, heads_per_group):
    lane_w = LANES // heads_per_group

    def head_lanes(x, i):
        if heads_per_group == 1:
            return x
        lane = _iota(x.shape, 1)
        return jnp.where((lane >= i * lane_w) & (lane < (i + 1) * lane_w), x, 0.0)

    cum_sc[...] = cum
    k_sc[...] = k
    cum_last = cum_sc[CHUNK - 1:CHUNK, :]
    q_exp = q * jnp.exp2(cum)
    k_end = k * jnp.exp2(cum_last - cum)
    dec = jnp.exp2(cum_last)

    qs_parts = [jnp.zeros((SUB, LANES), F32)]
    ks_parts = []
    for i in range(1, N_SUB):
        ref_i = cum_sc[i * SUB - 1:i * SUB, :]
        qs_parts.append(q[i * SUB:(i + 1) * SUB, :] * jnp.exp2(cum[i * SUB:(i + 1) * SUB, :] - ref_i))
        ks_parts.append(k[:i * SUB, :] * jnp.exp2(ref_i - cum[:i * SUB, :]))
    ks_parts.append(jnp.zeros((N_OFF - N_OFF_USED, LANES), F32))
    qs = jnp.concatenate(qs_parts, axis=0)
    ks = jnp.concatenate(ks_parts, axis=0).astype(BF16)

    lane_c = _iota((SUB, CHUNK), 1)
    row_c = _iota((SUB, CHUNK), 0)
    diag_parts = [[] for _ in range(heads_per_group)]
    for i in range(N_SUB):
        acc = [jnp.zeros((SUB, CHUNK), F32) for _ in range(heads_per_group)]
        q_i = q[i * SUB:(i + 1) * SUB, :]
        cum_i = cum[i * SUB:(i + 1) * SUB, :]
        for j in range(SUB):
            s = i * SUB + j
            w = q_i * jnp.exp2(cum_i - cum_sc[s:s + 1, :]) * k_sc[s:s + 1, :]
            for hd in range(heads_per_group):
                a_col = jnp.sum(head_lanes(w, hd), axis=-1, keepdims=True)
                acc[hd] = jnp.where(lane_c == s, a_col, acc[hd])
        causal = (row_c + i * SUB) >= lane_c
        for hd in range(heads_per_group):
            diag_parts[hd].append(jnp.where(causal, acc[hd], 0.0))
    a_diag = [jnp.concatenate(p, axis=0) for p in diag_parts]
    return dict(
        qs=[head_lanes(qs, i).astype(BF16) for i in range(heads_per_group)],
        q_exp=[head_lanes(q_exp, i).astype(BF16) for i in range(heads_per_group)],
        k_end=[head_lanes(k_end, i).astype(BF16) for i in range(heads_per_group)],
        ks=ks, dec=dec, a_diag=a_diag)


def _vector_decay_chunk(groups, v_all, heads_per_group, st_ref):
    n_groups = len(groups)
    n_heads = n_groups * heads_per_group
    rows_all = n_heads * CHUNK
    zeros_bf = jnp.zeros((CHUNK, LANES), BF16)

    lhs_rows = []
    for h in range(n_heads):
        g, i = divmod(h, heads_per_group)
        lhs_rows.append(jnp.concatenate(
            [groups[g]["qs"][i] if gg == g else zeros_bf for gg in range(n_groups)], axis=1))
    ks_cat = jnp.concatenate([grp["ks"] for grp in groups], axis=1)
    p_all = _dot_nt(jnp.concatenate(lhs_rows, axis=0), ks_cat)
    row_blk = _div_pow2(_iota((rows_all, N_OFF), 0) & (CHUNK - 1), SUB)
    col = _iota((rows_all, N_OFF), 1)
    lo = lax.shift_right_logical(row_blk * (row_blk - 1), 1) * SUB
    p_all = jnp.where((col >= lo) & (col < lo + row_blk * SUB), p_all, 0.0)

    a_all = jnp.concatenate([a for grp in groups for a in grp["a_diag"]], axis=0)
    a_all = jnp.concatenate([a_all, jnp.zeros((rows_all, LANES - CHUNK), F32)], axis=1)
    scores = jnp.concatenate([p_all, a_all], axis=1).astype(BF16)
    width = v_all.shape[1]
    v_rows = jnp.concatenate(
        [v_all[:i * SUB, :] for i in range(1, N_SUB)]
        + [jnp.zeros((N_OFF - N_OFF_USED, width), F32), v_all, jnp.zeros((LANES - CHUNK, width), F32)],
        axis=0).astype(BF16)
    r = _dot(scores, v_rows)
    outs = [r[h * CHUNK:(h + 1) * CHUNK, h * HEAD_W:(h + 1) * HEAD_W] for h in range(n_heads)]

    v_bf = v_all.astype(BF16)
    if heads_per_group == 1:
        st = st_ref[...]
        st_bf = st.reshape(n_heads * HEAD_W, LANES).astype(BF16)
        qe_all = jnp.concatenate([grp["q_exp"][0] for grp in groups], axis=0)
        inter = _dot_nt(qe_all, st_bf)
        ke_cat = jnp.concatenate([grp["k_end"][0] for grp in groups], axis=1)
        upd = _dot_tn(v_bf, ke_cat)
        for h in range(n_heads):
            outs[h] = outs[h] + inter[h * CHUNK:(h + 1) * CHUNK, h * HEAD_W:(h + 1) * HEAD_W]
            st_ref[h] = st[h] * groups[h]["dec"] + upd[h * HEAD_W:(h + 1) * HEAD_W, h * LANES:(h + 1) * LANES]
    else:
        for g, grp in enumerate(groups):
            st = st_ref[g]
            qe_rows = jnp.concatenate(grp["q_exp"], axis=0)
            inter = _dot_nt(qe_rows, st.astype(BF16))
            ke_rows = jnp.concatenate(grp["k_end"], axis=0)
            v_g = [v_bf[:, (g * heads_per_group + i) * HEAD_W:(g * heads_per_group + i + 1) * HEAD_W]
                   for i in range(heads_per_group)]
            upd = _dot_tn(jnp.concatenate(v_g, axis=0), ke_rows)
            for i in range(heads_per_group):
                h = g * heads_per_group + i
                outs[h] = outs[h] + inter[i * CHUNK:(i + 1) * CHUNK, :]
            st_ref[g] = st * grp["dec"] + upd
    return outs


def _modulated_norm(x, mod_ref, normg_ref):
    shift = mod_ref[0:1, 0:D_MODEL]
    scale = mod_ref[0:1, D_MODEL:2 * D_MODEL]
    h = (_rms(x, D_MODEL) * normg_ref[...]) * (1.0 + scale) + shift
    return h.astype(BF16)


def _in_proj(h_ref, w_in_ref, proj_ref, col0, col1):
    n_col_blk = 1024
    for c0 in range(col0, col1, n_col_blk):
        c1 = min(c0 + n_col_blk, col1)
        proj_ref[:, c0:c1] = _dot(h_ref[...], w_in_ref[:, c0:c1])


def _mixer_prep(layer, proj_ref, hk_ref, gcum_ref, lb_logits_ref, w_gk2_ref, b_gk2_ref):
    tri_chunk = _lower_tri(TILE, CHUNK)

    lg = [lb_logits_ref[i:i + 1, :] for i in range(DEPTH)]
    lg_max = functools.reduce(jnp.maximum, lg)
    lg_exp = [jnp.exp(r - lg_max) for r in lg]
    lg_den = functools.reduce(lambda a, b: a + b, lg_exp)
    lower = jnp.zeros((1, BRANCH_W), F32)
    for i in range(1, layer + 1):
        lower = lower + lg_exp[i] / lg_den
    log_lb = jnp.log(lower)
    log1m_lb = jnp.log1p(-lower)
    proj_ref[:, C_AQ:C_AQ + BRANCH_W] = _silu(proj_ref[:, C_AQ:C_AQ + BRANCH_W])
    log_f = _logaddexp(log_lb, log1m_lb + _log_sigmoid(proj_ref[:, C_AF:C_AF + BRANCH_W]))
    hk_ref[...] = 1.0 - jnp.exp(log_f)
    proj_ref[:, C_AF:C_AF + BRANCH_W] = _dot3(tri_chunk, log_f * LOG2_E)

    low_rank = proj_ref[:, C_LR:C_LR + LANES]
    gk_gate = _dot(low_rank.astype(BF16), w_gk2_ref[...].astype(BF16)) + b_gk2_ref[...]
    gcum_ref[...] = _dot3(tri_chunk, _log_sigmoid(gk_gate) * (LOG2_E / GLA_GATE_TEMP))
    gla_dk = GLA_KEY_W // N_HEADS
    proj_ref[:, C_GQ:C_GQ + GLA_KEY_W] = proj_ref[:, C_GQ:C_GQ + GLA_KEY_W] * (gla_dk ** -0.5)


def _vector_decay_mixers(proj_ref, y_ref, hk_ref, gcum_ref, cum_sc, k_sc, st_hgrn, st_gla,
                         hgrn_g_ref, gla_g_ref):
    def chunk_body(ci):
        r0 = ci * CHUNK
        rows = slice(r0, r0 + CHUNK)
        groups = []
        for hd in range(N_HEADS):
            cs = slice(hd * HEAD_W, (hd + 1) * HEAD_W)
            groups.append(_key_group_terms(
                proj_ref[rows, C_AQ + cs.start:C_AQ + cs.stop], hk_ref[rows, cs],
                proj_ref[rows, C_AF + cs.start:C_AF + cs.stop],
                cum_sc.at[ci, hd], k_sc.at[ci, hd], 1))
        outs = []
        for p0 in range(0, N_HEADS, HEADS_PER_MATMUL):
            outs += _vector_decay_chunk(
                groups[p0:p0 + HEADS_PER_MATMUL],
                proj_ref[rows, C_AI + p0 * HEAD_W:C_AI + (p0 + HEADS_PER_MATMUL) * HEAD_W],
                1, st_hgrn.at[pl.ds(p0, HEADS_PER_MATMUL)])
        for hd, o in enumerate(outs):
            cs = slice(hd * HEAD_W, (hd + 1) * HEAD_W)
            g_act = _silu(proj_ref[rows, C_AG + cs.start:C_AG + cs.stop])
            y_ref[rows, cs] = (_rms(o, HEAD_W) * hgrn_g_ref[:, cs] * g_act).astype(BF16)
        groups = []
        for pr in range(N_HEADS // 2):
            ks_ = slice(pr * LANES, (pr + 1) * LANES)
            groups.append(_key_group_terms(
                proj_ref[rows, C_GQ + ks_.start:C_GQ + ks_.stop],
                proj_ref[rows, C_GK + ks_.start:C_GK + ks_.stop],
                gcum_ref[rows, ks_],
                cum_sc.at[ci, N_HEADS + pr], k_sc.at[ci, N_HEADS + pr], 2))
        outs = []
        for pr in range(N_HEADS // 2):
            outs += _vector_decay_chunk(
                groups[pr:pr + 1],
                proj_ref[rows, C_GV + 2 * pr * HEAD_W:C_GV + 2 * (pr + 1) * HEAD_W],
                2, st_gla.at[pl.ds(pr, 1)])
        for hd, o in enumerate(outs):
            cs = slice(hd * HEAD_W, (hd + 1) * HEAD_W)
            g_act = _silu(proj_ref[rows, C_GG + cs.start:C_GG + cs.stop])
            y_ref[rows, 3 * BRANCH_W + cs.start:3 * BRANCH_W + cs.stop] = (
                _rms(o, HEAD_W) * gla_g_ref[:, cs] * g_act).astype(BF16)

    for ci in range(TILE // CHUNK):
        chunk_body(ci)


def _scalar_decay_mixers(proj_ref, y_ref, conv_ref, st_ret, st_ssd, cos_ref, sin_ref,
                         ret_g_ref, conv_w_ref, conv_b_ref, dt_bias_ref, a_log_ref, d_skip_ref,
                         ssm_g_ref):
    tri_tile = _lower_tri(TILE, TILE)
    tail = proj_ref[:, C_DT:C_DT + LANES]

    t_col = _iota((TILE, TILE), 0)
    s_row = _iota((TILE, TILE), 1)
    t_minus_s = (t_col - s_row).astype(F32)
    causal_tile = t_col >= s_row
    t_plus1 = (_iota((TILE, HEAD_W), 0) + 1).astype(F32)
    s_to_end = (TILE - 1 - _iota((TILE, HEAD_W), 0)).astype(F32)
    cos_t = cos_ref[...]
    sin_t = sin_ref[...]
    for hd in range(N_HEADS):
        cs = slice(hd * HEAD_W, (hd + 1) * HEAD_W)
        log2_gamma = math.log1p(-(2.0 ** -(RET_DECAY_EXP0 + hd))) * LOG2_E
        rq = proj_ref[:, C_RQ + cs.start:C_RQ + cs.stop]
        rk = proj_ref[:, C_RK + cs.start:C_RK + cs.stop]
        q = rq * cos_t + pltpu.roll(rq, HEAD_W // 2, 1) * sin_t
        k = (rk * cos_t + pltpu.roll(rk, HEAD_W // 2, 1) * sin_t) * (HEAD_W ** -0.5)
        v_bf = proj_ref[:, C_RV + cs.start:C_RV + cs.stop].astype(BF16)
        q_bf = q.astype(BF16)
        decay = jnp.where(causal_tile, jnp.exp2(t_minus_s * log2_gamma), 0.0)
        scores = _dot_nt(q_bf, k.astype(BF16)) * decay
        st = st_ret[hd]
        o = (_dot(scores.astype(BF16), v_bf)
             + jnp.exp2(t_plus1 * log2_gamma) * _dot(q_bf, st.astype(BF16)))
        k_end = k * jnp.exp2(s_to_end * log2_gamma)
        st_ret[hd] = st * (2.0 ** (TILE * log2_gamma)) + _dot_tn(k_end.astype(BF16), v_bf)
        g_act = _silu(proj_ref[:, C_RG + cs.start:C_RG + cs.stop])
        y_ref[:, BRANCH_W + cs.start:BRANCH_W + cs.stop] = (
            _rms(o, HEAD_W) * ret_g_ref[:, cs] * g_act).astype(BF16)

    u = proj_ref[:, C_XBC:C_XBC + SSM_CONV_CH]
    prev = conv_ref[...]
    row8 = _iota((SUBLANES, SSM_CONV_CH), 0)
    xbc = conv_b_ref[...] + conv_w_ref[SSM_CONV - 1:SSM_CONV, :] * u
    for d in range(1, SSM_CONV):
        u_d = pltpu.roll(u, d, 0)
        head = jnp.where(row8 < d, pltpu.roll(prev, d, 0), u_d[0:SUBLANES, :])
        u_d = jnp.concatenate([head, u_d[SUBLANES:, :]], axis=0)
        xbc = xbc + conv_w_ref[SSM_CONV - 1 - d:SSM_CONV - d, :] * u_d
    conv_ref[...] = u[TILE - SUBLANES:TILE, :]
    xbc = _silu(xbc)
    xs = xbc[:, 0:BRANCH_W]
    dt = _softplus(tail + dt_bias_ref[...])
    log_a = dt * (-LOG2_E * jnp.exp(a_log_ref[...]))
    cum = _dot3(tri_tile, log_a)
    cum_t = cum.T
    e_r = _iota((LANES, BRANCH_W), 0)
    e_c = _iota((LANES, BRANCH_W), 1)
    expand = jnp.where(e_r == _div_pow2(e_c, SSM_HEAD_DIM), 1.0, 0.0).astype(BF16)
    dt_e = _dot(dt.astype(BF16), expand)
    cum_e = _dot3_rhs(cum, expand)
    cum_last_e = cum_e[TILE - 1:TILE, :]
    d_e = _dot3_rhs(jnp.broadcast_to(d_skip_ref[...], (SUBLANES, LANES)), expand)[0:1, :]
    xdt = xs * dt_e
    x_end = (xdt * jnp.exp2(cum_last_e - cum_e)).astype(BF16)
    hpg = SSM_HEADS // SSM_GROUPS
    gw = hpg * SSM_HEAD_DIM
    o_parts = []
    for g in range(SSM_GROUPS):
        b_bf = xbc[:, BRANCH_W + g * SSM_STATE:BRANCH_W + (g + 1) * SSM_STATE].astype(BF16)
        c_bf = xbc[:, BRANCH_W + (SSM_GROUPS + g) * SSM_STATE:
                   BRANCH_W + (SSM_GROUPS + g + 1) * SSM_STATE].astype(BF16)
        cb = _dot_nt(c_bf, b_bf)
        st = st_ssd[g]
        inter = _dot(c_bf, st.astype(BF16)) * jnp.exp2(cum_e[:, g * gw:(g + 1) * gw])
        for hh in range(hpg):
            hd = g * hpg + hh
            rel = cum[:, hd:hd + 1] - cum_t[hd:hd + 1, :]
            w = jnp.where(causal_tile, cb * jnp.exp2(rel), 0.0)
            o_parts.append(_dot(w.astype(BF16), xdt[:, hd * SSM_HEAD_DIM:(hd + 1) * SSM_HEAD_DIM].astype(BF16))
                           + inter[:, hh * SSM_HEAD_DIM:(hh + 1) * SSM_HEAD_DIM])
        st_ssd[g] = (st * jnp.exp2(cum_last_e[:, g * gw:(g + 1) * gw])
                     + _dot_tn(b_bf, x_end[:, g * gw:(g + 1) * gw]))
    o_ssd = jnp.concatenate(o_parts, axis=1)
    y_ssd = (o_ssd + d_e * xs) * _silu(proj_ref[:, C_MZ:C_MZ + BRANCH_W])
    gn = BRANCH_W // SSM_GROUPS
    for g in range(SSM_GROUPS):
        cs = slice(g * gn, (g + 1) * gn)
        y_ref[:, 2 * BRANCH_W + cs.start:2 * BRANCH_W + cs.stop] = (
            _rms(y_ssd[:, cs], gn) * ssm_g_ref[:, cs]).astype(BF16)


def _layer_kernel(layer, final,
                  x_ref, xn_ref, mod_ref, modn_ref, normg_ref, w_in_ref, w_out_ref, cos_ref, sin_ref,
                  lb_logits_ref, hgrn_g_ref, ret_g_ref, conv_w_ref, conv_b_ref, dt_bias_ref,
                  a_log_ref, d_skip_ref, ssm_g_ref, w_gk2_ref, b_gk2_ref, gla_g_ref, final_g_ref,
                  o_ref,
                  proj_ref, h_ref, y_ref, conv_ref, hk_ref, gcum_ref, cum_sc, k_sc,
                  st_hgrn, st_ret, st_ssd, st_gla):
    b_idx = pl.program_id(0)
    t_idx = pl.program_id(1)

    def project_vector_part(x_tile, tile_mod_ref):
        h_ref[...] = _modulated_norm(x_tile, tile_mod_ref, normg_ref)
        _in_proj(h_ref, w_in_ref, proj_ref, 0, VEC_W)
        _mixer_prep(layer, proj_ref, hk_ref, gcum_ref, lb_logits_ref, w_gk2_ref, b_gk2_ref)

    @pl.when(t_idx == 0)
    def _():
        st_hgrn[...] = jnp.zeros_like(st_hgrn)
        st_ret[...] = jnp.zeros_like(st_ret)
        st_ssd[...] = jnp.zeros_like(st_ssd)
        st_gla[...] = jnp.zeros_like(st_gla)
        conv_ref[...] = jnp.zeros_like(conv_ref)

    @pl.when((t_idx == 0) & (b_idx == 0))
    def _():
        project_vector_part(x_ref[...], mod_ref)

    _in_proj(h_ref, w_in_ref, proj_ref, VEC_W, PROJ_W)
    _vector_decay_mixers(proj_ref, y_ref, hk_ref, gcum_ref, cum_sc, k_sc, st_hgrn, st_gla,
                         hgrn_g_ref, gla_g_ref)
    project_vector_part(xn_ref[...], modn_ref)
    _scalar_decay_mixers(proj_ref, y_ref, conv_ref, st_ret, st_ssd, cos_ref, sin_ref,
                         ret_g_ref, conv_w_ref, conv_b_ref, dt_bias_ref, a_log_ref, d_skip_ref,
                         ssm_g_ref)
    gate = mod_ref[0:1, 2 * D_MODEL:3 * D_MODEL]
    x_new = x_ref[...] + gate * _dot(y_ref[...], w_out_ref[...])
    if final:
        x_new = _rms(x_new, D_MODEL) * final_g_ref[...]
    o_ref[...] = x_new


def _dot3_rhs(x, m_bf16):
    hi = x.astype(BF16)
    r1 = x - hi.astype(F32)
    mid = r1.astype(BF16)
    lo = (r1 - mid.astype(F32)).astype(BF16)
    return _dot(hi, m_bf16) + _dot(mid, m_bf16) + _dot(lo, m_bf16)


def _resident(shape):
    nd = len(shape)
    return pl.BlockSpec(shape, lambda b, t, _nd=nd: (0,) * _nd, pipeline_mode=pl.Buffered(1))


def _layer_call(layer, final, x, mod, params):
    batch, seq, _ = x.shape
    n_tiles = seq // TILE
    n_chunks = TILE // CHUNK

    def next_tile(t):
        return jnp.where(t == n_tiles - 1, 0, t + 1)

    def next_batch(b, t):
        return jnp.where(t == n_tiles - 1, jnp.minimum(b + 1, batch - 1), b)

    small = [params[k] for k in (
        "lb_logits", "hgrn_g", "ret_g", "conv_w", "conv_b", "dt_bias", "a_log", "d_skip",
        "ssm_g", "w_gk2", "b_gk2", "gla_g", "final_g")]
    in_specs = [
        pl.BlockSpec((None, TILE, D_MODEL), lambda b, t: (b, t, 0)),
        pl.BlockSpec((None, TILE, D_MODEL), lambda b, t: (next_batch(b, t), next_tile(t), 0)),
        pl.BlockSpec((None, SUBLANES, 3 * D_MODEL), lambda b, t: (b, 0, 0)),
        pl.BlockSpec((None, SUBLANES, 3 * D_MODEL), lambda b, t: (next_batch(b, t), 0, 0)),
        _resident((1, D_MODEL)),
        pl.BlockSpec((None, D_MODEL, PROJ_W), lambda b, t: (layer, 0, 0), pipeline_mode=pl.Buffered(1)),
        pl.BlockSpec((None, D_INNER, D_MODEL), lambda b, t: (layer, 0, 0), pipeline_mode=pl.Buffered(1)),
        pl.BlockSpec((TILE, HEAD_W), lambda b, t: (t, 0)),
        pl.BlockSpec((TILE, HEAD_W), lambda b, t: (t, 0)),
    ] + [_resident(a.shape) for a in small]
    scratch = [
        pltpu.VMEM((TILE, PROJ_W), F32),
        pltpu.VMEM((TILE, D_MODEL), BF16),
        pltpu.VMEM((TILE, D_INNER), BF16),
        pltpu.VMEM((SUBLANES, SSM_CONV_CH), F32),
        pltpu.VMEM((TILE, BRANCH_W), F32),
        pltpu.VMEM((TILE, GLA_KEY_W), F32),
        pltpu.VMEM((n_chunks, N_KEY_GROUPS, CHUNK, LANES), F32),
        pltpu.VMEM((n_chunks, N_KEY_GROUPS, CHUNK, LANES), F32),
        pltpu.VMEM((N_HEADS, HEAD_W, HEAD_W), F32),
        pltpu.VMEM((N_HEADS, HEAD_W, HEAD_W), F32),
        pltpu.VMEM((SSM_GROUPS, SSM_STATE, BRANCH_W // SSM_GROUPS), F32),
        pltpu.VMEM((N_HEADS // 2, HEAD_W, LANES), F32),
    ]
    return pl.pallas_call(
        functools.partial(_layer_kernel, layer, final),
        grid=(batch, n_tiles),
        in_specs=in_specs,
        out_specs=pl.BlockSpec((None, TILE, D_MODEL), lambda b, t: (b, t, 0)),
        out_shape=jax.ShapeDtypeStruct(x.shape, F32),
        scratch_shapes=scratch,
        compiler_params=pltpu.CompilerParams(
            dimension_semantics=("arbitrary", "arbitrary"),
            vmem_limit_bytes=VMEM_LIMIT_BYTES),
        name=f"mixer_layer{layer}",
    )(x, x, mod, mod, params["norm_g"], params["w_in"], params["w_out"], params["cos"], params["sin"], *small)


def _pad_lanes(v, lane0=0):
    return jnp.zeros((1, LANES), F32).at[0, lane0:lane0 + v.shape[0]].set(v)


def kernel(x, c, w_ada, b_ada, norm_g, w_in, hgrn_lb_logits, hgrn_onorm_g, ret_onorm_g, ssm_conv_w,
           ssm_conv_b, ssm_dt_bias, ssm_a_log, ssm_d, ssm_norm_g, gla_w_gk2, gla_b_gk2, gla_onorm_g,
           w_out, final_g):
    batch, seq, _ = x.shape
    c_pad = jnp.zeros((SUBLANES, D_MODEL), F32).at[:batch].set(c)
    mod_all = _ada_mod(c_pad, w_ada, b_ada)

    inv_freq = ROPE_BASE ** (-np.arange(0, HEAD_W, 2, dtype=np.float64) / HEAD_W)
    ang = np.arange(seq, dtype=np.float64)[:, None] * inv_freq[None, :]
    cos_tab = jnp.asarray(np.concatenate([np.cos(ang), np.cos(ang)], axis=-1), F32)
    sin_tab = jnp.asarray(np.concatenate([-np.sin(ang), np.sin(ang)], axis=-1), F32)

    w_in_p = _w_in_layout(w_in)
    w_out_bf = w_out.astype(BF16)

    for layer in range(DEPTH):
        w_gk2_p = jnp.zeros((LANES, GLA_KEY_W), F32).at[:GLA_LOWRANK].set(gla_w_gk2[layer])
        params = dict(
            norm_g=norm_g[layer][None, :], w_in=w_in_p, w_out=w_out_bf,
            cos=cos_tab, sin=sin_tab,
            lb_logits=hgrn_lb_logits, hgrn_g=hgrn_onorm_g[layer][None, :],
            ret_g=ret_onorm_g[layer][None, :], conv_w=ssm_conv_w[layer],
            conv_b=ssm_conv_b[layer][None, :], dt_bias=_pad_lanes(ssm_dt_bias[layer]),
            a_log=_pad_lanes(ssm_a_log[layer]), d_skip=_pad_lanes(ssm_d[layer]),
            ssm_g=ssm_norm_g[layer][None, :], w_gk2=w_gk2_p, b_gk2=gla_b_gk2[layer][None, :],
            gla_g=gla_onorm_g[layer][None, :], final_g=final_g[None, :])
        mod = jnp.broadcast_to(mod_all[layer][:batch, None, :], (batch, SUBLANES, 3 * D_MODEL))
        x = _layer_call(layer, layer == DEPTH - 1, x, mod, params)
    return x
```

```python
import functools
import math

import numpy as np
import jax
import jax.numpy as jnp
from jax import lax
from jax.experimental import pallas as pl
from jax.experimental.pallas import tpu as pltpu

F32 = jnp.float32
BF16 = jnp.bfloat16

D_MODEL = 1024
DEPTH = 2
BRANCH_W = 512
D_INNER = 4 * BRANCH_W
HEAD_W = 128
N_HEADS = 4
RET_DECAY_EXP0 = 5.0
ROPE_BASE = 10000.0
SSM_HEADS = 8
SSM_HEAD_DIM = 64
SSM_GROUPS = 2
SSM_STATE = 128
SSM_CONV = 4
SSM_CONV_CH = BRANCH_W + 2 * SSM_GROUPS * SSM_STATE
GLA_KEY_W = 256
GLA_LOWRANK = 16
GLA_GATE_TEMP = 16.0
EPS = 1e-6
LOG2_E = math.log2(math.e)

LANES = 128
SUBLANES = 8

C_AQ, C_AF, C_AI, C_AG = 0, 512, 1024, 1536
C_GQ, C_GK, C_GV, C_GG = 2048, 2304, 2560, 3072
C_LR = 3584
VEC_W = C_LR + LANES
C_RQ, C_RK, C_RV, C_RG = 3712, 4224, 4736, 5248
C_MZ, C_XBC = 5760, 6272
C_DT = 7296
PROJ_W = C_DT + LANES
ORIG_RQ = 2048
ORIG_DT = 5632
ORIG_GQ = ORIG_DT + SSM_HEADS
ORIG_LR = ORIG_GQ + 2 * GLA_KEY_W + 2 * BRANCH_W

TILE = 256
CHUNK = 64
SUB = 8
N_SUB = CHUNK // SUB
N_KEY_GROUPS = N_HEADS + N_HEADS // 2
VMEM_LIMIT_BYTES = 56 * 1024 * 1024


def _dot(a, b):
    return jnp.dot(a, b, preferred_element_type=F32)


def _dot_nt(a, b):
    return lax.dot_general(a, b, (((1,), (1,)), ((), ())), preferred_element_type=F32)


def _dot_tn(a, b):
    return lax.dot_general(a, b, (((0,), (0,)), ((), ())), preferred_element_type=F32)


def _dot3(m_bf16, x):
    hi = x.astype(BF16)
    r1 = x - hi.astype(F32)
    mid = r1.astype(BF16)
    lo = (r1 - mid.astype(F32)).astype(BF16)
    return _dot(m_bf16, hi) + _dot(m_bf16, mid) + _dot(m_bf16, lo)


def _silu(x):
    hx = 0.5 * x
    return hx + hx * jnp.tanh(hx)


def _log1p_exp_neg_abs(x):
    return jnp.log(1.0 + jnp.exp(-jnp.abs(x)))


def _log_sigmoid(x):
    return jnp.minimum(x, 0.0) - _log1p_exp_neg_abs(x)


def _softplus(x):
    return jnp.maximum(x, 0.0) + _log1p_exp_neg_abs(x)


def _logaddexp(a, b):
    return jnp.maximum(a, b) + _log1p_exp_neg_abs(a - b)


def _rms(x, width):
    return x * lax.rsqrt(jnp.sum(x * x, axis=-1, keepdims=True) * (1.0 / width) + EPS)


def _iota(shape, dim):
    return lax.broadcasted_iota(jnp.int32, shape, dim)


def _div_pow2(x, d):
    shift = d.bit_length() - 1
    assert 1 << shift == d
    return lax.shift_right_logical(x, shift)


def _lower_tri(n, block):
    r = _iota((n, n), 0)
    c = _iota((n, n), 1)
    keep = (c <= r) & (_div_pow2(r, block) == _div_pow2(c, block))
    return jnp.where(keep, 1.0, 0.0).astype(BF16)


def _ada_kernel(c_ref, w_ref, b_ref, o_ref):
    c_act = _silu(c_ref[...]).astype(BF16)
    o_ref[...] = _dot(c_act, w_ref[...].astype(BF16)) + b_ref[...]


def _ada_mod(c_pad, w_ada, b_ada):
    n_blk = 3
    return pl.pallas_call(
        _ada_kernel,
        grid=(DEPTH, n_blk),
        in_specs=[
            pl.BlockSpec((SUBLANES, D_MODEL), lambda l, j: (0, 0)),
            pl.BlockSpec((None, D_MODEL, D_MODEL), lambda l, j: (l, 0, j)),
            pl.BlockSpec((None, 1, D_MODEL), lambda l, j: (l, 0, j)),
        ],
        out_specs=pl.BlockSpec((None, SUBLANES, D_MODEL), lambda l, j: (l, 0, j)),
        out_shape=jax.ShapeDtypeStruct((DEPTH, SUBLANES, 3 * D_MODEL), F32),
        name="ada_mod",
    )(c_pad, w_ada, b_ada.reshape(DEPTH, 1, 3 * D_MODEL))


def _w_in_layout_kernel(w_ref, o_ref):
    def put(dst, src0, src1):
        o_ref[:, dst:dst + (src1 - src0)] = w_ref[:, src0:src1].astype(BF16)

    put(C_AQ, 0, ORIG_RQ)
    put(C_GQ, ORIG_GQ, ORIG_LR)
    put(C_RQ, ORIG_RQ, ORIG_DT)
    rows = w_ref.shape[0]
    low_rank = w_ref[:, ORIG_LR:ORIG_LR + GLA_LOWRANK]
    o_ref[:, C_LR:C_LR + LANES] = jnp.concatenate(
        [low_rank, jnp.zeros((rows, LANES - GLA_LOWRANK), F32)], axis=1).astype(BF16)
    dt = w_ref[:, ORIG_DT:ORIG_GQ]
    o_ref[:, C_DT:C_DT + LANES] = jnp.concatenate(
        [dt, jnp.zeros((rows, LANES - SSM_HEADS), F32)], axis=1).astype(BF16)


def _w_in_layout(w_in):
    row_blk = 128
    orig_w = w_in.shape[2]
    return pl.pallas_call(
        _w_in_layout_kernel,
        grid=(DEPTH, D_MODEL // row_blk),
        in_specs=[pl.BlockSpec((None, row_blk, orig_w), lambda l, r: (l, r, 0))],
        out_specs=pl.BlockSpec((None, row_blk, PROJ_W), lambda l, r: (l, r, 0)),
        out_shape=jax.ShapeDtypeStruct((DEPTH, D_MODEL, PROJ_W), BF16),
        name="w_in_layout",
    )(w_in)


HEADS_PER_MATMUL = 2
N_OFF_USED = SUB * (N_SUB * (N_SUB - 1) // 2)
N_OFF = -(-N_OFF_USED // LANES) * LANES


def _key_group_terms(q, k, cum, cum_sc, k_sc, heads_per_group):
    lane_w = LANES // heads_per_group

    def head_lanes(x, i):
        if heads_per_group == 1:
            return x
        lane = _iota(x.shape, 1)
        return jnp.where((lane >= i * lane_w) & (lane < (i + 1) * lane_w), x, 0.0)

    cum_sc[...] = cum
    k_sc[...] = k
    cum_last = cum_sc[CHUNK - 1:CHUNK, :]
    q_exp = q * jnp.exp2(cum)
    k_end = k * jnp.exp2(cum_last - cum)
    dec = jnp.exp2(cum_last)

    qs_parts = [jnp.zeros((SUB, LANES), F32)]
    ks_parts = []
    for i in range(1, N_SUB):
        ref_i = cum_sc[i * SUB - 1:i * SUB, :]
        qs_parts.append(q[i * SUB:(i + 1) * SUB, :] * jnp.exp2(cum[i * SUB:(i + 1) * SUB, :] - ref_i))
        ks_parts.append(k[:i * SUB, :] * jnp.exp2(ref_i - cum[:i * SUB, :]))
    ks_parts.append(jnp.zeros((N_OFF - N_OFF_USED, LANES), F32))
    qs = jnp.concatenate(qs_parts, axis=0)
    ks = jnp.concatenate(ks_parts, axis=0).astype(BF16)

    lane_c = _iota((SUB, CHUNK), 1)
    row_c = _iota((SUB, CHUNK), 0)
    diag_parts = [[] for _ in range(heads_per_group)]
    for i in range(N_SUB):
        acc = [jnp.zeros((SUB, CHUNK), F32) for _ in range(heads_per_group)]
        q_i = q[i * SUB:(i + 1) * SUB, :]
        cum_i = cum[i * SUB:(i + 1) * SUB, :]
        for j in range(SUB):
            s = i * SUB + j
            w = q_i * jnp.exp2(cum_i - cum_sc[s:s + 1, :]) * k_sc[s:s + 1, :]
            for hd in range(heads_per_group):
                a_col = jnp.sum(head_lanes(w, hd), axis=-1, keepdims=True)
                acc[hd] = jnp.where(lane_c == s, a_col, acc[hd])
        causal = (row_c + i * SUB) >= lane_c
        for hd in range(heads_per_group):
            diag_parts[hd].append(jnp.where(causal, acc[hd], 0.0))
    a_diag = [jnp.concatenate(p, axis=0) for p in diag_parts]
    return dict(
        qs=[head_lanes(qs, i).astype(BF16) for i in range(heads_per_group)],
        q_exp=[head_lanes(q_exp, i).astype(BF16) for i in range(heads_per_group)],
        k_end=[head_lanes(k_end, i).astype(BF16) for i in range(heads_per_group)],
        ks=ks, dec=dec, a_diag=a_diag)


def _vector_decay_chunk(groups, v_all, heads_per_group, st_ref):
    n_groups = len(groups)
    n_heads = n_groups * heads_per_group
    rows_all = n_heads * CHUNK
    zeros_bf = jnp.zeros((CHUNK, LANES), BF16)

    lhs_rows = []
    for h in range(n_heads):
        g, i = divmod(h, heads_per_group)
        lhs_rows.append(jnp.concatenate(
            [groups[g]["qs"][i] if gg == g else zeros_bf for gg in range(n_groups)], axis=1))
    ks_cat = jnp.concatenate([grp["ks"] for grp in groups], axis=1)
    p_all = _dot_nt(jnp.concatenate(lhs_rows, axis=0), ks_cat)
    row_blk = _div_pow2(_iota((rows_all, N_OFF), 0) & (CHUNK - 1), SUB)
    col = _iota((rows_all, N_OFF), 1)
    lo = lax.shift_right_logical(row_blk * (row_blk - 1), 1) * SUB
    p_all = jnp.where((col >= lo) & (col < lo + row_blk * SUB), p_all, 0.0)

    a_all = jnp.concatenate([a for grp in groups for a in grp["a_diag"]], axis=0)
    a_all = jnp.concatenate([a_all, jnp.zeros((rows_all, LANES - CHUNK), F32)], axis=1)
    scores = jnp.concatenate([p_all, a_all], axis=1).astype(BF16)
    width = v_all.shape[1]
    v_rows = jnp.concatenate(
        [v_all[:i * SUB, :] for i in range(1, N_SUB)]
        + [jnp.zeros((N_OFF - N_OFF_USED, width), F32), v_all, jnp.zeros((LANES - CHUNK, width), F32)],
        axis=0).astype(BF16)
    r = _dot(scores, v_rows)
    outs = [r[h * CHUNK:(h + 1) * CHUNK, h * HEAD_W:(h + 1) * HEAD_W] for h in range(n_heads)]

    v_bf = v_all.astype(BF16)
    if heads_per_group == 1:
        st = st_ref[...]
        st_bf = st.reshape(n_heads * HEAD_W, LANES).astype(BF16)
        qe_all = jnp.concatenate([grp["q_exp"][0] for grp in groups], axis=0)
        inter = _dot_nt(qe_all, st_bf)
        ke_cat = jnp.concatenate([grp["k_end"][0] for grp in groups], axis=1)
        upd = _dot_tn(v_bf, ke_cat)
        for h in range(n_heads):
            outs[h] = outs[h] + inter[h * CHUNK:(h + 1) * CHUNK, h * HEAD_W:(h + 1) * HEAD_W]
            st_ref[h] = st[h] * groups[h]["dec"] + upd[h * HEAD_W:(h + 1) * HEAD_W, h * LANES:(h + 1) * LANES]
    else:
        for g, grp in enumerate(groups):
            st = st_ref[g]
            qe_rows = jnp.concatenate(grp["q_exp"], axis=0)
            inter = _dot_nt(qe_rows, st.astype(BF16))
            ke_rows = jnp.concatenate(grp["k_end"], axis=0)
            v_g = [v_bf[:, (g * heads_per_group + i) * HEAD_W:(g * heads_per_group + i + 1) * HEAD_W]
                   for i in range(heads_per_group)]
            upd = _dot_tn(jnp.concatenate(v_g, axis=0), ke_rows)
            for i in range(heads_per_group):
                h = g * heads_per_group + i
                outs[h] = outs[h] + inter[i * CHUNK:(i + 1) * CHUNK, :]
            st_ref[g] = st * grp["dec"] + upd
    return outs


def _modulated_norm(x, mod_ref, normg_ref):
    shift = mod_ref[0:1, 0:D_MODEL]
    scale = mod_ref[0:1, D_MODEL:2 * D_MODEL]
    h = (_rms(x, D_MODEL) * normg_ref[...]) * (1.0 + scale) + shift
    return h.astype(BF16)


def _in_proj(h_ref, w_in_ref, proj_ref, col0, col1):
    n_col_blk = 1024
    for c0 in range(col0, col1, n_col_blk):
        c1 = min(c0 + n_col_blk, col1)
        proj_ref[:, c0:c1] = _dot(h_ref[...], w_in_ref[:, c0:c1])


def _mixer_prep(layer, proj_ref, hk_ref, gcum_ref, lb_logits_ref, w_gk2_ref, b_gk2_ref):
    tri_chunk = _lower_tri(TILE, CHUNK)

    lg = [lb_logits_ref[i:i + 1, :] for i in range(DEPTH)]
    lg_max = functools.reduce(jnp.maximum, lg)
    lg_exp = [jnp.exp(r - lg_max) for r in lg]
    lg_den = functools.reduce(lambda a, b: a + b, lg_exp)
    lower = jnp.zeros((1, BRANCH_W), F32)
    for i in range(1, layer + 1):
        lower = lower + lg_exp[i] / lg_den
    log_lb = jnp.log(lower)
    log1m_lb = jnp.log1p(-lower)
    proj_ref[:, C_AQ:C_AQ + BRANCH_W] = _silu(proj_ref[:, C_AQ:C_AQ + BRANCH_W])
    log_f = _logaddexp(log_lb, log1m_lb + _log_sigmoid(proj_ref[:, C_AF:C_AF + BRANCH_W]))
    hk_ref[...] = 1.0 - jnp.exp(log_f)
    proj_ref[:, C_AF:C_AF + BRANCH_W] = _dot3(tri_chunk, log_f * LOG2_E)

    low_rank = proj_ref[:, C_LR:C_LR + LANES]
    gk_gate = _dot(low_rank.astype(BF16), w_gk2_ref[...].astype(BF16)) + b_gk2_ref[...]
    gcum_ref[...] = _dot3(tri_chunk, _log_sigmoid(gk_gate) * (LOG2_E / GLA_GATE_TEMP))
    gla_dk = GLA_KEY_W // N_HEADS
    proj_ref[:, C_GQ:C_GQ + GLA_KEY_W] = proj_ref[:, C_GQ:C_GQ + GLA_KEY_W] * (gla_dk ** -0.5)


def _vector_decay_mixers(proj_ref, y_ref, hk_ref, gcum_ref, cum_sc, k_sc, st_hgrn, st_gla,
                         hgrn_g_ref, gla_g_ref):
    def chunk_body(ci):
        r0 = ci * CHUNK
        rows = slice(r0, r0 + CHUNK)
        groups = []
        for hd in range(N_HEADS):
            cs = slice(hd * HEAD_W, (hd + 1) * HEAD_W)
            groups.append(_key_group_terms(
                proj_ref[rows, C_AQ + cs.start:C_AQ + cs.stop], hk_ref[rows, cs],
                proj_ref[rows, C_AF + cs.start:C_AF + cs.stop],
                cum_sc.at[ci, hd], k_sc.at[ci, hd], 1))
        outs = []
        for p0 in range(0, N_HEADS, HEADS_PER_MATMUL):
            outs += _vector_decay_chunk(
                groups[p0:p0 + HEADS_PER_MATMUL],
                proj_ref[rows, C_AI + p0 * HEAD_W:C_AI + (p0 + HEADS_PER_MATMUL) * HEAD_W],
                1, st_hgrn.at[pl.ds(p0, HEADS_PER_MATMUL)])
        for hd, o in enumerate(outs):
            cs = slice(hd * HEAD_W, (hd + 1) * HEAD_W)
            g_act = _silu(proj_ref[rows, C_AG + cs.start:C_AG + cs.stop])
            y_ref[rows, cs] = (_rms(o, HEAD_W) * hgrn_g_ref[:, cs] * g_act).astype(BF16)
        groups = []
        for pr in range(N_HEADS // 2):
            ks_ = slice(pr * LANES, (pr + 1) * LANES)
            groups.append(_key_group_terms(
                proj_ref[rows, C_GQ + ks_.start:C_GQ + ks_.stop],
                proj_ref[rows, C_GK + ks_.start:C_GK + ks_.stop],
                gcum_ref[rows, ks_],
                cum_sc.at[ci, N_HEADS + pr], k_sc.at[ci, N_HEADS + pr], 2))
        outs = []
        for pr in range(N_HEADS // 2):
            outs += _vector_decay_chunk(
                groups[pr:pr + 1],
                proj_ref[rows, C_GV + 2 * pr * HEAD_W:C_GV + 2 * (pr + 1) * HEAD_W],
                2, st_gla.at[pl.ds(pr, 1)])
        for hd, o in enumerate(outs):
            cs = slice(hd * HEAD_W, (hd + 1) * HEAD_W)
            g_act = _silu(proj_ref[rows, C_GG + cs.start:C_GG + cs.stop])
            y_ref[rows, 3 * BRANCH_W + cs.start:3 * BRANCH_W + cs.stop] = (
                _rms(o, HEAD_W) * gla_g_ref[:, cs] * g_act).astype(BF16)

    for ci in range(TILE // CHUNK):
        chunk_body(ci)


def _scalar_decay_mixers(proj_ref, y_ref, conv_ref, st_ret, st_ssd, cos_ref, sin_ref,
                         ret_g_ref, conv_w_ref, conv_b_ref, dt_bias_ref, a_log_ref, d_skip_ref,
                         ssm_g_ref):
    tri_tile = _lower_tri(TILE, TILE)
    def cols(c0, width):
        return proj_ref[:, c0:c0 + width]

    tail = cols(C_DT, LANES)

    t_col = _iota((TILE, TILE), 0)
    s_row = _iota((TILE, TILE), 1)
    t_minus_s = (t_col - s_row).astype(F32)
    causal_tile = t_col >= s_row
    t_plus1 = (_iota((TILE, HEAD_W), 0) + 1).astype(F32)
    s_to_end = (TILE - 1 - _iota((TILE, HEAD_W), 0)).astype(F32)
    cos_t = cos_ref[...]
    sin_t = sin_ref[...]
    for hd in range(N_HEADS):
        cs = slice(hd * HEAD_W, (hd + 1) * HEAD_W)
        log2_gamma = math.log1p(-(2.0 ** -(RET_DECAY_EXP0 + hd))) * LOG2_E
        rq = cols(C_RQ + cs.start, HEAD_W)
        rk = cols(C_RK + cs.start, HEAD_W)
        q = rq * cos_t + pltpu.roll(rq, HEAD_W // 2, 1) * sin_t
        k = (rk * cos_t + pltpu.roll(rk, HEAD_W // 2, 1) * sin_t) * (HEAD_W ** -0.5)
        v_bf = cols(C_RV + cs.start, HEAD_W).astype(BF16)
        q_bf = q.astype(BF16)
        decay = jnp.where(causal_tile, jnp.exp2(t_minus_s * log2_gamma), 0.0)
        scores = _dot_nt(q_bf, k.astype(BF16)) * decay
        st = st_ret[hd]
        o = (_dot(scores.astype(BF16), v_bf)
             + jnp.exp2(t_plus1 * log2_gamma) * _dot(q_bf, st.astype(BF16)))
        k_end = k * jnp.exp2(s_to_end * log2_gamma)
        st_ret[hd] = st * (2.0 ** (TILE * log2_gamma)) + _dot_tn(k_end.astype(BF16), v_bf)
        g_act = _silu(cols(C_RG + cs.start, HEAD_W))
        y_ref[:, BRANCH_W + cs.start:BRANCH_W + cs.stop] = (
            _rms(o, HEAD_W) * ret_g_ref[:, cs] * g_act).astype(BF16)

    u = cols(C_XBC, SSM_CONV_CH)
    prev = conv_ref[...]
    row8 = _iota((SUBLANES, SSM_CONV_CH), 0)
    xbc = conv_b_ref[...] + conv_w_ref[SSM_CONV - 1:SSM_CONV, :] * u
    for d in range(1, SSM_CONV):
        u_d = pltpu.roll(u, d, 0)
        head = jnp.where(row8 < d, pltpu.roll(prev, d, 0), u_d[0:SUBLANES, :])
        u_d = jnp.concatenate([head, u_d[SUBLANES:, :]], axis=0)
        xbc = xbc + conv_w_ref[SSM_CONV - 1 - d:SSM_CONV - d, :] * u_d
    conv_ref[...] = u[TILE - SUBLANES:TILE, :]
    xbc = _silu(xbc)
    xs = xbc[:, 0:BRANCH_W]
    dt = _softplus(tail + dt_bias_ref[...])
    log_a = dt * (-LOG2_E * jnp.exp(a_log_ref[...]))
    cum = _dot3(tri_tile, log_a)
    cum_t = cum.T

    def on_value_lanes(per_head):
        rows = per_head.shape[0]
        return jnp.concatenate([jnp.broadcast_to(per_head[:, h:h + 1], (rows, SSM_HEAD_DIM))
                                for h in range(SSM_HEADS)], axis=1)

    dt_e = on_value_lanes(dt)
    cum_e = on_value_lanes(cum)
    cum_last_e = cum_e[TILE - 1:TILE, :]
    d_e = on_value_lanes(d_skip_ref[...])
    xdt = xs * dt_e
    x_end = (xdt * jnp.exp2(cum_last_e - cum_e)).astype(BF16)
    hpg = SSM_HEADS // SSM_GROUPS
    gw = hpg * SSM_HEAD_DIM
    o_parts = []
    for g in range(SSM_GROUPS):
        b_bf = xbc[:, BRANCH_W + g * SSM_STATE:BRANCH_W + (g + 1) * SSM_STATE].astype(BF16)
        c_bf = xbc[:, BRANCH_W + (SSM_GROUPS + g) * SSM_STATE:
                   BRANCH_W + (SSM_GROUPS + g + 1) * SSM_STATE].astype(BF16)
        cb = _dot_nt(c_bf, b_bf)
        st = st_ssd[g]
        inter = _dot(c_bf, st.astype(BF16)) * jnp.exp2(cum_e[:, g * gw:(g + 1) * gw])
        for hh in range(hpg):
            hd = g * hpg + hh
            rel = cum[:, hd:hd + 1] - cum_t[hd:hd + 1, :]
            w = jnp.where(causal_tile, cb * jnp.exp2(rel), 0.0)
            o_parts.append(_dot(w.astype(BF16), xdt[:, hd * SSM_HEAD_DIM:(hd + 1) * SSM_HEAD_DIM].astype(BF16))
                           + inter[:, hh * SSM_HEAD_DIM:(hh + 1) * SSM_HEAD_DIM])
        st_ssd[g] = (st * jnp.exp2(cum_last_e[:, g * gw:(g + 1) * gw])
                     + _dot_tn(b_bf, x_end[:, g * gw:(g + 1) * gw]))
    o_ssd = jnp.concatenate(o_parts, axis=1)
    y_ssd = (o_ssd + d_e * xs) * _silu(cols(C_MZ, BRANCH_W))
    gn = BRANCH_W // SSM_GROUPS
    for g in range(SSM_GROUPS):
        cs = slice(g * gn, (g + 1) * gn)
        y_ref[:, 2 * BRANCH_W + cs.start:2 * BRANCH_W + cs.stop] = (
            _rms(y_ssd[:, cs], gn) * ssm_g_ref[:, cs]).astype(BF16)


def _layer_kernel(layer, final,
                  x_ref, xn_ref, mod_ref, modn_ref, normg_ref, w_in_ref, w_out_ref, cos_ref, sin_ref,
                  lb_logits_ref, hgrn_g_ref, ret_g_ref, conv_w_ref, conv_b_ref, dt_bias_ref,
                  a_log_ref, d_skip_ref, ssm_g_ref, w_gk2_ref, b_gk2_ref, gla_g_ref, final_g_ref,
                  o_ref,
                  proj_ref, h_ref, y_ref, conv_ref, hk_ref, gcum_ref, cum_sc, k_sc,
                  st_hgrn, st_ret, st_ssd, st_gla):
    b_idx = pl.program_id(0)
    t_idx = pl.program_id(1)

    def project_vector_part(x_tile, tile_mod_ref):
        h_ref[...] = _modulated_norm(x_tile, tile_mod_ref, normg_ref)
        _in_proj(h_ref, w_in_ref, proj_ref, 0, VEC_W)
        _mixer_prep(layer, proj_ref, hk_ref, gcum_ref, lb_logits_ref, w_gk2_ref, b_gk2_ref)

    @pl.when(t_idx == 0)
    def _():
        st_hgrn[...] = jnp.zeros_like(st_hgrn)
        st_ret[...] = jnp.zeros_like(st_ret)
        st_ssd[...] = jnp.zeros_like(st_ssd)
        st_gla[...] = jnp.zeros_like(st_gla)
        conv_ref[...] = jnp.zeros_like(conv_ref)

    @pl.when((t_idx == 0) & (b_idx == 0))
    def _():
        project_vector_part(x_ref[...], mod_ref)

    _in_proj(h_ref, w_in_ref, proj_ref, VEC_W, PROJ_W)
    _vector_decay_mixers(proj_ref, y_ref, hk_ref, gcum_ref, cum_sc, k_sc, st_hgrn, st_gla,
                         hgrn_g_ref, gla_g_ref)
    project_vector_part(xn_ref[...], modn_ref)
    _scalar_decay_mixers(proj_ref, y_ref, conv_ref, st_ret, st_ssd, cos_ref, sin_ref,
                         ret_g_ref, conv_w_ref, conv_b_ref, dt_bias_ref, a_log_ref, d_skip_ref,
                         ssm_g_ref)
    gate = mod_ref[0:1, 2 * D_MODEL:3 * D_MODEL]
    x_new = x_ref[...] + gate * _dot(y_ref[...], w_out_ref[...])
    if final:
        x_new = _rms(x_new, D_MODEL) * final_g_ref[...]
    o_ref[...] = x_new


def _resident(shape):
    nd = len(shape)
    return pl.BlockSpec(shape, lambda b, t, _nd=nd: (0,) * _nd, pipeline_mode=pl.Buffered(1))


def _layer_call(layer, final, x, mod, params):
    batch, seq, _ = x.shape
    n_tiles = seq // TILE
    n_chunks = TILE // CHUNK

    def next_tile(t):
        return jnp.where(t == n_tiles - 1, 0, t + 1)

    def next_batch(b, t):
        return jnp.where(t == n_tiles - 1, jnp.minimum(b + 1, batch - 1), b)

    small = [params[k] for k in (
        "lb_logits", "hgrn_g", "ret_g", "conv_w", "conv_b", "dt_bias", "a_log", "d_skip",
        "ssm_g", "w_gk2", "b_gk2", "gla_g", "final_g")]
    in_specs = [
        pl.BlockSpec((None, TILE, D_MODEL), lambda b, t: (b, t, 0)),
        pl.BlockSpec((None, TILE, D_MODEL), lambda b, t: (next_batch(b, t), next_tile(t), 0)),
        pl.BlockSpec((None, SUBLANES, 3 * D_MODEL), lambda b, t: (b, 0, 0)),
        pl.BlockSpec((None, SUBLANES, 3 * D_MODEL), lambda b, t: (next_batch(b, t), 0, 0)),
        _resident((1, D_MODEL)),
        pl.BlockSpec((None, D_MODEL, PROJ_W), lambda b, t: (layer, 0, 0), pipeline_mode=pl.Buffered(1)),
        pl.BlockSpec((None, D_INNER, D_MODEL), lambda b, t: (layer, 0, 0), pipeline_mode=pl.Buffered(1)),
        pl.BlockSpec((TILE, HEAD_W), lambda b, t: (t, 0)),
        pl.BlockSpec((TILE, HEAD_W), lambda b, t: (t, 0)),
    ] + [_resident(a.shape) for a in small]
    scratch = [
        pltpu.VMEM((TILE, PROJ_W), F32),
        pltpu.VMEM((TILE, D_MODEL), BF16),
        pltpu.VMEM((TILE, D_INNER), BF16),
        pltpu.VMEM((SUBLANES, SSM_CONV_CH), F32),
        pltpu.VMEM((TILE, BRANCH_W), F32),
        pltpu.VMEM((TILE, GLA_KEY_W), F32),
        pltpu.VMEM((n_chunks, N_KEY_GROUPS, CHUNK, LANES), F32),
        pltpu.VMEM((n_chunks, N_KEY_GROUPS, CHUNK, LANES), F32),
        pltpu.VMEM((N_HEADS, HEAD_W, HEAD_W), F32),
        pltpu.VMEM((N_HEADS, HEAD_W, HEAD_W), F32),
        pltpu.VMEM((SSM_GROUPS, SSM_STATE, BRANCH_W // SSM_GROUPS), F32),
        pltpu.VMEM((N_HEADS // 2, HEAD_W, LANES), F32),
    ]
    return pl.pallas_call(
        functools.partial(_layer_kernel, layer, final),
        grid=(batch, n_tiles),
        in_specs=in_specs,
        out_specs=pl.BlockSpec((None, TILE, D_MODEL), lambda b, t: (b, t, 0)),
        out_shape=jax.ShapeDtypeStruct(x.shape, F32),
        scratch_shapes=scratch,
        compiler_params=pltpu.CompilerParams(
            dimension_semantics=("arbitrary", "arbitrary"),
            vmem_limit_bytes=VMEM_LIMIT_BYTES),
        name=f"mixer_layer{layer}",
    )(x, x, mod, mod, params["norm_g"], params["w_in"], params["w_out"], params["cos"], params["sin"], *small)


def _pad_lanes(v, lane0=0):
    return jnp.zeros((1, LANES), F32).at[0, lane0:lane0 + v.shape[0]].set(v)


def kernel(x, c, w_ada, b_ada, norm_g, w_in, hgrn_lb_logits, hgrn_onorm_g, ret_onorm_g, ssm_conv_w,
           ssm_conv_b, ssm_dt_bias, ssm_a_log, ssm_d, ssm_norm_g, gla_w_gk2, gla_b_gk2, gla_onorm_g,
           w_out, final_g):
    batch, seq, _ = x.shape
    c_pad = jnp.zeros((SUBLANES, D_MODEL), F32).at[:batch].set(c)
    mod_all = _ada_mod(c_pad, w_ada, b_ada)

    inv_freq = ROPE_BASE ** (-np.arange(0, HEAD_W, 2, dtype=np.float64) / HEAD_W)
    ang = np.arange(seq, dtype=np.float64)[:, None] * inv_freq[None, :]
    cos_tab = jnp.asarray(np.concatenate([np.cos(ang), np.cos(ang)], axis=-1), F32)
    sin_tab = jnp.asarray(np.concatenate([-np.sin(ang), np.sin(ang)], axis=-1), F32)

    w_in_p = _w_in_layout(w_in)
    w_out_bf = w_out.astype(BF16)

    for layer in range(DEPTH):
        w_gk2_p = jnp.zeros((LANES, GLA_KEY_W), F32).at[:GLA_LOWRANK].set(gla_w_gk2[layer])
        params = dict(
            norm_g=norm_g[layer][None, :], w_in=w_in_p, w_out=w_out_bf,
            cos=cos_tab, sin=sin_tab,
            lb_logits=hgrn_lb_logits, hgrn_g=hgrn_onorm_g[layer][None, :],
            ret_g=ret_onorm_g[layer][None, :], conv_w=ssm_conv_w[layer],
            conv_b=ssm_conv_b[layer][None, :], dt_bias=_pad_lanes(ssm_dt_bias[layer]),
            a_log=_pad_lanes(ssm_a_log[layer]), d_skip=_pad_lanes(ssm_d[layer]),
            ssm_g=ssm_norm_g[layer][None, :], w_gk2=w_gk2_p, b_gk2=gla_b_gk2[layer][None, :],
            gla_g=gla_onorm_g[layer][None, :], final_g=final_g[None, :])
        mod = jnp.broadcast_to(mod_all[layer][:batch, None, :], (batch, SUBLANES, 3 * D_MODEL))
        x = _layer_call(layer, layer == DEPTH - 1, x, mod, params)
    return x
```

```python
import functools
import math

import numpy as np
import jax
import jax.numpy as jnp
from jax import lax
from jax.experimental import pallas as pl
from jax.experimental.pallas import tpu as pltpu

F32 = jnp.float32
BF16 = jnp.bfloat16

D_MODEL = 1024
DEPTH = 2
BRANCH_W = 512
D_INNER = 4 * BRANCH_W
HEAD_W = 128
N_HEADS = 4
RET_DECAY_EXP0 = 5.0
ROPE_BASE = 10000.0
SSM_HEADS = 8
SSM_HEAD_DIM = 64
SSM_GROUPS = 2
SSM_STATE = 128
SSM_CONV = 4
SSM_CONV_CH = BRANCH_W + 2 * SSM_GROUPS * SSM_STATE
GLA_KEY_W = 256
GLA_LOWRANK = 16
GLA_GATE_TEMP = 16.0
EPS = 1e-6
LOG2_E = math.log2(math.e)

LANES = 128
SUBLANES = 8

C_AQ, C_AF, C_AI, C_AG = 0, 512, 1024, 1536
C_GQ, C_GK, C_GV, C_GG = 2048, 2304, 2560, 3072
C_LR = 3584
VEC_W = C_LR + LANES
C_RQ, C_RK, C_RV, C_RG = 3712, 4224, 4736, 5248
C_MZ, C_XBC = 5760, 6272
C_DT = 7296
PROJ_W = C_DT + LANES
ORIG_RQ = 2048
ORIG_DT = 5632
ORIG_GQ = ORIG_DT + SSM_HEADS
ORIG_LR = ORIG_GQ + 2 * GLA_KEY_W + 2 * BRANCH_W
ORIG_W = ORIG_LR + GLA_LOWRANK
GLA_TAIL_ROWS = ORIG_W - ORIG_GQ
LR_LANE0 = LANES - GLA_LOWRANK

TILE = 256
CHUNK = 64
SUB = 8
N_SUB = CHUNK // SUB
N_KEY_GROUPS = N_HEADS + N_HEADS // 2
VMEM_LIMIT_BYTES = 56 * 1024 * 1024


def _dot(a, b):
    return jnp.dot(a, b, preferred_element_type=F32)


def _dot_nt(a, b):
    return lax.dot_general(a, b, (((1,), (1,)), ((), ())), preferred_element_type=F32)


def _dot_tn(a, b):
    return lax.dot_general(a, b, (((0,), (0,)), ((), ())), preferred_element_type=F32)


def _dot3(m_bf16, x):
    hi = x.astype(BF16)
    r1 = x - hi.astype(F32)
    mid = r1.astype(BF16)
    lo = (r1 - mid.astype(F32)).astype(BF16)
    return _dot(m_bf16, hi) + _dot(m_bf16, mid) + _dot(m_bf16, lo)


def _silu(x):
    hx = 0.5 * x
    return hx + hx * jnp.tanh(hx)


def _log1p_exp_neg_abs(x):
    return jnp.log(1.0 + jnp.exp(-jnp.abs(x)))


def _log_sigmoid(x):
    return jnp.minimum(x, 0.0) - _log1p_exp_neg_abs(x)


def _softplus(x):
    return jnp.maximum(x, 0.0) + _log1p_exp_neg_abs(x)


def _logaddexp(a, b):
    return jnp.maximum(a, b) + _log1p_exp_neg_abs(a - b)


def _rms(x, width):
    return x * lax.rsqrt(jnp.sum(x * x, axis=-1, keepdims=True) * (1.0 / width) + EPS)


def _iota(shape, dim):
    return lax.broadcasted_iota(jnp.int32, shape, dim)


def _div_pow2(x, d):
    shift = d.bit_length() - 1
    assert 1 << shift == d
    return lax.shift_right_logical(x, shift)


def _lower_tri(n, block):
    r = _iota((n, n), 0)
    c = _iota((n, n), 1)
    keep = (c <= r) & (_div_pow2(r, block) == _div_pow2(c, block))
    return jnp.where(keep, 1.0, 0.0).astype(BF16)


def _ada_kernel(c_ref, w_ref, b_ref, o_ref):
    c_act = _silu(c_ref[...]).astype(BF16)
    o_ref[...] = _dot(c_act, w_ref[...].astype(BF16)) + b_ref[...]


def _ada_mod(c_pad, w_ada, b_ada):
    n_blk = 3
    return pl.pallas_call(
        _ada_kernel,
        grid=(DEPTH, n_blk),
        in_specs=[
            pl.BlockSpec((SUBLANES, D_MODEL), lambda l, j: (0, 0)),
            pl.BlockSpec((None, D_MODEL, D_MODEL), lambda l, j: (l, 0, j)),
            pl.BlockSpec((None, 1, D_MODEL), lambda l, j: (l, 0, j)),
        ],
        out_specs=pl.BlockSpec((None, SUBLANES, D_MODEL), lambda l, j: (l, 0, j)),
        out_shape=jax.ShapeDtypeStruct((DEPTH, SUBLANES, 3 * D_MODEL), F32),
        name="ada_mod",
    )(c_pad, w_ada, b_ada.reshape(DEPTH, 1, 3 * D_MODEL))


HEADS_PER_MATMUL = 2
N_OFF_USED = SUB * (N_SUB * (N_SUB - 1) // 2)
N_OFF = -(-N_OFF_USED // LANES) * LANES


def _key_group_terms(q, k, cum, cum_sc, k_sc, heads_per_group):
    lane_w = LANES // heads_per_group

    def head_lanes(x, i):
        if heads_per_group == 1:
            return x
        lane = _iota(x.shape, 1)
        return jnp.where((lane >= i * lane_w) & (lane < (i + 1) * lane_w), x, 0.0)

    cum_sc[...] = cum
    k_sc[...] = k
    cum_last = cum_sc[CHUNK - 1:CHUNK, :]
    q_exp = q * jnp.exp2(cum)
    k_end = k * jnp.exp2(cum_last - cum)
    dec = jnp.exp2(cum_last)

    qs_parts = [jnp.zeros((SUB, LANES), F32)]
    ks_parts = []
    for i in range(1, N_SUB):
        ref_i = cum_sc[i * SUB - 1:i * SUB, :]
        qs_parts.append(q[i * SUB:(i + 1) * SUB, :] * jnp.exp2(cum[i * SUB:(i + 1) * SUB, :] - ref_i))
        ks_parts.append(k[:i * SUB, :] * jnp.exp2(ref_i - cum[:i * SUB, :]))
    ks_parts.append(jnp.zeros((N_OFF - N_OFF_USED, LANES), F32))
    qs = jnp.concatenate(qs_parts, axis=0)
    ks = jnp.concatenate(ks_parts, axis=0).astype(BF16)

    lane_c = _iota((SUB, CHUNK), 1)
    row_c = _iota((SUB, CHUNK), 0)
    diag_parts = [[] for _ in range(heads_per_group)]
    for i in range(N_SUB):
        acc = [jnp.zeros((SUB, CHUNK), F32) for _ in range(heads_per_group)]
        q_i = q[i * SUB:(i + 1) * SUB, :]
        cum_i = cum[i * SUB:(i + 1) * SUB, :]
        for j in range(SUB):
            s = i * SUB + j
            w = q_i * jnp.exp2(cum_i - cum_sc[s:s + 1, :]) * k_sc[s:s + 1, :]
            for hd in range(heads_per_group):
                a_col = jnp.sum(head_lanes(w, hd), axis=-1, keepdims=True)
                acc[hd] = jnp.where(lane_c == s, a_col, acc[hd])
        causal = (row_c + i * SUB) >= lane_c
        for hd in range(heads_per_group):
            diag_parts[hd].append(jnp.where(causal, acc[hd], 0.0))
    a_diag = [jnp.concatenate(p, axis=0) for p in diag_parts]
    return dict(
        qs=[head_lanes(qs, i).astype(BF16) for i in range(heads_per_group)],
        q_exp=[head_lanes(q_exp, i).astype(BF16) for i in range(heads_per_group)],
        k_end=[head_lanes(k_end, i).astype(BF16) for i in range(heads_per_group)],
        ks=ks, dec=dec, a_diag=a_diag)


def _vector_decay_chunk(groups, v_all, heads_per_group, st_ref):
    n_groups = len(groups)
    n_heads = n_groups * heads_per_group
    rows_all = n_heads * CHUNK
    zeros_bf = jnp.zeros((CHUNK, LANES), BF16)

    lhs_rows = []
    for h in range(n_heads):
        g, i = divmod(h, heads_per_group)
        lhs_rows.append(jnp.concatenate(
            [groups[g]["qs"][i] if gg == g else zeros_bf for gg in range(n_groups)], axis=1))
    ks_cat = jnp.concatenate([grp["ks"] for grp in groups], axis=1)
    p_all = _dot_nt(jnp.concatenate(lhs_rows, axis=0), ks_cat)
    row_blk = _div_pow2(_iota((rows_all, N_OFF), 0) & (CHUNK - 1), SUB)
    col = _iota((rows_all, N_OFF), 1)
    lo = lax.shift_right_logical(row_blk * (row_blk - 1), 1) * SUB
    p_all = jnp.where((col >= lo) & (col < lo + row_blk * SUB), p_all, 0.0)

    a_all = jnp.concatenate([a for grp in groups for a in grp["a_diag"]], axis=0)
    a_all = jnp.concatenate([a_all, jnp.zeros((rows_all, LANES - CHUNK), F32)], axis=1)
    scores = jnp.concatenate([p_all, a_all], axis=1).astype(BF16)
    width = v_all.shape[1]
    v_rows = jnp.concatenate(
        [v_all[:i * SUB, :] for i in range(1, N_SUB)]
        + [jnp.zeros((N_OFF - N_OFF_USED, width), F32), v_all, jnp.zeros((LANES - CHUNK, width), F32)],
        axis=0).astype(BF16)
    r = _dot(scores, v_rows)
    outs = [r[h * CHUNK:(h + 1) * CHUNK, h * HEAD_W:(h + 1) * HEAD_W] for h in range(n_heads)]

    v_bf = v_all.astype(BF16)
    if heads_per_group == 1:
        st = st_ref[...]
        st_bf = st.reshape(n_heads * HEAD_W, LANES).astype(BF16)
        qe_all = jnp.concatenate([grp["q_exp"][0] for grp in groups], axis=0)
        inter = _dot_nt(qe_all, st_bf)
        ke_cat = jnp.concatenate([grp["k_end"][0] for grp in groups], axis=1)
        upd = _dot_tn(v_bf, ke_cat)
        for h in range(n_heads):
            outs[h] = outs[h] + inter[h * CHUNK:(h + 1) * CHUNK, h * HEAD_W:(h + 1) * HEAD_W]
            st_ref[h] = st[h] * groups[h]["dec"] + upd[h * HEAD_W:(h + 1) * HEAD_W, h * LANES:(h + 1) * LANES]
    else:
        for g, grp in enumerate(groups):
            st = st_ref[g]
            qe_rows = jnp.concatenate(grp["q_exp"], axis=0)
            inter = _dot_nt(qe_rows, st.astype(BF16))
            ke_rows = jnp.concatenate(grp["k_end"], axis=0)
            v_g = [v_bf[:, (g * heads_per_group + i) * HEAD_W:(g * heads_per_group + i + 1) * HEAD_W]
                   for i in range(heads_per_group)]
            upd = _dot_tn(jnp.concatenate(v_g, axis=0), ke_rows)
            for i in range(heads_per_group):
                h = g * heads_per_group + i
                outs[h] = outs[h] + inter[i * CHUNK:(i + 1) * CHUNK, :]
            st_ref[g] = st * grp["dec"] + upd
    return outs


def _modulated_norm(x, mod_ref, normg_ref):
    shift = mod_ref[0:1, 0:D_MODEL]
    scale = mod_ref[0:1, D_MODEL:2 * D_MODEL]
    h = (_rms(x, D_MODEL) * normg_ref[...]) * (1.0 + scale) + shift
    return h.astype(BF16)


def _in_proj(h_ref, wt_ref, proj_ref, dst, row0, row1):
    n_blk = 1024
    for r0 in range(row0, row1, n_blk):
        r1 = min(r0 + n_blk, row1)
        proj_ref[:, dst + r0 - row0:dst + r1 - row0] = _dot_nt(h_ref[...], wt_ref[r0:r1, :])


def _in_proj_vector_part(h_ref, wt_ref, wt_gla_ref, proj_ref):
    _in_proj(h_ref, wt_ref, proj_ref, C_AQ, 0, ORIG_RQ)
    _in_proj(h_ref, wt_gla_ref, proj_ref, C_GQ, 0, ORIG_LR - ORIG_GQ)
    _in_proj(h_ref, wt_gla_ref, proj_ref, C_LR, GLA_TAIL_ROWS - LANES, GLA_TAIL_ROWS)


def _in_proj_scalar_part(h_ref, wt_ref, proj_ref):
    _in_proj(h_ref, wt_ref, proj_ref, C_RQ, ORIG_RQ, ORIG_DT)
    _in_proj(h_ref, wt_ref, proj_ref, C_DT, ORIG_DT, ORIG_DT + LANES)


def _mixer_prep(layer, proj_ref, hk_ref, gcum_ref, lb_logits_ref, w_gk2_ref, b_gk2_ref):
    tri_chunk = _lower_tri(TILE, CHUNK)

    lg = [lb_logits_ref[i:i + 1, :] for i in range(DEPTH)]
    lg_max = functools.reduce(jnp.maximum, lg)
    lg_exp = [jnp.exp(r - lg_max) for r in lg]
    lg_den = functools.reduce(lambda a, b: a + b, lg_exp)
    lower = jnp.zeros((1, BRANCH_W), F32)
    for i in range(1, layer + 1):
        lower = lower + lg_exp[i] / lg_den
    log_lb = jnp.log(lower)
    log1m_lb = jnp.log1p(-lower)
    proj_ref[:, C_AQ:C_AQ + BRANCH_W] = _silu(proj_ref[:, C_AQ:C_AQ + BRANCH_W])
    log_f = _logaddexp(log_lb, log1m_lb + _log_sigmoid(proj_ref[:, C_AF:C_AF + BRANCH_W]))
    hk_ref[...] = 1.0 - jnp.exp(log_f)
    proj_ref[:, C_AF:C_AF + BRANCH_W] = _dot3(tri_chunk, log_f * LOG2_E)

    low_rank = proj_ref[:, C_LR:C_LR + LANES]
    gk_gate = _dot(low_rank.astype(BF16), w_gk2_ref[...].astype(BF16)) + b_gk2_ref[...]
    gcum_ref[...] = _dot3(tri_chunk, _log_sigmoid(gk_gate) * (LOG2_E / GLA_GATE_TEMP))
    gla_dk = GLA_KEY_W // N_HEADS
    proj_ref[:, C_GQ:C_GQ + GLA_KEY_W] = proj_ref[:, C_GQ:C_GQ + GLA_KEY_W] * (gla_dk ** -0.5)


def _vector_decay_mixers(proj_ref, y_ref, hk_ref, gcum_ref, cum_sc, k_sc, st_hgrn, st_gla,
                         hgrn_g_ref, gla_g_ref):
    def chunk_body(ci):
        r0 = ci * CHUNK
        rows = slice(r0, r0 + CHUNK)
        groups = []
        for hd in range(N_HEADS):
            cs = slice(hd * HEAD_W, (hd + 1) * HEAD_W)
            groups.append(_key_group_terms(
                proj_ref[rows, C_AQ + cs.start:C_AQ + cs.stop], hk_ref[rows, cs],
                proj_ref[rows, C_AF + cs.start:C_AF + cs.stop],
                cum_sc.at[ci, hd], k_sc.at[ci, hd], 1))
        outs = []
        for p0 in range(0, N_HEADS, HEADS_PER_MATMUL):
            outs += _vector_decay_chunk(
                groups[p0:p0 + HEADS_PER_MATMUL],
                proj_ref[rows, C_AI + p0 * HEAD_W:C_AI + (p0 + HEADS_PER_MATMUL) * HEAD_W],
                1, st_hgrn.at[pl.ds(p0, HEADS_PER_MATMUL)])
        for hd, o in enumerate(outs):
            cs = slice(hd * HEAD_W, (hd + 1) * HEAD_W)
            g_act = _silu(proj_ref[rows, C_AG + cs.start:C_AG + cs.stop])
            y_ref[rows, cs] = (_rms(o, HEAD_W) * hgrn_g_ref[:, cs] * g_act).astype(BF16)
        groups = []
        for pr in range(N_HEADS // 2):
            ks_ = slice(pr * LANES, (pr + 1) * LANES)
            groups.append(_key_group_terms(
                proj_ref[rows, C_GQ + ks_.start:C_GQ + ks_.stop],
                proj_ref[rows, C_GK + ks_.start:C_GK + ks_.stop],
                gcum_ref[rows, ks_],
                cum_sc.at[ci, N_HEADS + pr], k_sc.at[ci, N_HEADS + pr], 2))
        outs = []
        for pr in range(N_HEADS // 2):
            outs += _vector_decay_chunk(
                groups[pr:pr + 1],
                proj_ref[rows, C_GV + 2 * pr * HEAD_W:C_GV + 2 * (pr + 1) * HEAD_W],
                2, st_gla.at[pl.ds(pr, 1)])
        for hd, o in enumerate(outs):
            cs = slice(hd * HEAD_W, (hd + 1) * HEAD_W)
            g_act = _silu(proj_ref[rows, C_GG + cs.start:C_GG + cs.stop])
            y_ref[rows, 3 * BRANCH_W + cs.start:3 * BRANCH_W + cs.stop] = (
                _rms(o, HEAD_W) * gla_g_ref[:, cs] * g_act).astype(BF16)

    for ci in range(TILE // CHUNK):
        chunk_body(ci)


def _scalar_decay_mixers(proj_ref, y_ref, conv_ref, st_ret, st_ssd, cos_ref, sin_ref,
                         ret_g_ref, conv_w_ref, conv_b_ref, dt_bias_ref, a_log_ref, d_skip_ref,
                         ssm_g_ref):
    tri_tile = _lower_tri(TILE, TILE)
    def cols(c0, width):
        return proj_ref[:, c0:c0 + width]

    tail = cols(C_DT, LANES)

    t_col = _iota((TILE, TILE), 0)
    s_row = _iota((TILE, TILE), 1)
    t_minus_s = (t_col - s_row).astype(F32)
    causal_tile = t_col >= s_row
    t_plus1 = (_iota((TILE, HEAD_W), 0) + 1).astype(F32)
    s_to_end = (TILE - 1 - _iota((TILE, HEAD_W), 0)).astype(F32)
    cos_t = cos_ref[...]
    sin_t = sin_ref[...]
    for hd in range(N_HEADS):
        cs = slice(hd * HEAD_W, (hd + 1) * HEAD_W)
        log2_gamma = math.log1p(-(2.0 ** -(RET_DECAY_EXP0 + hd))) * LOG2_E
        rq = cols(C_RQ + cs.start, HEAD_W)
        rk = cols(C_RK + cs.start, HEAD_W)
        q = rq * cos_t + pltpu.roll(rq, HEAD_W // 2, 1) * sin_t
        k = (rk * cos_t + pltpu.roll(rk, HEAD_W // 2, 1) * sin_t) * (HEAD_W ** -0.5)
        v_bf = cols(C_RV + cs.start, HEAD_W).astype(BF16)
        q_bf = q.astype(BF16)
        decay = jnp.where(causal_tile, jnp.exp2(t_minus_s * log2_gamma), 0.0)
        scores = _dot_nt(q_bf, k.astype(BF16)) * decay
        st = st_ret[hd]
        o = (_dot(scores.astype(BF16), v_bf)
             + jnp.exp2(t_plus1 * log2_gamma) * _dot(q_bf, st.astype(BF16)))
        k_end = k * jnp.exp2(s_to_end * log2_gamma)
        st_ret[hd] = st * (2.0 ** (TILE * log2_gamma)) + _dot_tn(k_end.astype(BF16), v_bf)
        g_act = _silu(cols(C_RG + cs.start, HEAD_W))
        y_ref[:, BRANCH_W + cs.start:BRANCH_W + cs.stop] = (
            _rms(o, HEAD_W) * ret_g_ref[:, cs] * g_act).astype(BF16)

    u = cols(C_XBC, SSM_CONV_CH)
    prev = conv_ref[...]
    row8 = _iota((SUBLANES, SSM_CONV_CH), 0)
    xbc = conv_b_ref[...] + conv_w_ref[SSM_CONV - 1:SSM_CONV, :] * u
    for d in range(1, SSM_CONV):
        u_d = pltpu.roll(u, d, 0)
        head = jnp.where(row8 < d, pltpu.roll(prev, d, 0), u_d[0:SUBLANES, :])
        u_d = jnp.concatenate([head, u_d[SUBLANES:, :]], axis=0)
        xbc = xbc + conv_w_ref[SSM_CONV - 1 - d:SSM_CONV - d, :] * u_d
    conv_ref[...] = u[TILE - SUBLANES:TILE, :]
    xbc = _silu(xbc)
    xs = xbc[:, 0:BRANCH_W]
    dt = _softplus(tail + dt_bias_ref[...])
    log_a = dt * (-LOG2_E * jnp.exp(a_log_ref[...]))
    cum = _dot3(tri_tile, log_a)
    cum_t = cum.T

    def on_value_lanes(per_head):
        rows = per_head.shape[0]
        return jnp.concatenate([jnp.broadcast_to(per_head[:, h:h + 1], (rows, SSM_HEAD_DIM))
                                for h in range(SSM_HEADS)], axis=1)

    dt_e = on_value_lanes(dt)
    cum_e = on_value_lanes(cum)
    cum_last_e = cum_e[TILE - 1:TILE, :]
    d_e = on_value_lanes(d_skip_ref[...])
    xdt = xs * dt_e
    x_end = (xdt * jnp.exp2(cum_last_e - cum_e)).astype(BF16)
    hpg = SSM_HEADS // SSM_GROUPS
    gw = hpg * SSM_HEAD_DIM
    o_parts = []
    for g in range(SSM_GROUPS):
        b_bf = xbc[:, BRANCH_W + g * SSM_STATE:BRANCH_W + (g + 1) * SSM_STATE].astype(BF16)
        c_bf = xbc[:, BRANCH_W + (SSM_GROUPS + g) * SSM_STATE:
                   BRANCH_W + (SSM_GROUPS + g + 1) * SSM_STATE].astype(BF16)
        cb = _dot_nt(c_bf, b_bf)
        st = st_ssd[g]
        inter = _dot(c_bf, st.astype(BF16)) * jnp.exp2(cum_e[:, g * gw:(g + 1) * gw])
        for hh in range(hpg):
            hd = g * hpg + hh
            rel = cum[:, hd:hd + 1] - cum_t[hd:hd + 1, :]
            w = jnp.where(causal_tile, cb * jnp.exp2(rel), 0.0)
            o_parts.append(_dot(w.astype(BF16), xdt[:, hd * SSM_HEAD_DIM:(hd + 1) * SSM_HEAD_DIM].astype(BF16))
                           + inter[:, hh * SSM_HEAD_DIM:(hh + 1) * SSM_HEAD_DIM])
        st_ssd[g] = (st * jnp.exp2(cum_last_e[:, g * gw:(g + 1) * gw])
                     + _dot_tn(b_bf, x_end[:, g * gw:(g + 1) * gw]))
    o_ssd = jnp.concatenate(o_parts, axis=1)
    y_ssd = (o_ssd + d_e * xs) * _silu(cols(C_MZ, BRANCH_W))
    gn = BRANCH_W // SSM_GROUPS
    for g in range(SSM_GROUPS):
        cs = slice(g * gn, (g + 1) * gn)
        y_ref[:, 2 * BRANCH_W + cs.start:2 * BRANCH_W + cs.stop] = (
            _rms(y_ssd[:, cs], gn) * ssm_g_ref[:, cs]).astype(BF16)


def _layer_kernel(layer, final,
                  x_ref, xn_ref, mod_ref, modn_ref, normg_ref, wt_ref, wt_gla_ref, w_out_ref,
                  cos_ref, sin_ref,
                  lb_logits_ref, hgrn_g_ref, ret_g_ref, conv_w_ref, conv_b_ref, dt_bias_ref,
                  a_log_ref, d_skip_ref, ssm_g_ref, w_gk2_ref, b_gk2_ref, gla_g_ref, final_g_ref,
                  o_ref,
                  proj_ref, h_ref, y_ref, conv_ref, hk_ref, gcum_ref, cum_sc, k_sc,
                  st_hgrn, st_ret, st_ssd, st_gla):
    b_idx = pl.program_id(0)
    t_idx = pl.program_id(1)

    def project_vector_part(x_tile, tile_mod_ref):
        h_ref[...] = _modulated_norm(x_tile, tile_mod_ref, normg_ref)
        _in_proj_vector_part(h_ref, wt_ref, wt_gla_ref, proj_ref)
        _mixer_prep(layer, proj_ref, hk_ref, gcum_ref, lb_logits_ref, w_gk2_ref, b_gk2_ref)

    @pl.when(t_idx == 0)
    def _():
        st_hgrn[...] = jnp.zeros_like(st_hgrn)
        st_ret[...] = jnp.zeros_like(st_ret)
        st_ssd[...] = jnp.zeros_like(st_ssd)
        st_gla[...] = jnp.zeros_like(st_gla)
        conv_ref[...] = jnp.zeros_like(conv_ref)

    @pl.when((t_idx == 0) & (b_idx == 0))
    def _():
        project_vector_part(x_ref[...], mod_ref)

    _in_proj_scalar_part(h_ref, wt_ref, proj_ref)
    _vector_decay_mixers(proj_ref, y_ref, hk_ref, gcum_ref, cum_sc, k_sc, st_hgrn, st_gla,
                         hgrn_g_ref, gla_g_ref)
    project_vector_part(xn_ref[...], modn_ref)
    _scalar_decay_mixers(proj_ref, y_ref, conv_ref, st_ret, st_ssd, cos_ref, sin_ref,
                         ret_g_ref, conv_w_ref, conv_b_ref, dt_bias_ref, a_log_ref, d_skip_ref,
                         ssm_g_ref)
    gate = mod_ref[0:1, 2 * D_MODEL:3 * D_MODEL]
    x_new = x_ref[...] + gate * _dot(y_ref[...], w_out_ref[...])
    if final:
        x_new = _rms(x_new, D_MODEL) * final_g_ref[...]
    o_ref[...] = x_new


def _resident(shape):
    nd = len(shape)
    return pl.BlockSpec(shape, lambda b, t, _nd=nd: (0,) * _nd, pipeline_mode=pl.Buffered(1))


def _layer_call(layer, final, x, mod, params):
    batch, seq, _ = x.shape
    n_tiles = seq // TILE
    n_chunks = TILE // CHUNK

    def next_tile(t):
        return jnp.where(t == n_tiles - 1, 0, t + 1)

    def next_batch(b, t):
        return jnp.where(t == n_tiles - 1, jnp.minimum(b + 1, batch - 1), b)

    small = [params[k] for k in (
        "lb_logits", "hgrn_g", "ret_g", "conv_w", "conv_b", "dt_bias", "a_log", "d_skip",
        "ssm_g", "w_gk2", "b_gk2", "gla_g", "final_g")]
    in_specs = [
        pl.BlockSpec((None, TILE, D_MODEL), lambda b, t: (b, t, 0)),
        pl.BlockSpec((None, TILE, D_MODEL), lambda b, t: (next_batch(b, t), next_tile(t), 0)),
        pl.BlockSpec((None, SUBLANES, 3 * D_MODEL), lambda b, t: (b, 0, 0)),
        pl.BlockSpec((None, SUBLANES, 3 * D_MODEL), lambda b, t: (next_batch(b, t), 0, 0)),
        _resident((1, D_MODEL)),
        pl.BlockSpec((None, ORIG_W, D_MODEL), lambda b, t: (layer, 0, 0), pipeline_mode=pl.Buffered(1)),
        pl.BlockSpec((None, GLA_TAIL_ROWS, D_MODEL), lambda b, t: (layer, 0, 0),
                     pipeline_mode=pl.Buffered(1)),
        pl.BlockSpec((None, D_INNER, D_MODEL), lambda b, t: (layer, 0, 0), pipeline_mode=pl.Buffered(1)),
        pl.BlockSpec((TILE, HEAD_W), lambda b, t: (t, 0)),
        pl.BlockSpec((TILE, HEAD_W), lambda b, t: (t, 0)),
    ] + [_resident(a.shape) for a in small]
    scratch = [
        pltpu.VMEM((TILE, PROJ_W), F32),
        pltpu.VMEM((TILE, D_MODEL), BF16),
        pltpu.VMEM((TILE, D_INNER), BF16),
        pltpu.VMEM((SUBLANES, SSM_CONV_CH), F32),
        pltpu.VMEM((TILE, BRANCH_W), F32),
        pltpu.VMEM((TILE, GLA_KEY_W), F32),
        pltpu.VMEM((n_chunks, N_KEY_GROUPS, CHUNK, LANES), F32),
        pltpu.VMEM((n_chunks, N_KEY_GROUPS, CHUNK, LANES), F32),
        pltpu.VMEM((N_HEADS, HEAD_W, HEAD_W), F32),
        pltpu.VMEM((N_HEADS, HEAD_W, HEAD_W), F32),
        pltpu.VMEM((SSM_GROUPS, SSM_STATE, BRANCH_W // SSM_GROUPS), F32),
        pltpu.VMEM((N_HEADS // 2, HEAD_W, LANES), F32),
    ]
    return pl.pallas_call(
        functools.partial(_layer_kernel, layer, final),
        grid=(batch, n_tiles),
        in_specs=in_specs,
        out_specs=pl.BlockSpec((None, TILE, D_MODEL), lambda b, t: (b, t, 0)),
        out_shape=jax.ShapeDtypeStruct(x.shape, F32),
        scratch_shapes=scratch,
        compiler_params=pltpu.CompilerParams(
            dimension_semantics=("arbitrary", "arbitrary"),
            vmem_limit_bytes=VMEM_LIMIT_BYTES),
        name=f"mixer_layer{layer}",
    )(x, x, mod, mod, params["norm_g"], params["wt"], params["wt_gla"], params["w_out"], params["cos"], params["sin"], *small)


def _pad_lanes(v, lane0=0):
    return jnp.zeros((1, LANES), F32).at[0, lane0:lane0 + v.shape[0]].set(v)


def kernel(x, c, w_ada, b_ada, norm_g, w_in, hgrn_lb_logits, hgrn_onorm_g, ret_onorm_g, ssm_conv_w,
           ssm_conv_b, ssm_dt_bias, ssm_a_log, ssm_d, ssm_norm_g, gla_w_gk2, gla_b_gk2, gla_onorm_g,
           w_out, final_g):
    batch, seq, _ = x.shape
    c_pad = jnp.zeros((SUBLANES, D_MODEL), F32).at[:batch].set(c)
    mod_all = _ada_mod(c_pad, w_ada, b_ada)

    inv_freq = ROPE_BASE ** (-np.arange(0, HEAD_W, 2, dtype=np.float64) / HEAD_W)
    ang = np.arange(seq, dtype=np.float64)[:, None] * inv_freq[None, :]
    cos_tab = jnp.asarray(np.concatenate([np.cos(ang), np.cos(ang)], axis=-1), F32)
    sin_tab = jnp.asarray(np.concatenate([-np.sin(ang), np.sin(ang)], axis=-1), F32)

    wt = jnp.swapaxes(w_in, 1, 2).astype(BF16)
    wt_gla = wt[:, ORIG_GQ:, :]
    w_out_bf = w_out.astype(BF16)

    for layer in range(DEPTH):
        w_gk2_p = jnp.zeros((LANES, GLA_KEY_W), F32).at[LR_LANE0:].set(gla_w_gk2[layer])
        params = dict(
            norm_g=norm_g[layer][None, :], wt=wt, wt_gla=wt_gla, w_out=w_out_bf,
            cos=cos_tab, sin=sin_tab,
            lb_logits=hgrn_lb_logits, hgrn_g=hgrn_onorm_g[layer][None, :],
            ret_g=ret_onorm_g[layer][None, :], conv_w=ssm_conv_w[layer],
            conv_b=ssm_conv_b[layer][None, :], dt_bias=_pad_lanes(ssm_dt_bias[layer]),
            a_log=_pad_lanes(ssm_a_log[layer]), d_skip=_pad_lanes(ssm_d[layer]),
            ssm_g=ssm_norm_g[layer][None, :], w_gk2=w_gk2_p, b_gk2=gla_b_gk2[layer][None, :],
            gla_g=gla_onorm_g[layer][None, :], final_g=final_g[None, :])
        mod = jnp.broadcast_to(mod_all[layer][:batch, None, :], (batch, SUBLANES, 3 * D_MODEL))
        x = _layer_call(layer, layer == DEPTH - 1, x, mod, params)
    return x
```

```python
import functools
import math

import numpy as np
import jax
import jax.numpy as jnp
from jax import lax
from jax.experimental import pallas as pl
from jax.experimental.pallas import tpu as pltpu

F32 = jnp.float32
BF16 = jnp.bfloat16

D_MODEL = 1024
DEPTH = 2
BRANCH_W = 512
D_INNER = 4 * BRANCH_W
HEAD_W = 128
N_HEADS = 4
RET_DECAY_EXP0 = 5.0
ROPE_BASE = 10000.0
SSM_HEADS = 8
SSM_HEAD_DIM = 64
SSM_GROUPS = 2
SSM_STATE = 128
SSM_CONV = 4
SSM_CONV_CH = BRANCH_W + 2 * SSM_GROUPS * SSM_STATE
GLA_KEY_W = 256
GLA_LOWRANK = 16
GLA_GATE_TEMP = 16.0
EPS = 1e-6
LOG2_E = math.log2(math.e)

LANES = 128
SUBLANES = 8

C_AQ, C_AF, C_AI, C_AG = 0, 512, 1024, 1536
C_GQ, C_GK, C_GV, C_GG = 2048, 2304, 2560, 3072
C_LR = 3584
VEC_W = C_LR + LANES
C_RQ, C_RK, C_RV, C_RG = 3712, 4224, 4736, 5248
C_MZ, C_XBC = 5760, 6272
C_DT = 7296
PROJ_W = C_DT + LANES
ORIG_RQ = 2048
ORIG_DT = 5632
ORIG_GQ = ORIG_DT + SSM_HEADS
ORIG_LR = ORIG_GQ + 2 * GLA_KEY_W + 2 * BRANCH_W
ORIG_W = ORIG_LR + GLA_LOWRANK
GLA_TAIL_ROWS = ORIG_W - ORIG_GQ
LR_LANE0 = LANES - GLA_LOWRANK

TILE = 256
CHUNK = 64
SUB = 8
N_SUB = CHUNK // SUB
N_KEY_GROUPS = N_HEADS + N_HEADS // 2
VMEM_LIMIT_BYTES = 56 * 1024 * 1024


def _dot(a, b):
    return jnp.dot(a, b, preferred_element_type=F32)


def _dot_nt(a, b):
    return lax.dot_general(a, b, (((1,), (1,)), ((), ())), preferred_element_type=F32)


def _dot_tn(a, b):
    return lax.dot_general(a, b, (((0,), (0,)), ((), ())), preferred_element_type=F32)


def _dot3(m_bf16, x):
    hi = x.astype(BF16)
    r1 = x - hi.astype(F32)
    mid = r1.astype(BF16)
    lo = (r1 - mid.astype(F32)).astype(BF16)
    return _dot(m_bf16, hi) + _dot(m_bf16, mid) + _dot(m_bf16, lo)


def _dot3_rhs(x, m_bf16):
    hi = x.astype(BF16)
    r1 = x - hi.astype(F32)
    mid = r1.astype(BF16)
    lo = (r1 - mid.astype(F32)).astype(BF16)
    return _dot(hi, m_bf16) + _dot(mid, m_bf16) + _dot(lo, m_bf16)


def _silu(x):
    hx = 0.5 * x
    return hx + hx * jnp.tanh(hx)


def _log1p_exp_neg_abs(x):
    return jnp.log(1.0 + jnp.exp(-jnp.abs(x)))


def _log_sigmoid(x):
    return jnp.minimum(x, 0.0) - _log1p_exp_neg_abs(x)


def _softplus(x):
    return jnp.maximum(x, 0.0) + _log1p_exp_neg_abs(x)


def _logaddexp(a, b):
    return jnp.maximum(a, b) + _log1p_exp_neg_abs(a - b)


def _rms(x, width):
    return x * lax.rsqrt(jnp.sum(x * x, axis=-1, keepdims=True) * (1.0 / width) + EPS)


def _iota(shape, dim):
    return lax.broadcasted_iota(jnp.int32, shape, dim)


def _div_pow2(x, d):
    shift = d.bit_length() - 1
    assert 1 << shift == d
    return lax.shift_right_logical(x, shift)


def _lower_tri(n, block):
    r = _iota((n, n), 0)
    c = _iota((n, n), 1)
    keep = (c <= r) & (_div_pow2(r, block) == _div_pow2(c, block))
    return jnp.where(keep, 1.0, 0.0).astype(BF16)


def _ada_kernel(c_ref, w_ref, b_ref, o_ref):
    c_act = _silu(c_ref[...]).astype(BF16)
    o_ref[...] = _dot(c_act, w_ref[...].astype(BF16)) + b_ref[...]


def _ada_mod(c_pad, w_ada, b_ada):
    n_blk = 3
    return pl.pallas_call(
        _ada_kernel,
        grid=(DEPTH, n_blk),
        in_specs=[
            pl.BlockSpec((SUBLANES, D_MODEL), lambda l, j: (0, 0)),
            pl.BlockSpec((None, D_MODEL, D_MODEL), lambda l, j: (l, 0, j)),
            pl.BlockSpec((None, 1, D_MODEL), lambda l, j: (l, 0, j)),
        ],
        out_specs=pl.BlockSpec((None, SUBLANES, D_MODEL), lambda l, j: (l, 0, j)),
        out_shape=jax.ShapeDtypeStruct((DEPTH, SUBLANES, 3 * D_MODEL), F32),
        name="ada_mod",
    )(c_pad, w_ada, b_ada.reshape(DEPTH, 1, 3 * D_MODEL))


HEADS_PER_MATMUL = 2
N_OFF_USED = SUB * (N_SUB * (N_SUB - 1) // 2)
N_OFF = -(-N_OFF_USED // LANES) * LANES


def _key_group_terms(q, k, cum, cum_sc, k_sc, heads_per_group):
    lane_w = LANES // heads_per_group

    def head_lanes(x, i):
        if heads_per_group == 1:
            return x
        lane = _iota(x.shape, 1)
        return jnp.where((lane >= i * lane_w) & (lane < (i + 1) * lane_w), x, 0.0)

    cum_sc[...] = cum
    k_sc[...] = k
    cum_last = cum_sc[CHUNK - 1:CHUNK, :]
    q_exp = q * jnp.exp2(cum)
    k_end = k * jnp.exp2(cum_last - cum)
    dec = jnp.exp2(cum_last)

    qs_parts = [jnp.zeros((SUB, LANES), F32)]
    ks_parts = []
    for i in range(1, N_SUB):
        ref_i = cum_sc[i * SUB - 1:i * SUB, :]
        qs_parts.append(q[i * SUB:(i + 1) * SUB, :] * jnp.exp2(cum[i * SUB:(i + 1) * SUB, :] - ref_i))
        ks_parts.append(k[:i * SUB, :] * jnp.exp2(ref_i - cum[:i * SUB, :]))
    ks_parts.append(jnp.zeros((N_OFF - N_OFF_USED, LANES), F32))
    qs = jnp.concatenate(qs_parts, axis=0)
    ks = jnp.concatenate(ks_parts, axis=0).astype(BF16)

    lane_c = _iota((SUB, CHUNK), 1)
    row_c = _iota((SUB, CHUNK), 0)
    diag_parts = [[] for _ in range(heads_per_group)]
    for i in range(N_SUB):
        acc = [jnp.zeros((SUB, CHUNK), F32) for _ in range(heads_per_group)]
        q_i = q[i * SUB:(i + 1) * SUB, :]
        cum_i = cum[i * SUB:(i + 1) * SUB, :]
        for j in range(SUB):
            s = i * SUB + j
            w = q_i * jnp.exp2(cum_i - cum_sc[s:s + 1, :]) * k_sc[s:s + 1, :]
            for hd in range(heads_per_group):
                a_col = jnp.sum(head_lanes(w, hd), axis=-1, keepdims=True)
                acc[hd] = jnp.where(lane_c == s, a_col, acc[hd])
        causal = (row_c + i * SUB) >= lane_c
        for hd in range(heads_per_group):
            diag_parts[hd].append(jnp.where(causal, acc[hd], 0.0))
    a_diag = [jnp.concatenate(p, axis=0) for p in diag_parts]
    return dict(
        qs=[head_lanes(qs, i).astype(BF16) for i in range(heads_per_group)],
        q_exp=[head_lanes(q_exp, i).astype(BF16) for i in range(heads_per_group)],
        k_end=[head_lanes(k_end, i).astype(BF16) for i in range(heads_per_group)],
        ks=ks, dec=dec, a_diag=a_diag)


def _vector_decay_chunk(groups, v_all, heads_per_group, st_ref):
    n_groups = len(groups)
    n_heads = n_groups * heads_per_group
    rows_all = n_heads * CHUNK
    zeros_bf = jnp.zeros((CHUNK, LANES), BF16)

    lhs_rows = []
    for h in range(n_heads):
        g, i = divmod(h, heads_per_group)
        lhs_rows.append(jnp.concatenate(
            [groups[g]["qs"][i] if gg == g else zeros_bf for gg in range(n_groups)], axis=1))
    ks_cat = jnp.concatenate([grp["ks"] for grp in groups], axis=1)
    p_all = _dot_nt(jnp.concatenate(lhs_rows, axis=0), ks_cat)
    row_blk = _div_pow2(_iota((rows_all, N_OFF), 0) & (CHUNK - 1), SUB)
    col = _iota((rows_all, N_OFF), 1)
    lo = lax.shift_right_logical(row_blk * (row_blk - 1), 1) * SUB
    p_all = jnp.where((col >= lo) & (col < lo + row_blk * SUB), p_all, 0.0)

    a_all = jnp.concatenate([a for grp in groups for a in grp["a_diag"]], axis=0)
    a_all = jnp.concatenate([a_all, jnp.zeros((rows_all, LANES - CHUNK), F32)], axis=1)
    scores = jnp.concatenate([p_all, a_all], axis=1).astype(BF16)
    width = v_all.shape[1]
    v_rows = jnp.concatenate(
        [v_all[:i * SUB, :] for i in range(1, N_SUB)]
        + [jnp.zeros((N_OFF - N_OFF_USED, width), F32), v_all, jnp.zeros((LANES - CHUNK, width), F32)],
        axis=0).astype(BF16)
    r = _dot(scores, v_rows)
    outs = [r[h * CHUNK:(h + 1) * CHUNK, h * HEAD_W:(h + 1) * HEAD_W] for h in range(n_heads)]

    v_bf = v_all.astype(BF16)
    if heads_per_group == 1:
        st = st_ref[...]
        st_bf = st.reshape(n_heads * HEAD_W, LANES).astype(BF16)
        qe_all = jnp.concatenate([grp["q_exp"][0] for grp in groups], axis=0)
        inter = _dot_nt(qe_all, st_bf)
        ke_cat = jnp.concatenate([grp["k_end"][0] for grp in groups], axis=1)
        upd = _dot_tn(v_bf, ke_cat)
        for h in range(n_heads):
            outs[h] = outs[h] + inter[h * CHUNK:(h + 1) * CHUNK, h * HEAD_W:(h + 1) * HEAD_W]
            st_ref[h] = st[h] * groups[h]["dec"] + upd[h * HEAD_W:(h + 1) * HEAD_W, h * LANES:(h + 1) * LANES]
    else:
        for g, grp in enumerate(groups):
            st = st_ref[g]
            qe_rows = jnp.concatenate(grp["q_exp"], axis=0)
            inter = _dot_nt(qe_rows, st.astype(BF16))
            ke_rows = jnp.concatenate(grp["k_end"], axis=0)
            v_g = [v_bf[:, (g * heads_per_group + i) * HEAD_W:(g * heads_per_group + i + 1) * HEAD_W]
                   for i in range(heads_per_group)]
            upd = _dot_tn(jnp.concatenate(v_g, axis=0), ke_rows)
            for i in range(heads_per_group):
                h = g * heads_per_group + i
                outs[h] = outs[h] + inter[i * CHUNK:(i + 1) * CHUNK, :]
            st_ref[g] = st * grp["dec"] + upd
    return outs


def _modulated_norm(x, mod_ref, normg_ref):
    shift = mod_ref[0:1, 0:D_MODEL]
    scale = mod_ref[0:1, D_MODEL:2 * D_MODEL]
    h = (_rms(x, D_MODEL) * normg_ref[...]) * (1.0 + scale) + shift
    return h.astype(BF16)


def _in_proj(h_ref, wt_ref, proj_ref, dst, row0, row1):
    n_blk = 1024
    for r0 in range(row0, row1, n_blk):
        r1 = min(r0 + n_blk, row1)
        proj_ref[:, dst + r0 - row0:dst + r1 - row0] = _dot_nt(h_ref[...], wt_ref[r0:r1, :])


def _in_proj_vector_part(h_ref, wt_ref, wt_gla_ref, proj_ref):
    _in_proj(h_ref, wt_ref, proj_ref, C_AQ, 0, ORIG_RQ)
    _in_proj(h_ref, wt_gla_ref, proj_ref, C_GQ, 0, ORIG_LR - ORIG_GQ)
    _in_proj(h_ref, wt_gla_ref, proj_ref, C_LR, GLA_TAIL_ROWS - LANES, GLA_TAIL_ROWS)


def _in_proj_scalar_part(h_ref, wt_ref, proj_ref):
    _in_proj(h_ref, wt_ref, proj_ref, C_RQ, ORIG_RQ, ORIG_DT)
    _in_proj(h_ref, wt_ref, proj_ref, C_DT, ORIG_DT, ORIG_DT + LANES)


def _mixer_prep(layer, proj_ref, hk_ref, gcum_ref, lb_logits_ref, w_gk2_ref, b_gk2_ref):
    tri_chunk = _lower_tri(TILE, CHUNK)

    lg = [lb_logits_ref[i:i + 1, :] for i in range(DEPTH)]
    lg_max = functools.reduce(jnp.maximum, lg)
    lg_exp = [jnp.exp(r - lg_max) for r in lg]
    lg_den = functools.reduce(lambda a, b: a + b, lg_exp)
    lower = jnp.zeros((1, BRANCH_W), F32)
    for i in range(1, layer + 1):
        lower = lower + lg_exp[i] / lg_den
    log_lb = jnp.log(lower)
    log1m_lb = jnp.log1p(-lower)
    proj_ref[:, C_AQ:C_AQ + BRANCH_W] = _silu(proj_ref[:, C_AQ:C_AQ + BRANCH_W])
    log_f = _logaddexp(log_lb, log1m_lb + _log_sigmoid(proj_ref[:, C_AF:C_AF + BRANCH_W]))
    hk_ref[...] = 1.0 - jnp.exp(log_f)
    proj_ref[:, C_AF:C_AF + BRANCH_W] = _dot3(tri_chunk, log_f * LOG2_E)

    low_rank = proj_ref[:, C_LR:C_LR + LANES]
    gk_gate = _dot(low_rank.astype(BF16), w_gk2_ref[...].astype(BF16)) + b_gk2_ref[...]
    gcum_ref[...] = _dot3(tri_chunk, _log_sigmoid(gk_gate) * (LOG2_E / GLA_GATE_TEMP))
    gla_dk = GLA_KEY_W // N_HEADS
    proj_ref[:, C_GQ:C_GQ + GLA_KEY_W] = proj_ref[:, C_GQ:C_GQ + GLA_KEY_W] * (gla_dk ** -0.5)


def _vector_decay_mixers(proj_ref, y_ref, hk_ref, gcum_ref, cum_sc, k_sc, st_hgrn, st_gla,
                         hgrn_g_ref, gla_g_ref):
    def chunk_body(ci):
        r0 = ci * CHUNK
        rows = slice(r0, r0 + CHUNK)
        groups = []
        for hd in range(N_HEADS):
            cs = slice(hd * HEAD_W, (hd + 1) * HEAD_W)
            groups.append(_key_group_terms(
                proj_ref[rows, C_AQ + cs.start:C_AQ + cs.stop], hk_ref[rows, cs],
                proj_ref[rows, C_AF + cs.start:C_AF + cs.stop],
                cum_sc.at[ci, hd], k_sc.at[ci, hd], 1))
        outs = []
        for p0 in range(0, N_HEADS, HEADS_PER_MATMUL):
            outs += _vector_decay_chunk(
                groups[p0:p0 + HEADS_PER_MATMUL],
                proj_ref[rows, C_AI + p0 * HEAD_W:C_AI + (p0 + HEADS_PER_MATMUL) * HEAD_W],
                1, st_hgrn.at[pl.ds(p0, HEADS_PER_MATMUL)])
        for hd, o in enumerate(outs):
            cs = slice(hd * HEAD_W, (hd + 1) * HEAD_W)
            g_act = _silu(proj_ref[rows, C_AG + cs.start:C_AG + cs.stop])
            y_ref[rows, cs] = (_rms(o, HEAD_W) * hgrn_g_ref[:, cs] * g_act).astype(BF16)
        groups = []
        for pr in range(N_HEADS // 2):
            ks_ = slice(pr * LANES, (pr + 1) * LANES)
            groups.append(_key_group_terms(
                proj_ref[rows, C_GQ + ks_.start:C_GQ + ks_.stop],
                proj_ref[rows, C_GK + ks_.start:C_GK + ks_.stop],
                gcum_ref[rows, ks_],
                cum_sc.at[ci, N_HEADS + pr], k_sc.at[ci, N_HEADS + pr], 2))
        outs = []
        for pr in range(N_HEADS // 2):
            outs += _vector_decay_chunk(
                groups[pr:pr + 1],
                proj_ref[rows, C_GV + 2 * pr * HEAD_W:C_GV + 2 * (pr + 1) * HEAD_W],
                2, st_gla.at[pl.ds(pr, 1)])
        for hd, o in enumerate(outs):
            cs = slice(hd * HEAD_W, (hd + 1) * HEAD_W)
            g_act = _silu(proj_ref[rows, C_GG + cs.start:C_GG + cs.stop])
            y_ref[rows, 3 * BRANCH_W + cs.start:3 * BRANCH_W + cs.stop] = (
                _rms(o, HEAD_W) * gla_g_ref[:, cs] * g_act).astype(BF16)

    for ci in range(TILE // CHUNK):
        chunk_body(ci)


def _scalar_decay_mixers(proj_ref, y_ref, conv_ref, st_ret, st_ssd, cos_ref, sin_ref,
                         ret_g_ref, conv_w_ref, conv_b_ref, dt_bias_ref, a_log_ref, d_skip_ref,
                         ssm_g_ref):
    tri_tile = _lower_tri(TILE, TILE)
    def cols(c0, width):
        return proj_ref[:, c0:c0 + width]

    tail = cols(C_DT, LANES)

    t_col = _iota((TILE, TILE), 0)
    s_row = _iota((TILE, TILE), 1)
    t_minus_s = (t_col - s_row).astype(F32)
    causal_tile = t_col >= s_row
    t_plus1 = (_iota((TILE, HEAD_W), 0) + 1).astype(F32)
    s_to_end = (TILE - 1 - _iota((TILE, HEAD_W), 0)).astype(F32)
    cos_t = cos_ref[...]
    sin_t = sin_ref[...]
    for hd in range(N_HEADS):
        cs = slice(hd * HEAD_W, (hd + 1) * HEAD_W)
        log2_gamma = math.log1p(-(2.0 ** -(RET_DECAY_EXP0 + hd))) * LOG2_E
        rq = cols(C_RQ + cs.start, HEAD_W)
        rk = cols(C_RK + cs.start, HEAD_W)
        q = rq * cos_t + pltpu.roll(rq, HEAD_W // 2, 1) * sin_t
        k = (rk * cos_t + pltpu.roll(rk, HEAD_W // 2, 1) * sin_t) * (HEAD_W ** -0.5)
        v_bf = cols(C_RV + cs.start, HEAD_W).astype(BF16)
        q_bf = q.astype(BF16)
        decay = jnp.where(causal_tile, jnp.exp2(t_minus_s * log2_gamma), 0.0)
        scores = _dot_nt(q_bf, k.astype(BF16)) * decay
        st = st_ret[hd]
        o = (_dot(scores.astype(BF16), v_bf)
             + jnp.exp2(t_plus1 * log2_gamma) * _dot(q_bf, st.astype(BF16)))
        k_end = k * jnp.exp2(s_to_end * log2_gamma)
        st_ret[hd] = st * (2.0 ** (TILE * log2_gamma)) + _dot_tn(k_end.astype(BF16), v_bf)
        g_act = _silu(cols(C_RG + cs.start, HEAD_W))
        y_ref[:, BRANCH_W + cs.start:BRANCH_W + cs.stop] = (
            _rms(o, HEAD_W) * ret_g_ref[:, cs] * g_act).astype(BF16)

    u = cols(C_XBC, SSM_CONV_CH)
    prev = conv_ref[...]
    row8 = _iota((SUBLANES, SSM_CONV_CH), 0)
    xbc = conv_b_ref[...] + conv_w_ref[SSM_CONV - 1:SSM_CONV, :] * u
    for d in range(1, SSM_CONV):
        u_d = pltpu.roll(u, d, 0)
        head = jnp.where(row8 < d, pltpu.roll(prev, d, 0), u_d[0:SUBLANES, :])
        u_d = jnp.concatenate([head, u_d[SUBLANES:, :]], axis=0)
        xbc = xbc + conv_w_ref[SSM_CONV - 1 - d:SSM_CONV - d, :] * u_d
    conv_ref[...] = u[TILE - SUBLANES:TILE, :]
    xbc = _silu(xbc)
    xs = xbc[:, 0:BRANCH_W]
    dt = _softplus(tail + dt_bias_ref[...])
    log_a = dt * (-LOG2_E * jnp.exp(a_log_ref[...]))
    cum = _dot3(tri_tile, log_a)
    cum_t = cum.T

    e_r = _iota((LANES, BRANCH_W), 0)
    e_c = _iota((LANES, BRANCH_W), 1)
    expand = jnp.where(e_r == _div_pow2(e_c, SSM_HEAD_DIM), 1.0, 0.0).astype(BF16)
    dt_e = _dot(dt.astype(BF16), expand)
    cum_e = _dot3_rhs(cum, expand)
    cum_last_e = cum_e[TILE - 1:TILE, :]
    d_e = _dot3_rhs(jnp.broadcast_to(d_skip_ref[...], (SUBLANES, LANES)), expand)[0:1, :]
    xdt = xs * dt_e
    x_end = (xdt * jnp.exp2(cum_last_e - cum_e)).astype(BF16)
    hpg = SSM_HEADS // SSM_GROUPS
    gw = hpg * SSM_HEAD_DIM
    o_parts = []
    for g in range(SSM_GROUPS):
        b_bf = xbc[:, BRANCH_W + g * SSM_STATE:BRANCH_W + (g + 1) * SSM_STATE].astype(BF16)
        c_bf = xbc[:, BRANCH_W + (SSM_GROUPS + g) * SSM_STATE:
                   BRANCH_W + (SSM_GROUPS + g + 1) * SSM_STATE].astype(BF16)
        cb = _dot_nt(c_bf, b_bf)
        st = st_ssd[g]
        inter = _dot(c_bf, st.astype(BF16)) * jnp.exp2(cum_e[:, g * gw:(g + 1) * gw])
        for hh in range(hpg):
            hd = g * hpg + hh
            rel = cum[:, hd:hd + 1] - cum_t[hd:hd + 1, :]
            w = jnp.where(causal_tile, cb * jnp.exp2(rel), 0.0)
            o_parts.append(_dot(w.astype(BF16), xdt[:, hd * SSM_HEAD_DIM:(hd + 1) * SSM_HEAD_DIM].astype(BF16))
                           + inter[:, hh * SSM_HEAD_DIM:(hh + 1) * SSM_HEAD_DIM])
        st_ssd[g] = (st * jnp.exp2(cum_last_e[:, g * gw:(g + 1) * gw])
                     + _dot_tn(b_bf, x_end[:, g * gw:(g + 1) * gw]))
    o_ssd = jnp.concatenate(o_parts, axis=1)
    y_ssd = (o_ssd + d_e * xs) * _silu(cols(C_MZ, BRANCH_W))
    gn = BRANCH_W // SSM_GROUPS
    for g in range(SSM_GROUPS):
        cs = slice(g * gn, (g + 1) * gn)
        y_ref[:, 2 * BRANCH_W + cs.start:2 * BRANCH_W + cs.stop] = (
            _rms(y_ssd[:, cs], gn) * ssm_g_ref[:, cs]).astype(BF16)


def _layer_kernel(layer, final,
                  x_ref, xn_ref, mod_ref, modn_ref, normg_ref, wt_ref, wt_gla_ref, w_out_ref,
                  cos_ref, sin_ref,
                  lb_logits_ref, hgrn_g_ref, ret_g_ref, conv_w_ref, conv_b_ref, dt_bias_ref,
                  a_log_ref, d_skip_ref, ssm_g_ref, w_gk2_ref, b_gk2_ref, gla_g_ref, final_g_ref,
                  o_ref,
                  proj_ref, h_ref, y_ref, conv_ref, hk_ref, gcum_ref, cum_sc, k_sc,
                  st_hgrn, st_ret, st_ssd, st_gla):
    b_idx = pl.program_id(0)
    t_idx = pl.program_id(1)

    def project_vector_part(x_tile, tile_mod_ref):
        h_ref[...] = _modulated_norm(x_tile, tile_mod_ref, normg_ref)
        _in_proj_vector_part(h_ref, wt_ref, wt_gla_ref, proj_ref)
        _mixer_prep(layer, proj_ref, hk_ref, gcum_ref, lb_logits_ref, w_gk2_ref, b_gk2_ref)

    @pl.when(t_idx == 0)
    def _():
        st_hgrn[...] = jnp.zeros_like(st_hgrn)
        st_ret[...] = jnp.zeros_like(st_ret)
        st_ssd[...] = jnp.zeros_like(st_ssd)
        st_gla[...] = jnp.zeros_like(st_gla)
        conv_ref[...] = jnp.zeros_like(conv_ref)

    @pl.when((t_idx == 0) & (b_idx == 0))
    def _():
        project_vector_part(x_ref[...], mod_ref)

    _in_proj_scalar_part(h_ref, wt_ref, proj_ref)
    _vector_decay_mixers(proj_ref, y_ref, hk_ref, gcum_ref, cum_sc, k_sc, st_hgrn, st_gla,
                         hgrn_g_ref, gla_g_ref)
    project_vector_part(xn_ref[...], modn_ref)
    _scalar_decay_mixers(proj_ref, y_ref, conv_ref, st_ret, st_ssd, cos_ref, sin_ref,
                         ret_g_ref, conv_w_ref, conv_b_ref, dt_bias_ref, a_log_ref, d_skip_ref,
                         ssm_g_ref)
    gate = mod_ref[0:1, 2 * D_MODEL:3 * D_MODEL]
    x_new = x_ref[...] + gate * _dot(y_ref[...], w_out_ref[...])
    if final:
        x_new = _rms(x_new, D_MODEL) * final_g_ref[...]
    o_ref[...] = x_new


def _resident(shape):
    nd = len(shape)
    return pl.BlockSpec(shape, lambda b, t, _nd=nd: (0,) * _nd, pipeline_mode=pl.Buffered(1))


def _layer_call(layer, final, x, mod, params):
    batch, seq, _ = x.shape
    n_tiles = seq // TILE
    n_chunks = TILE // CHUNK

    def next_tile(t):
        return jnp.where(t == n_tiles - 1, 0, t + 1)

    def next_batch(b, t):
        return jnp.where(t == n_tiles - 1, jnp.minimum(b + 1, batch - 1), b)

    small = [params[k] for k in (
        "lb_logits", "hgrn_g", "ret_g", "conv_w", "conv_b", "dt_bias", "a_log", "d_skip",
        "ssm_g", "w_gk2", "b_gk2", "gla_g", "final_g")]
    in_specs = [
        pl.BlockSpec((None, TILE, D_MODEL), lambda b, t: (b, t, 0)),
        pl.BlockSpec((None, TILE, D_MODEL), lambda b, t: (next_batch(b, t), next_tile(t), 0)),
        pl.BlockSpec((None, SUBLANES, 3 * D_MODEL), lambda b, t: (b, 0, 0)),
        pl.BlockSpec((None, SUBLANES, 3 * D_MODEL), lambda b, t: (next_batch(b, t), 0, 0)),
        _resident((1, D_MODEL)),
        pl.BlockSpec((None, ORIG_W, D_MODEL), lambda b, t: (layer, 0, 0), pipeline_mode=pl.Buffered(1)),
        pl.BlockSpec((None, GLA_TAIL_ROWS, D_MODEL), lambda b, t: (layer, 0, 0),
                     pipeline_mode=pl.Buffered(1)),
        pl.BlockSpec((None, D_INNER, D_MODEL), lambda b, t: (layer, 0, 0), pipeline_mode=pl.Buffered(1)),
        pl.BlockSpec((TILE, HEAD_W), lambda b, t: (t, 0)),
        pl.BlockSpec((TILE, HEAD_W), lambda b, t: (t, 0)),
    ] + [_resident(a.shape) for a in small]
    scratch = [
        pltpu.VMEM((TILE, PROJ_W), F32),
        pltpu.VMEM((TILE, D_MODEL), BF16),
        pltpu.VMEM((TILE, D_INNER), BF16),
        pltpu.VMEM((SUBLANES, SSM_CONV_CH), F32),
        pltpu.VMEM((TILE, BRANCH_W), F32),
        pltpu.VMEM((TILE, GLA_KEY_W), F32),
        pltpu.VMEM((n_chunks, N_KEY_GROUPS, CHUNK, LANES), F32),
        pltpu.VMEM((n_chunks, N_KEY_GROUPS, CHUNK, LANES), F32),
        pltpu.VMEM((N_HEADS, HEAD_W, HEAD_W), F32),
        pltpu.VMEM((N_HEADS, HEAD_W, HEAD_W), F32),
        pltpu.VMEM((SSM_GROUPS, SSM_STATE, BRANCH_W // SSM_GROUPS), F32),
        pltpu.VMEM((N_HEADS // 2, HEAD_W, LANES), F32),
    ]
    return pl.pallas_call(
        functools.partial(_layer_kernel, layer, final),
        grid=(batch, n_tiles),
        in_specs=in_specs,
        out_specs=pl.BlockSpec((None, TILE, D_MODEL), lambda b, t: (b, t, 0)),
        out_shape=jax.ShapeDtypeStruct(x.shape, F32),
        scratch_shapes=scratch,
        compiler_params=pltpu.CompilerParams(
            dimension_semantics=("arbitrary", "arbitrary"),
            vmem_limit_bytes=VMEM_LIMIT_BYTES),
        name=f"mixer_layer{layer}",
    )(x, x, mod, mod, params["norm_g"], params["wt"], params["wt_gla"], params["w_out"], params["cos"], params["sin"], *small)


def _pad_lanes(v, lane0=0):
    return jnp.zeros((1, LANES), F32).at[0, lane0:lane0 + v.shape[0]].set(v)


def kernel(x, c, w_ada, b_ada, norm_g, w_in, hgrn_lb_logits, hgrn_onorm_g, ret_onorm_g, ssm_conv_w,
           ssm_conv_b, ssm_dt_bias, ssm_a_log, ssm_d, ssm_norm_g, gla_w_gk2, gla_b_gk2, gla_onorm_g,
           w_out, final_g):
    batch, seq, _ = x.shape
    c_pad = jnp.zeros((SUBLANES, D_MODEL), F32).at[:batch].set(c)
    mod_all = _ada_mod(c_pad, w_ada, b_ada)

    inv_freq = ROPE_BASE ** (-np.arange(0, HEAD_W, 2, dtype=np.float64) / HEAD_W)
    ang = np.arange(seq, dtype=np.float64)[:, None] * inv_freq[None, :]
    cos_tab = jnp.asarray(np.concatenate([np.cos(ang), np.cos(ang)], axis=-1), F32)
    sin_tab = jnp.asarray(np.concatenate([-np.sin(ang), np.sin(ang)], axis=-1), F32)

    wt = jnp.swapaxes(w_in, 1, 2).astype(BF16)
    wt_gla = wt[:, ORIG_GQ:, :]
    w_out_bf = w_out.astype(BF16)

    for layer in range(DEPTH):
        w_gk2_p = jnp.zeros((LANES, GLA_KEY_W), F32).at[LR_LANE0:].set(gla_w_gk2[layer])
        params = dict(
            norm_g=norm_g[layer][None, :], wt=wt, wt_gla=wt_gla, w_out=w_out_bf,
            cos=cos_tab, sin=sin_tab,
            lb_logits=hgrn_lb_logits, hgrn_g=hgrn_onorm_g[layer][None, :],
            ret_g=ret_onorm_g[layer][None, :], conv_w=ssm_conv_w[layer],
            conv_b=ssm_conv_b[layer][None, :], dt_bias=_pad_lanes(ssm_dt_bias[layer]),
            a_log=_pad_lanes(ssm_a_log[layer]), d_skip=_pad_lanes(ssm_d[layer]),
            ssm_g=ssm_norm_g[layer][None, :], w_gk2=w_gk2_p, b_gk2=gla_b_gk2[layer][None, :],
            gla_g=gla_onorm_g[layer][None, :], final_g=final_g[None, :])
        mod = jnp.broadcast_to(mod_all[layer][:batch, None, :], (batch, SUBLANES, 3 * D_MODEL))
        x = _layer_call(layer, layer == DEPTH - 1, x, mod, params)
    return x
```

```python
import functools
import math

import numpy as np
import jax
import jax.numpy as jnp
from jax import lax
from jax.experimental import pallas as pl
from jax.experimental.pallas import tpu as pltpu

F32 = jnp.float32
BF16 = jnp.bfloat16

D_MODEL = 1024
DEPTH = 2
BRANCH_W = 512
D_INNER = 4 * BRANCH_W
HEAD_W = 128
N_HEADS = 4
RET_DECAY_EXP0 = 5.0
ROPE_BASE = 10000.0
SSM_HEADS = 8
SSM_HEAD_DIM = 64
SSM_GROUPS = 2
SSM_STATE = 128
SSM_CONV = 4
SSM_CONV_CH = BRANCH_W + 2 * SSM_GROUPS * SSM_STATE
GLA_KEY_W = 256
GLA_LOWRANK = 16
GLA_GATE_TEMP = 16.0
EPS = 1e-6
LOG2_E = math.log2(math.e)

LANES = 128
SUBLANES = 8

C_AQ, C_AF, C_AI, C_AG = 0, 512, 1024, 1536
C_GQ, C_GK, C_GV, C_GG = 2048, 2304, 2560, 3072
C_LR = 3584
VEC_W = C_LR + LANES
C_RQ, C_RK, C_RV, C_RG = 3712, 4224, 4736, 5248
C_MZ, C_XBC = 5760, 6272
C_DT = 7296
PROJ_W = C_DT + LANES
ORIG_RQ = 2048
ORIG_DT = 5632
ORIG_GQ = ORIG_DT + SSM_HEADS
ORIG_LR = ORIG_GQ + 2 * GLA_KEY_W + 2 * BRANCH_W
ORIG_W = ORIG_LR + GLA_LOWRANK
GLA_TAIL_ROWS = ORIG_W - ORIG_GQ
LR_LANE0 = LANES - GLA_LOWRANK

TILE = 256
CHUNK = 64
SUB = 8
N_SUB = CHUNK // SUB
N_KEY_GROUPS = N_HEADS + N_HEADS // 2
VMEM_LIMIT_BYTES = 56 * 1024 * 1024


def _dot(a, b):
    return jnp.dot(a, b, preferred_element_type=F32)


def _dot_nt(a, b):
    return lax.dot_general(a, b, (((1,), (1,)), ((), ())), preferred_element_type=F32)


def _dot_tn(a, b):
    return lax.dot_general(a, b, (((0,), (0,)), ((), ())), preferred_element_type=F32)


def _dot3(m_bf16, x):
    hi = x.astype(BF16)
    r1 = x - hi.astype(F32)
    mid = r1.astype(BF16)
    lo = (r1 - mid.astype(F32)).astype(BF16)
    return _dot(m_bf16, hi) + _dot(m_bf16, mid) + _dot(m_bf16, lo)


def _dot3_rhs(x, m_bf16):
    hi = x.astype(BF16)
    r1 = x - hi.astype(F32)
    mid = r1.astype(BF16)
    lo = (r1 - mid.astype(F32)).astype(BF16)
    return _dot(hi, m_bf16) + _dot(mid, m_bf16) + _dot(lo, m_bf16)


def _silu(x):
    hx = 0.5 * x
    return hx + hx * jnp.tanh(hx)


def _log1p_exp_neg_abs(x):
    return jnp.log(1.0 + jnp.exp(-jnp.abs(x)))


def _log_sigmoid(x):
    return jnp.minimum(x, 0.0) - _log1p_exp_neg_abs(x)


def _softplus(x):
    return jnp.maximum(x, 0.0) + _log1p_exp_neg_abs(x)


def _logaddexp(a, b):
    return jnp.maximum(a, b) + _log1p_exp_neg_abs(a - b)


def _rms(x, width):
    return x * lax.rsqrt(jnp.sum(x * x, axis=-1, keepdims=True) * (1.0 / width) + EPS)


def _iota(shape, dim):
    return lax.broadcasted_iota(jnp.int32, shape, dim)


def _div_pow2(x, d):
    shift = d.bit_length() - 1
    assert 1 << shift == d
    return lax.shift_right_logical(x, shift)


def _lower_tri(n, block):
    r = _iota((n, n), 0)
    c = _iota((n, n), 1)
    keep = (c <= r) & (_div_pow2(r, block) == _div_pow2(c, block))
    return jnp.where(keep, 1.0, 0.0).astype(BF16)


def _ada_kernel(c_ref, w_ref, b_ref, o_ref):
    c_act = _silu(c_ref[...]).astype(BF16)
    o_ref[...] = _dot(c_act, w_ref[...].astype(BF16)) + b_ref[...]


def _ada_mod(c_pad, w_ada, b_ada):
    n_blk = 3
    return pl.pallas_call(
        _ada_kernel,
        grid=(DEPTH, n_blk),
        in_specs=[
            pl.BlockSpec((SUBLANES, D_MODEL), lambda l, j: (0, 0)),
            pl.BlockSpec((None, D_MODEL, D_MODEL), lambda l, j: (l, 0, j)),
            pl.BlockSpec((None, 1, D_MODEL), lambda l, j: (l, 0, j)),
        ],
        out_specs=pl.BlockSpec((None, SUBLANES, D_MODEL), lambda l, j: (l, 0, j)),
        out_shape=jax.ShapeDtypeStruct((DEPTH, SUBLANES, 3 * D_MODEL), F32),
        name="ada_mod",
    )(c_pad, w_ada, b_ada.reshape(DEPTH, 1, 3 * D_MODEL))


HEADS_PER_MATMUL = 2
N_OFF_USED = SUB * (N_SUB * (N_SUB - 1) // 2)
N_OFF = -(-N_OFF_USED // LANES) * LANES


def _key_group_terms(q, k, cum, cum_sc, k_sc, heads_per_group):
    lane_w = LANES // heads_per_group

    def head_lanes(x, i):
        if heads_per_group == 1:
            return x
        lane = _iota(x.shape, 1)
        return jnp.where((lane >= i * lane_w) & (lane < (i + 1) * lane_w), x, 0.0)

    cum_sc[...] = cum
    k_sc[...] = k
    cum_last = cum_sc[CHUNK - 1:CHUNK, :]
    q_exp = q * jnp.exp2(cum)
    k_end = k * jnp.exp2(cum_last - cum)
    dec = jnp.exp2(cum_last)

    qs_parts = [jnp.zeros((SUB, LANES), F32)]
    ks_parts = []
    for i in range(1, N_SUB):
        ref_i = cum_sc[i * SUB - 1:i * SUB, :]
        qs_parts.append(q[i * SUB:(i + 1) * SUB, :] * jnp.exp2(cum[i * SUB:(i + 1) * SUB, :] - ref_i))
        ks_parts.append(k[:i * SUB, :] * jnp.exp2(ref_i - cum[:i * SUB, :]))
    ks_parts.append(jnp.zeros((N_OFF - N_OFF_USED, LANES), F32))
    qs = jnp.concatenate(qs_parts, axis=0)
    ks = jnp.concatenate(ks_parts, axis=0).astype(BF16)

    lane_c = _iota((SUB, CHUNK), 1)
    row_c = _iota((SUB, CHUNK), 0)
    diag_parts = [[] for _ in range(heads_per_group)]
    for i in range(N_SUB):
        acc = [jnp.zeros((SUB, CHUNK), F32) for _ in range(heads_per_group)]
        q_i = q[i * SUB:(i + 1) * SUB, :]
        cum_i = cum[i * SUB:(i + 1) * SUB, :]
        for j in range(SUB):
            s = i * SUB + j
            w = q_i * jnp.exp2(cum_i - cum_sc[s:s + 1, :]) * k_sc[s:s + 1, :]
            for hd in range(heads_per_group):
                a_col = jnp.sum(head_lanes(w, hd), axis=-1, keepdims=True)
                acc[hd] = jnp.where(lane_c == s, a_col, acc[hd])
        causal = (row_c + i * SUB) >= lane_c
        for hd in range(heads_per_group):
            diag_parts[hd].append(jnp.where(causal, acc[hd], 0.0))
    a_diag = [jnp.concatenate(p, axis=0) for p in diag_parts]
    return dict(
        qs=[head_lanes(qs, i).astype(BF16) for i in range(heads_per_group)],
        q_exp=[head_lanes(q_exp, i).astype(BF16) for i in range(heads_per_group)],
        k_end=[head_lanes(k_end, i).astype(BF16) for i in range(heads_per_group)],
        ks=ks, dec=dec, a_diag=a_diag)


def _vector_decay_chunk(groups, v_all, heads_per_group, st_ref):
    n_groups = len(groups)
    n_heads = n_groups * heads_per_group
    rows_all = n_heads * CHUNK
    zeros_bf = jnp.zeros((CHUNK, LANES), BF16)

    lhs_rows = []
    for h in range(n_heads):
        g, i = divmod(h, heads_per_group)
        lhs_rows.append(jnp.concatenate(
            [groups[g]["qs"][i] if gg == g else zeros_bf for gg in range(n_groups)], axis=1))
    ks_cat = jnp.concatenate([grp["ks"] for grp in groups], axis=1)
    p_all = _dot_nt(jnp.concatenate(lhs_rows, axis=0), ks_cat)
    row_blk = _div_pow2(_iota((rows_all, N_OFF), 0) & (CHUNK - 1), SUB)
    col = _iota((rows_all, N_OFF), 1)
    lo = lax.shift_right_logical(row_blk * (row_blk - 1), 1) * SUB
    p_all = jnp.where((col >= lo) & (col < lo + row_blk * SUB), p_all, 0.0)

    a_all = jnp.concatenate([a for grp in groups for a in grp["a_diag"]], axis=0)
    a_all = jnp.concatenate([a_all, jnp.zeros((rows_all, LANES - CHUNK), F32)], axis=1)
    scores = jnp.concatenate([p_all, a_all], axis=1).astype(BF16)
    width = v_all.shape[1]
    v_rows = jnp.concatenate(
        [v_all[:i * SUB, :] for i in range(1, N_SUB)]
        + [jnp.zeros((N_OFF - N_OFF_USED, width), F32), v_all, jnp.zeros((LANES - CHUNK, width), F32)],
        axis=0).astype(BF16)
    r = _dot(scores, v_rows)
    outs = [r[h * CHUNK:(h + 1) * CHUNK, h * HEAD_W:(h + 1) * HEAD_W] for h in range(n_heads)]

    v_bf = v_all.astype(BF16)
    if heads_per_group == 1:
        st = st_ref[...]
        st_bf = st.reshape(n_heads * HEAD_W, LANES).astype(BF16)
        qe_all = jnp.concatenate([grp["q_exp"][0] for grp in groups], axis=0)
        inter = _dot_nt(qe_all, st_bf)
        ke_cat = jnp.concatenate([grp["k_end"][0] for grp in groups], axis=1)
        upd = _dot_tn(v_bf, ke_cat)
        for h in range(n_heads):
            outs[h] = outs[h] + inter[h * CHUNK:(h + 1) * CHUNK, h * HEAD_W:(h + 1) * HEAD_W]
            st_ref[h] = st[h] * groups[h]["dec"] + upd[h * HEAD_W:(h + 1) * HEAD_W, h * LANES:(h + 1) * LANES]
    else:
        for g, grp in enumerate(groups):
            st = st_ref[g]
            qe_rows = jnp.concatenate(grp["q_exp"], axis=0)
            inter = _dot_nt(qe_rows, st.astype(BF16))
            ke_rows = jnp.concatenate(grp["k_end"], axis=0)
            v_g = [v_bf[:, (g * heads_per_group + i) * HEAD_W:(g * heads_per_group + i + 1) * HEAD_W]
                   for i in range(heads_per_group)]
            upd = _dot_tn(jnp.concatenate(v_g, axis=0), ke_rows)
            for i in range(heads_per_group):
                h = g * heads_per_group + i
                outs[h] = outs[h] + inter[i * CHUNK:(i + 1) * CHUNK, :]
            st_ref[g] = st * grp["dec"] + upd
    return outs


def _modulated_norm(x, mod_ref, normg_ref):
    shift = mod_ref[0:1, 0:D_MODEL]
    scale = mod_ref[0:1, D_MODEL:2 * D_MODEL]
    h = (_rms(x, D_MODEL) * normg_ref[...]) * (1.0 + scale) + shift
    return h.astype(BF16)


def _in_proj(h_ref, wt_ref, proj_ref, dst, row0, row1):
    n_blk = 1024
    for r0 in range(row0, row1, n_blk):
        r1 = min(r0 + n_blk, row1)
        proj_ref[:, dst + r0 - row0:dst + r1 - row0] = _dot_nt(h_ref[...], wt_ref[r0:r1, :])


def _in_proj_vector_part(h_ref, wt_ref, wt_gla_ref, proj_ref):
    _in_proj(h_ref, wt_ref, proj_ref, C_AQ, 0, ORIG_RQ)
    _in_proj(h_ref, wt_gla_ref, proj_ref, C_GQ, 0, ORIG_LR - ORIG_GQ)
    _in_proj(h_ref, wt_gla_ref, proj_ref, C_LR, GLA_TAIL_ROWS - LANES, GLA_TAIL_ROWS)


def _in_proj_scalar_part(h_ref, wt_ref, proj_ref):
    _in_proj(h_ref, wt_ref, proj_ref, C_RQ, ORIG_RQ, ORIG_DT)
    _in_proj(h_ref, wt_ref, proj_ref, C_DT, ORIG_DT, ORIG_DT + LANES)


def _mixer_prep(layer, proj_ref, hk_ref, gcum_ref, lb_logits_ref, w_gk2_ref, b_gk2_ref):
    tri_chunk = _lower_tri(TILE, CHUNK)

    lg = [lb_logits_ref[i:i + 1, :] for i in range(DEPTH)]
    lg_max = functools.reduce(jnp.maximum, lg)
    lg_exp = [jnp.exp(r - lg_max) for r in lg]
    lg_den = functools.reduce(lambda a, b: a + b, lg_exp)
    lower = jnp.zeros((1, BRANCH_W), F32)
    for i in range(1, layer + 1):
        lower = lower + lg_exp[i] / lg_den
    log_lb = jnp.log(lower)
    log1m_lb = jnp.log1p(-lower)
    proj_ref[:, C_AQ:C_AQ + BRANCH_W] = _silu(proj_ref[:, C_AQ:C_AQ + BRANCH_W])
    log_f = _logaddexp(log_lb, log1m_lb + _log_sigmoid(proj_ref[:, C_AF:C_AF + BRANCH_W]))
    hk_ref[...] = 1.0 - jnp.exp(log_f)
    proj_ref[:, C_AF:C_AF + BRANCH_W] = _dot3(tri_chunk, log_f * LOG2_E)

    low_rank = proj_ref[:, C_LR:C_LR + LANES]
    gk_gate = _dot(low_rank.astype(BF16), w_gk2_ref[...].astype(BF16)) + b_gk2_ref[...]
    gcum_ref[...] = _dot3(tri_chunk, _log_sigmoid(gk_gate) * (LOG2_E / GLA_GATE_TEMP))
    gla_dk = GLA_KEY_W // N_HEADS
    proj_ref[:, C_GQ:C_GQ + GLA_KEY_W] = proj_ref[:, C_GQ:C_GQ + GLA_KEY_W] * (gla_dk ** -0.5)


def _vector_decay_mixers(proj_ref, y_ref, hk_ref, gcum_ref, cum_sc, k_sc, st_hgrn, st_gla,
                         hgrn_g_ref, gla_g_ref):
    def chunk_body(ci):
        r0 = ci * CHUNK
        rows = slice(r0, r0 + CHUNK)
        groups = []
        for hd in range(N_HEADS):
            cs = slice(hd * HEAD_W, (hd + 1) * HEAD_W)
            groups.append(_key_group_terms(
                proj_ref[rows, C_AQ + cs.start:C_AQ + cs.stop], hk_ref[rows, cs],
                proj_ref[rows, C_AF + cs.start:C_AF + cs.stop],
                cum_sc.at[ci, hd], k_sc.at[ci, hd], 1))
        outs = []
        for p0 in range(0, N_HEADS, HEADS_PER_MATMUL):
            outs += _vector_decay_chunk(
                groups[p0:p0 + HEADS_PER_MATMUL],
                proj_ref[rows, C_AI + p0 * HEAD_W:C_AI + (p0 + HEADS_PER_MATMUL) * HEAD_W],
                1, st_hgrn.at[pl.ds(p0, HEADS_PER_MATMUL)])
        for hd, o in enumerate(outs):
            cs = slice(hd * HEAD_W, (hd + 1) * HEAD_W)
            g_act = _silu(proj_ref[rows, C_AG + cs.start:C_AG + cs.stop])
            y_ref[rows, cs] = (_rms(o, HEAD_W) * hgrn_g_ref[:, cs] * g_act).astype(BF16)
        groups = []
        for pr in range(N_HEADS // 2):
            ks_ = slice(pr * LANES, (pr + 1) * LANES)
            groups.append(_key_group_terms(
                proj_ref[rows, C_GQ + ks_.start:C_GQ + ks_.stop],
                proj_ref[rows, C_GK + ks_.start:C_GK + ks_.stop],
                gcum_ref[rows, ks_],
                cum_sc.at[ci, N_HEADS + pr], k_sc.at[ci, N_HEADS + pr], 2))
        outs = []
        for pr in range(N_HEADS // 2):
            outs += _vector_decay_chunk(
                groups[pr:pr + 1],
                proj_ref[rows, C_GV + 2 * pr * HEAD_W:C_GV + 2 * (pr + 1) * HEAD_W],
                2, st_gla.at[pl.ds(pr, 1)])
        for hd, o in enumerate(outs):
            cs = slice(hd * HEAD_W, (hd + 1) * HEAD_W)
            g_act = _silu(proj_ref[rows, C_GG + cs.start:C_GG + cs.stop])
            y_ref[rows, 3 * BRANCH_W + cs.start:3 * BRANCH_W + cs.stop] = (
                _rms(o, HEAD_W) * gla_g_ref[:, cs] * g_act).astype(BF16)

    for ci in range(TILE // CHUNK):
        chunk_body(ci)


def _ret_log2_gamma(hd):
    return math.log1p(-(2.0 ** -(RET_DECAY_EXP0 + hd))) * LOG2_E


def _init_retention_decays(ret_decay_ref, ret_edge_ref):
    t_col = _iota((TILE, TILE), 0)
    s_row = _iota((TILE, TILE), 1)
    t_minus_s = (t_col - s_row).astype(F32)
    t_plus1 = (_iota((TILE, HEAD_W), 0) + 1).astype(F32)
    s_to_end = (TILE - 1 - _iota((TILE, HEAD_W), 0)).astype(F32)
    for hd in range(N_HEADS):
        log2_gamma = _ret_log2_gamma(hd)
        ret_decay_ref[hd] = jnp.where(t_col >= s_row, jnp.exp2(t_minus_s * log2_gamma), 0.0)
        ret_edge_ref[hd, 0] = jnp.exp2(t_plus1 * log2_gamma)
        ret_edge_ref[hd, 1] = jnp.exp2(s_to_end * log2_gamma)


def _scalar_decay_mixers(proj_ref, y_ref, conv_ref, st_ret, st_ssd, ret_decay_ref, ret_edge_ref,
                         cos_ref, sin_ref,
                         ret_g_ref, conv_w_ref, conv_b_ref, dt_bias_ref, a_log_ref, d_skip_ref,
                         ssm_g_ref):
    tri_tile = _lower_tri(TILE, TILE)
    def cols(c0, width):
        return proj_ref[:, c0:c0 + width]

    tail = cols(C_DT, LANES)

    cos_t = cos_ref[...]
    sin_t = sin_ref[...]
    for hd in range(N_HEADS):
        cs = slice(hd * HEAD_W, (hd + 1) * HEAD_W)
        log2_gamma = _ret_log2_gamma(hd)
        rq = cols(C_RQ + cs.start, HEAD_W)
        rk = cols(C_RK + cs.start, HEAD_W)
        q = rq * cos_t + pltpu.roll(rq, HEAD_W // 2, 1) * sin_t
        k = (rk * cos_t + pltpu.roll(rk, HEAD_W // 2, 1) * sin_t) * (HEAD_W ** -0.5)
        v_bf = cols(C_RV + cs.start, HEAD_W).astype(BF16)
        q_bf = q.astype(BF16)
        scores = _dot_nt(q_bf, k.astype(BF16)) * ret_decay_ref[hd]
        st = st_ret[hd]
        o = (_dot(scores.astype(BF16), v_bf)
             + ret_edge_ref[hd, 0] * _dot(q_bf, st.astype(BF16)))
        k_end = k * ret_edge_ref[hd, 1]
        st_ret[hd] = st * (2.0 ** (TILE * log2_gamma)) + _dot_tn(k_end.astype(BF16), v_bf)
        g_act = _silu(cols(C_RG + cs.start, HEAD_W))
        y_ref[:, BRANCH_W + cs.start:BRANCH_W + cs.stop] = (
            _rms(o, HEAD_W) * ret_g_ref[:, cs] * g_act).astype(BF16)

    u = cols(C_XBC, SSM_CONV_CH)
    prev = conv_ref[...]
    row8 = _iota((SUBLANES, SSM_CONV_CH), 0)
    xbc = conv_b_ref[...] + conv_w_ref[SSM_CONV - 1:SSM_CONV, :] * u
    for d in range(1, SSM_CONV):
        u_d = pltpu.roll(u, d, 0)
        head = jnp.where(row8 < d, pltpu.roll(prev, d, 0), u_d[0:SUBLANES, :])
        u_d = jnp.concatenate([head, u_d[SUBLANES:, :]], axis=0)
        xbc = xbc + conv_w_ref[SSM_CONV - 1 - d:SSM_CONV - d, :] * u_d
    conv_ref[...] = u[TILE - SUBLANES:TILE, :]
    xbc = _silu(xbc)
    xs = xbc[:, 0:BRANCH_W]
    dt = _softplus(tail + dt_bias_ref[...])
    log_a = dt * (-LOG2_E * jnp.exp(a_log_ref[...]))
    cum = _dot3(tri_tile, log_a)
    cum_t = cum.T

    e_r = _iota((LANES, BRANCH_W), 0)
    e_c = _iota((LANES, BRANCH_W), 1)
    expand = jnp.where(e_r == _div_pow2(e_c, SSM_HEAD_DIM), 1.0, 0.0).astype(BF16)
    dt_e = _dot(dt.astype(BF16), expand)
    cum_e = _dot3_rhs(cum, expand)
    cum_last_e = cum_e[TILE - 1:TILE, :]
    d_e = _dot3_rhs(jnp.broadcast_to(d_skip_ref[...], (SUBLANES, LANES)), expand)[0:1, :]
    xdt = xs * dt_e
    x_end = (xdt * jnp.exp2(cum_last_e - cum_e)).astype(BF16)
    half = TILE // 2
    tri_half = _iota((half, half), 0) >= _iota((half, half), 1)
    hpg = SSM_HEADS // SSM_GROUPS
    gw = hpg * SSM_HEAD_DIM
    o_parts = []
    for g in range(SSM_GROUPS):
        b_bf = xbc[:, BRANCH_W + g * SSM_STATE:BRANCH_W + (g + 1) * SSM_STATE].astype(BF16)
        c_bf = xbc[:, BRANCH_W + (SSM_GROUPS + g) * SSM_STATE:
                   BRANCH_W + (SSM_GROUPS + g + 1) * SSM_STATE].astype(BF16)
        cb = _dot_nt(c_bf, b_bf)
        st = st_ssd[g]
        inter = _dot(c_bf, st.astype(BF16)) * jnp.exp2(cum_e[:, g * gw:(g + 1) * gw])
        for hh in range(hpg):
            hd = g * hpg + hh
            quad = {}
            for qi in range(2):
                for qj in range(qi + 1):
                    rq_, cq_ = slice(qi * half, (qi + 1) * half), slice(qj * half, (qj + 1) * half)
                    rel = cum[rq_, hd:hd + 1] - cum_t[hd:hd + 1, cq_]
                    w_q = cb[rq_, cq_] * jnp.exp2(rel)
                    quad[qi, qj] = (jnp.where(tri_half, w_q, 0.0) if qi == qj else w_q).astype(BF16)
            w = jnp.concatenate([jnp.concatenate([quad[0, 0], jnp.zeros((half, half), BF16)], axis=1),
                                 jnp.concatenate([quad[1, 0], quad[1, 1]], axis=1)], axis=0)
            o_parts.append(_dot(w, xdt[:, hd * SSM_HEAD_DIM:(hd + 1) * SSM_HEAD_DIM].astype(BF16))
                           + inter[:, hh * SSM_HEAD_DIM:(hh + 1) * SSM_HEAD_DIM])
        st_ssd[g] = (st * jnp.exp2(cum_last_e[:, g * gw:(g + 1) * gw])
                     + _dot_tn(b_bf, x_end[:, g * gw:(g + 1) * gw]))
    o_ssd = jnp.concatenate(o_parts, axis=1)
    y_ssd = (o_ssd + d_e * xs) * _silu(cols(C_MZ, BRANCH_W))
    gn = BRANCH_W // SSM_GROUPS
    for g in range(SSM_GROUPS):
        cs = slice(g * gn, (g + 1) * gn)
        y_ref[:, 2 * BRANCH_W + cs.start:2 * BRANCH_W + cs.stop] = (
            _rms(y_ssd[:, cs], gn) * ssm_g_ref[:, cs]).astype(BF16)


def _layer_kernel(layer, final,
                  x_ref, xn_ref, mod_ref, modn_ref, normg_ref, wt_ref, wt_gla_ref, w_out_ref,
                  cos_ref, sin_ref,
                  lb_logits_ref, hgrn_g_ref, ret_g_ref, conv_w_ref, conv_b_ref, dt_bias_ref,
                  a_log_ref, d_skip_ref, ssm_g_ref, w_gk2_ref, b_gk2_ref, gla_g_ref, final_g_ref,
                  o_ref,
                  proj_ref, h_ref, y_ref, conv_ref, hk_ref, gcum_ref, cum_sc, k_sc,
                  st_hgrn, st_ret, st_ssd, st_gla, ret_decay_ref, ret_edge_ref):
    b_idx = pl.program_id(0)
    t_idx = pl.program_id(1)

    def project_vector_part(x_tile, tile_mod_ref):
        h_ref[...] = _modulated_norm(x_tile, tile_mod_ref, normg_ref)
        _in_proj_vector_part(h_ref, wt_ref, wt_gla_ref, proj_ref)
        _mixer_prep(layer, proj_ref, hk_ref, gcum_ref, lb_logits_ref, w_gk2_ref, b_gk2_ref)

    @pl.when(t_idx == 0)
    def _():
        st_hgrn[...] = jnp.zeros_like(st_hgrn)
        st_ret[...] = jnp.zeros_like(st_ret)
        st_ssd[...] = jnp.zeros_like(st_ssd)
        st_gla[...] = jnp.zeros_like(st_gla)
        conv_ref[...] = jnp.zeros_like(conv_ref)

    @pl.when((t_idx == 0) & (b_idx == 0))
    def _():
        project_vector_part(x_ref[...], mod_ref)
        _init_retention_decays(ret_decay_ref, ret_edge_ref)

    _in_proj_scalar_part(h_ref, wt_ref, proj_ref)
    _vector_decay_mixers(proj_ref, y_ref, hk_ref, gcum_ref, cum_sc, k_sc, st_hgrn, st_gla,
                         hgrn_g_ref, gla_g_ref)
    project_vector_part(xn_ref[...], modn_ref)
    _scalar_decay_mixers(proj_ref, y_ref, conv_ref, st_ret, st_ssd, ret_decay_ref, ret_edge_ref,
                         cos_ref, sin_ref,
                         ret_g_ref, conv_w_ref, conv_b_ref, dt_bias_ref, a_log_ref, d_skip_ref,
                         ssm_g_ref)
    gate = mod_ref[0:1, 2 * D_MODEL:3 * D_MODEL]
    x_new = x_ref[...] + gate * _dot(y_ref[...], w_out_ref[...])
    if final:
        x_new = _rms(x_new, D_MODEL) * final_g_ref[...]
    o_ref[...] = x_new


def _resident(shape):
    nd = len(shape)
    return pl.BlockSpec(shape, lambda b, t, _nd=nd: (0,) * _nd, pipeline_mode=pl.Buffered(1))


def _layer_call(layer, final, x, mod, params):
    batch, seq, _ = x.shape
    n_tiles = seq // TILE
    n_chunks = TILE // CHUNK

    def next_tile(t):
        return jnp.where(t == n_tiles - 1, 0, t + 1)

    def next_batch(b, t):
        return jnp.where(t == n_tiles - 1, jnp.minimum(b + 1, batch - 1), b)

    small = [params[k] for k in (
        "lb_logits", "hgrn_g", "ret_g", "conv_w", "conv_b", "dt_bias", "a_log", "d_skip",
        "ssm_g", "w_gk2", "b_gk2", "gla_g", "final_g")]
    in_specs = [
        pl.BlockSpec((None, TILE, D_MODEL), lambda b, t: (b, t, 0)),
        pl.BlockSpec((None, TILE, D_MODEL), lambda b, t: (next_batch(b, t), next_tile(t), 0)),
        pl.BlockSpec((None, SUBLANES, 3 * D_MODEL), lambda b, t: (b, 0, 0)),
        pl.BlockSpec((None, SUBLANES, 3 * D_MODEL), lambda b, t: (next_batch(b, t), 0, 0)),
        _resident((1, D_MODEL)),
        pl.BlockSpec((None, ORIG_W, D_MODEL), lambda b, t: (layer, 0, 0), pipeline_mode=pl.Buffered(1)),
        pl.BlockSpec((None, GLA_TAIL_ROWS, D_MODEL), lambda b, t: (layer, 0, 0),
                     pipeline_mode=pl.Buffered(1)),
        pl.BlockSpec((None, D_INNER, D_MODEL), lambda b, t: (layer, 0, 0), pipeline_mode=pl.Buffered(1)),
        pl.BlockSpec((TILE, HEAD_W), lambda b, t: (t, 0)),
        pl.BlockSpec((TILE, HEAD_W), lambda b, t: (t, 0)),
    ] + [_resident(a.shape) for a in small]
    scratch = [
        pltpu.VMEM((TILE, PROJ_W), F32),
        pltpu.VMEM((TILE, D_MODEL), BF16),
        pltpu.VMEM((TILE, D_INNER), BF16),
        pltpu.VMEM((SUBLANES, SSM_CONV_CH), F32),
        pltpu.VMEM((TILE, BRANCH_W), F32),
        pltpu.VMEM((TILE, GLA_KEY_W), F32),
        pltpu.VMEM((n_chunks, N_KEY_GROUPS, CHUNK, LANES), F32),
        pltpu.VMEM((n_chunks, N_KEY_GROUPS, CHUNK, LANES), F32),
        pltpu.VMEM((N_HEADS, HEAD_W, HEAD_W), F32),
        pltpu.VMEM((N_HEADS, HEAD_W, HEAD_W), F32),
        pltpu.VMEM((SSM_GROUPS, SSM_STATE, BRANCH_W // SSM_GROUPS), F32),
        pltpu.VMEM((N_HEADS // 2, HEAD_W, LANES), F32),
        pltpu.VMEM((N_HEADS, TILE, TILE), F32),
        pltpu.VMEM((N_HEADS, 2, TILE, HEAD_W), F32),
    ]
    return pl.pallas_call(
        functools.partial(_layer_kernel, layer, final),
        grid=(batch, n_tiles),
        in_specs=in_specs,
        out_specs=pl.BlockSpec((None, TILE, D_MODEL), lambda b, t: (b, t, 0)),
        out_shape=jax.ShapeDtypeStruct(x.shape, F32),
        scratch_shapes=scratch,
        compiler_params=pltpu.CompilerParams(
            dimension_semantics=("arbitrary", "arbitrary"),
            vmem_limit_bytes=VMEM_LIMIT_BYTES),
        name=f"mixer_layer{layer}",
    )(x, x, mod, mod, params["norm_g"], params["wt"], params["wt_gla"], params["w_out"], params["cos"], params["sin"], *small)


def _pad_lanes(v, lane0=0):
    return jnp.zeros((1, LANES), F32).at[0, lane0:lane0 + v.shape[0]].set(v)


def kernel(x, c, w_ada, b_ada, norm_g, w_in, hgrn_lb_logits, hgrn_onorm_g, ret_onorm_g, ssm_conv_w,
           ssm_conv_b, ssm_dt_bias, ssm_a_log, ssm_d, ssm_norm_g, gla_w_gk2, gla_b_gk2, gla_onorm_g,
           w_out, final_g):
    batch, seq, _ = x.shape
    c_pad = jnp.zeros((SUBLANES, D_MODEL), F32).at[:batch].set(c)
    mod_all = _ada_mod(c_pad, w_ada, b_ada)

    inv_freq = ROPE_BASE ** (-np.arange(0, HEAD_W, 2, dtype=np.float64) / HEAD_W)
    ang = np.arange(seq, dtype=np.float64)[:, None] * inv_freq[None, :]
    cos_tab = jnp.asarray(np.concatenate([np.cos(ang), np.cos(ang)], axis=-1), F32)
    sin_tab = jnp.asarray(np.concatenate([-np.sin(ang), np.sin(ang)], axis=-1), F32)

    wt = jnp.swapaxes(w_in, 1, 2).astype(BF16)
    wt_gla = wt[:, ORIG_GQ:, :]
    w_out_bf = w_out.astype(BF16)

    for layer in range(DEPTH):
        w_gk2_p = jnp.zeros((LANES, GLA_KEY_W), F32).at[LR_LANE0:].set(gla_w_gk2[layer])
        params = dict(
            norm_g=norm_g[layer][None, :], wt=wt, wt_gla=wt_gla, w_out=w_out_bf,
            cos=cos_tab, sin=sin_tab,
            lb_logits=hgrn_lb_logits, hgrn_g=hgrn_onorm_g[layer][None, :],
            ret_g=ret_onorm_g[layer][None, :], conv_w=ssm_conv_w[layer],
            conv_b=ssm_conv_b[layer][None, :], dt_bias=_pad_lanes(ssm_dt_bias[layer]),
            a_log=_pad_lanes(ssm_a_log[layer]), d_skip=_pad_lanes(ssm_d[layer]),
            ssm_g=ssm_norm_g[layer][None, :], w_gk2=w_gk2_p, b_gk2=gla_b_gk2[layer][None, :],
            gla_g=gla_onorm_g[layer][None, :], final_g=final_g[None, :])
        mod = jnp.broadcast_to(mod_all[layer][:batch, None, :], (batch, SUBLANES, 3 * D_MODEL))
        x = _layer_call(layer, layer == DEPTH - 1, x, mod, params)
    return x
```

```python
import functools
import math

import numpy as np
import jax
import jax.numpy as jnp
from jax import lax
from jax.experimental import pallas as pl
from jax.experimental.pallas import tpu as pltpu

F32 = jnp.float32
BF16 = jnp.bfloat16

D_MODEL = 1024
DEPTH = 2
BRANCH_W = 512
D_INNER = 4 * BRANCH_W
HEAD_W = 128
N_HEADS = 4
RET_DECAY_EXP0 = 5.0
ROPE_BASE = 10000.0
SSM_HEADS = 8
SSM_HEAD_DIM = 64
SSM_GROUPS = 2
SSM_STATE = 128
SSM_CONV = 4
SSM_CONV_CH = BRANCH_W + 2 * SSM_GROUPS * SSM_STATE
GLA_KEY_W = 256
GLA_LOWRANK = 16
GLA_GATE_TEMP = 16.0
EPS = 1e-6
LOG2_E = math.log2(math.e)

LANES = 128
SUBLANES = 8

C_AQ, C_AF, C_AI, C_AG = 0, 512, 1024, 1536
C_GQ, C_GK, C_GV, C_GG = 2048, 2304, 2560, 3072
C_LR = 3584
VEC_W = C_LR + LANES
C_RQ, C_RK, C_RV, C_RG = 3712, 4224, 4736, 5248
C_MZ, C_XBC = 5760, 6272
C_DT = 7296
PROJ_W = C_DT + LANES
ORIG_RQ = 2048
ORIG_DT = 5632
ORIG_GQ = ORIG_DT + SSM_HEADS
ORIG_LR = ORIG_GQ + 2 * GLA_KEY_W + 2 * BRANCH_W
ORIG_W = ORIG_LR + GLA_LOWRANK
GLA_TAIL_ROWS = ORIG_W - ORIG_GQ
LR_LANE0 = LANES - GLA_LOWRANK

TILE = 256
CHUNK = 64
SUB = 8
N_SUB = CHUNK // SUB
N_KEY_GROUPS = N_HEADS + N_HEADS // 2
VMEM_LIMIT_BYTES = 56 * 1024 * 1024


def _dot(a, b):
    return jnp.dot(a, b, preferred_element_type=F32)


def _dot_nt(a, b):
    return lax.dot_general(a, b, (((1,), (1,)), ((), ())), preferred_element_type=F32)


def _dot_tn(a, b):
    return lax.dot_general(a, b, (((0,), (0,)), ((), ())), preferred_element_type=F32)


def _dot3(m_bf16, x):
    hi = x.astype(BF16)
    r1 = x - hi.astype(F32)
    mid = r1.astype(BF16)
    lo = (r1 - mid.astype(F32)).astype(BF16)
    return _dot(m_bf16, hi) + _dot(m_bf16, mid) + _dot(m_bf16, lo)


def _dot3_rhs(x, m_bf16):
    hi = x.astype(BF16)
    r1 = x - hi.astype(F32)
    mid = r1.astype(BF16)
    lo = (r1 - mid.astype(F32)).astype(BF16)
    return _dot(hi, m_bf16) + _dot(mid, m_bf16) + _dot(lo, m_bf16)


def _silu(x):
    hx = 0.5 * x
    return hx + hx * jnp.tanh(hx)


def _log1p_exp_neg_abs(x):
    return jnp.log(1.0 + jnp.exp(-jnp.abs(x)))


def _log_sigmoid(x):
    return jnp.minimum(x, 0.0) - _log1p_exp_neg_abs(x)


def _softplus(x):
    return jnp.maximum(x, 0.0) + _log1p_exp_neg_abs(x)


def _logaddexp(a, b):
    return jnp.maximum(a, b) + _log1p_exp_neg_abs(a - b)


def _rms(x, width):
    return x * lax.rsqrt(jnp.sum(x * x, axis=-1, keepdims=True) * (1.0 / width) + EPS)


def _iota(shape, dim):
    return lax.broadcasted_iota(jnp.int32, shape, dim)


def _div_pow2(x, d):
    shift = d.bit_length() - 1
    assert 1 << shift == d
    return lax.shift_right_logical(x, shift)


def _lower_tri(n, block):
    r = _iota((n, n), 0)
    c = _iota((n, n), 1)
    keep = (c <= r) & (_div_pow2(r, block) == _div_pow2(c, block))
    return jnp.where(keep, 1.0, 0.0).astype(BF16)


def _ada_kernel(c_ref, w_ref, b_ref, o_ref):
    c_act = _silu(c_ref[...]).astype(BF16)
    o_ref[...] = _dot(c_act, w_ref[...].astype(BF16)) + b_ref[...]


def _ada_mod(c_pad, w_ada, b_ada):
    n_blk = 3
    return pl.pallas_call(
        _ada_kernel,
        grid=(DEPTH, n_blk),
        in_specs=[
            pl.BlockSpec((SUBLANES, D_MODEL), lambda l, j: (0, 0)),
            pl.BlockSpec((None, D_MODEL, D_MODEL), lambda l, j: (l, 0, j)),
            pl.BlockSpec((None, 1, D_MODEL), lambda l, j: (l, 0, j)),
        ],
        out_specs=pl.BlockSpec((None, SUBLANES, D_MODEL), lambda l, j: (l, 0, j)),
        out_shape=jax.ShapeDtypeStruct((DEPTH, SUBLANES, 3 * D_MODEL), F32),
        name="ada_mod",
    )(c_pad, w_ada, b_ada.reshape(DEPTH, 1, 3 * D_MODEL))


HEADS_PER_MATMUL = 2
N_OFF_USED = SUB * (N_SUB * (N_SUB - 1) // 2)
N_OFF = -(-N_OFF_USED // LANES) * LANES


def _key_group_terms(q, k, cum, cum_sc, k_sc, heads_per_group):
    lane_w = LANES // heads_per_group

    def head_lanes(x, i):
        if heads_per_group == 1:
            return x
        lane = _iota(x.shape, 1)
        if i == 0:
            keep = lane < lane_w
        elif i == heads_per_group - 1:
            keep = lane >= i * lane_w
        else:
            keep = (lane >= i * lane_w) & (lane < (i + 1) * lane_w)
        return jnp.where(keep, x, 0.0)

    cum_sc[...] = cum
    k_sc[...] = k
    cum_last = cum_sc[CHUNK - 1:CHUNK, :]
    q_exp = q * jnp.exp2(cum)
    k_end = k * jnp.exp2(cum_last - cum)
    dec = jnp.exp2(cum_last)

    qs_parts = [jnp.zeros((SUB, LANES), F32)]
    ks_parts = []
    for i in range(1, N_SUB):
        ref_i = cum_sc[i * SUB - 1:i * SUB, :]
        qs_parts.append(q[i * SUB:(i + 1) * SUB, :] * jnp.exp2(cum[i * SUB:(i + 1) * SUB, :] - ref_i))
        ks_parts.append(k[:i * SUB, :] * jnp.exp2(ref_i - cum[:i * SUB, :]))
    ks_parts.append(jnp.zeros((N_OFF - N_OFF_USED, LANES), F32))
    qs = jnp.concatenate(qs_parts, axis=0)
    ks = jnp.concatenate(ks_parts, axis=0).astype(BF16)

    lane_c = _iota((SUB, CHUNK), 1)
    row_c = _iota((SUB, CHUNK), 0)
    diag_parts = [[] for _ in range(heads_per_group)]
    for i in range(N_SUB):
        acc = [jnp.zeros((SUB, CHUNK), F32) for _ in range(heads_per_group)]
        q_i = q[i * SUB:(i + 1) * SUB, :]
        cum_i = cum[i * SUB:(i + 1) * SUB, :]
        for j in range(SUB):
            s = i * SUB + j
            w = q_i * jnp.exp2(cum_i - cum_sc[s:s + 1, :]) * k_sc[s:s + 1, :]
            for hd in range(heads_per_group):
                a_col = jnp.sum(head_lanes(w, hd), axis=-1, keepdims=True)
                acc[hd] = jnp.where(lane_c == s, a_col, acc[hd])
        causal = (row_c + i * SUB) >= lane_c
        for hd in range(heads_per_group):
            diag_parts[hd].append(jnp.where(causal, acc[hd], 0.0))
    a_diag = [jnp.concatenate(p, axis=0) for p in diag_parts]
    return dict(
        qs=[head_lanes(qs, i).astype(BF16) for i in range(heads_per_group)],
        q_exp=[head_lanes(q_exp, i).astype(BF16) for i in range(heads_per_group)],
        k_end=[head_lanes(k_end, i).astype(BF16) for i in range(heads_per_group)],
        ks=ks, dec=dec, a_diag=a_diag)


def _vector_decay_chunk(groups, v_all, heads_per_group, st_ref):
    n_groups = len(groups)
    n_heads = n_groups * heads_per_group
    rows_all = n_heads * CHUNK
    zeros_bf = jnp.zeros((CHUNK, LANES), BF16)

    lhs_rows = []
    for h in range(n_heads):
        g, i = divmod(h, heads_per_group)
        lhs_rows.append(jnp.concatenate(
            [groups[g]["qs"][i] if gg == g else zeros_bf for gg in range(n_groups)], axis=1))
    ks_cat = jnp.concatenate([grp["ks"] for grp in groups], axis=1)
    p_all = _dot_nt(jnp.concatenate(lhs_rows, axis=0), ks_cat)
    row_blk = _div_pow2(_iota((rows_all, N_OFF), 0) & (CHUNK - 1), SUB)
    col = _iota((rows_all, N_OFF), 1)
    lo = lax.shift_right_logical(row_blk * (row_blk - 1), 1) * SUB
    p_all = jnp.where((col >= lo) & (col < lo + row_blk * SUB), p_all, 0.0)

    a_all = jnp.concatenate([a for grp in groups for a in grp["a_diag"]], axis=0)
    a_all = jnp.concatenate([a_all, jnp.zeros((rows_all, LANES - CHUNK), F32)], axis=1)
    scores = jnp.concatenate([p_all, a_all], axis=1).astype(BF16)
    width = v_all.shape[1]
    v_rows = jnp.concatenate(
        [v_all[:i * SUB, :] for i in range(1, N_SUB)]
        + [jnp.zeros((N_OFF - N_OFF_USED, width), F32), v_all, jnp.zeros((LANES - CHUNK, width), F32)],
        axis=0).astype(BF16)
    r = _dot(scores, v_rows)
    outs = [r[h * CHUNK:(h + 1) * CHUNK, h * HEAD_W:(h + 1) * HEAD_W] for h in range(n_heads)]

    v_bf = v_all.astype(BF16)
    if heads_per_group == 1:
        st = st_ref[...]
        st_bf = st.reshape(n_heads * HEAD_W, LANES).astype(BF16)
        qe_all = jnp.concatenate([grp["q_exp"][0] for grp in groups], axis=0)
        inter = _dot_nt(qe_all, st_bf)
        ke_cat = jnp.concatenate([grp["k_end"][0] for grp in groups], axis=1)
        upd = _dot_tn(v_bf, ke_cat)
        for h in range(n_heads):
            outs[h] = outs[h] + inter[h * CHUNK:(h + 1) * CHUNK, h * HEAD_W:(h + 1) * HEAD_W]
            st_ref[h] = st[h] * groups[h]["dec"] + upd[h * HEAD_W:(h + 1) * HEAD_W, h * LANES:(h + 1) * LANES]
    else:
        for g, grp in enumerate(groups):
            st = st_ref[g]
            qe_rows = jnp.concatenate(grp["q_exp"], axis=0)
            inter = _dot_nt(qe_rows, st.astype(BF16))
            ke_rows = jnp.concatenate(grp["k_end"], axis=0)
            v_g = [v_bf[:, (g * heads_per_group + i) * HEAD_W:(g * heads_per_group + i + 1) * HEAD_W]
                   for i in range(heads_per_group)]
            upd = _dot_tn(jnp.concatenate(v_g, axis=0), ke_rows)
            for i in range(heads_per_group):
                h = g * heads_per_group + i
                outs[h] = outs[h] + inter[i * CHUNK:(i + 1) * CHUNK, :]
            st_ref[g] = st * grp["dec"] + upd
    return outs


def _modulated_norm(x, mod_ref, normg_ref):
    shift = mod_ref[0:1, 0:D_MODEL]
    scale = mod_ref[0:1, D_MODEL:2 * D_MODEL]
    h = (_rms(x, D_MODEL) * normg_ref[...]) * (1.0 + scale) + shift
    return h.astype(BF16)


def _in_proj(h_ref, wt_ref, proj_ref, dst, row0, row1):
    n_blk = 1024
    for r0 in range(row0, row1, n_blk):
        r1 = min(r0 + n_blk, row1)
        proj_ref[:, dst + r0 - row0:dst + r1 - row0] = _dot_nt(h_ref[...], wt_ref[r0:r1, :])


def _in_proj_vector_part(h_ref, wt_ref, wt_gla_ref, proj_ref):
    _in_proj(h_ref, wt_ref, proj_ref, C_AQ, 0, ORIG_RQ)
    _in_proj(h_ref, wt_gla_ref, proj_ref, C_GQ, 0, ORIG_LR - ORIG_GQ)
    _in_proj(h_ref, wt_gla_ref, proj_ref, C_LR, GLA_TAIL_ROWS - LANES, GLA_TAIL_ROWS)


def _in_proj_scalar_part(h_ref, wt_ref, proj_ref):
    _in_proj(h_ref, wt_ref, proj_ref, C_RQ, ORIG_RQ, ORIG_DT)
    _in_proj(h_ref, wt_ref, proj_ref, C_DT, ORIG_DT, ORIG_DT + LANES)


def _mixer_prep(layer, proj_ref, hk_ref, gcum_ref, tri_ref, lb_logits_ref, w_gk2_ref, b_gk2_ref):
    tri_chunk = tri_ref[0]

    lg = [lb_logits_ref[i:i + 1, :] for i in range(DEPTH)]
    lg_max = functools.reduce(jnp.maximum, lg)
    lg_exp = [jnp.exp(r - lg_max) for r in lg]
    lg_den = functools.reduce(lambda a, b: a + b, lg_exp)
    lower = jnp.zeros((1, BRANCH_W), F32)
    for i in range(1, layer + 1):
        lower = lower + lg_exp[i] / lg_den
    log_lb = jnp.log(lower)
    log1m_lb = jnp.log1p(-lower)
    proj_ref[:, C_AQ:C_AQ + BRANCH_W] = _silu(proj_ref[:, C_AQ:C_AQ + BRANCH_W])
    log_f = _logaddexp(log_lb, log1m_lb + _log_sigmoid(proj_ref[:, C_AF:C_AF + BRANCH_W]))
    hk_ref[...] = 1.0 - jnp.exp(log_f)
    proj_ref[:, C_AF:C_AF + BRANCH_W] = _dot3(tri_chunk, log_f * LOG2_E)

    low_rank = proj_ref[:, C_LR:C_LR + LANES]
    gk_gate = _dot(low_rank.astype(BF16), w_gk2_ref[...].astype(BF16)) + b_gk2_ref[...]
    gcum_ref[...] = _dot3(tri_chunk, _log_sigmoid(gk_gate) * (LOG2_E / GLA_GATE_TEMP))
    gla_dk = GLA_KEY_W // N_HEADS
    proj_ref[:, C_GQ:C_GQ + GLA_KEY_W] = proj_ref[:, C_GQ:C_GQ + GLA_KEY_W] * (gla_dk ** -0.5)


def _vector_decay_mixers(proj_ref, y_ref, hk_ref, gcum_ref, cum_sc, k_sc, st_hgrn, st_gla,
                         hgrn_g_ref, gla_g_ref):
    def chunk_body(ci):
        r0 = ci * CHUNK
        rows = slice(r0, r0 + CHUNK)
        groups = []
        for hd in range(N_HEADS):
            cs = slice(hd * HEAD_W, (hd + 1) * HEAD_W)
            groups.append(_key_group_terms(
                proj_ref[rows, C_AQ + cs.start:C_AQ + cs.stop], hk_ref[rows, cs],
                proj_ref[rows, C_AF + cs.start:C_AF + cs.stop],
                cum_sc.at[ci, hd], k_sc.at[ci, hd], 1))
        outs = []
        for p0 in range(0, N_HEADS, HEADS_PER_MATMUL):
            outs += _vector_decay_chunk(
                groups[p0:p0 + HEADS_PER_MATMUL],
                proj_ref[rows, C_AI + p0 * HEAD_W:C_AI + (p0 + HEADS_PER_MATMUL) * HEAD_W],
                1, st_hgrn.at[pl.ds(p0, HEADS_PER_MATMUL)])
        for hd, o in enumerate(outs):
            cs = slice(hd * HEAD_W, (hd + 1) * HEAD_W)
            g_act = _silu(proj_ref[rows, C_AG + cs.start:C_AG + cs.stop])
            y_ref[rows, cs] = (_rms(o, HEAD_W) * hgrn_g_ref[:, cs] * g_act).astype(BF16)
        groups = []
        for pr in range(N_HEADS // 2):
            ks_ = slice(pr * LANES, (pr + 1) * LANES)
            groups.append(_key_group_terms(
                proj_ref[rows, C_GQ + ks_.start:C_GQ + ks_.stop],
                proj_ref[rows, C_GK + ks_.start:C_GK + ks_.stop],
                gcum_ref[rows, ks_],
                cum_sc.at[ci, N_HEADS + pr], k_sc.at[ci, N_HEADS + pr], 2))
        outs = []
        for pr in range(N_HEADS // 2):
            outs += _vector_decay_chunk(
                groups[pr:pr + 1],
                proj_ref[rows, C_GV + 2 * pr * HEAD_W:C_GV + 2 * (pr + 1) * HEAD_W],
                2, st_gla.at[pl.ds(pr, 1)])
        for hd, o in enumerate(outs):
            cs = slice(hd * HEAD_W, (hd + 1) * HEAD_W)
            g_act = _silu(proj_ref[rows, C_GG + cs.start:C_GG + cs.stop])
            y_ref[rows, 3 * BRANCH_W + cs.start:3 * BRANCH_W + cs.stop] = (
                _rms(o, HEAD_W) * gla_g_ref[:, cs] * g_act).astype(BF16)

    for ci in range(TILE // CHUNK):
        chunk_body(ci)


def _ret_log2_gamma(hd):
    return math.log1p(-(2.0 ** -(RET_DECAY_EXP0 + hd))) * LOG2_E


def _init_retention_decays(ret_decay_ref, ret_edge_ref):
    t_col = _iota((TILE, TILE), 0)
    s_row = _iota((TILE, TILE), 1)
    t_minus_s = (t_col - s_row).astype(F32)
    t_plus1 = (_iota((TILE, HEAD_W), 0) + 1).astype(F32)
    s_to_end = (TILE - 1 - _iota((TILE, HEAD_W), 0)).astype(F32)
    for hd in range(N_HEADS):
        log2_gamma = _ret_log2_gamma(hd)
        ret_decay_ref[hd] = jnp.where(t_col >= s_row, jnp.exp2(t_minus_s * log2_gamma), 0.0)
        ret_edge_ref[hd, 0] = jnp.exp2(t_plus1 * log2_gamma)
        ret_edge_ref[hd, 1] = jnp.exp2(s_to_end * log2_gamma)


def _init_matmul_constants(tri_ref, expand_ref):
    tri_ref[0] = _lower_tri(TILE, CHUNK)
    tri_ref[1] = _lower_tri(TILE, TILE)
    e_r = _iota((LANES, BRANCH_W), 0)
    e_c = _iota((LANES, BRANCH_W), 1)
    expand_ref[...] = jnp.where(e_r == _div_pow2(e_c, SSM_HEAD_DIM), 1.0, 0.0).astype(BF16)


def _scalar_decay_mixers(proj_ref, y_ref, conv_ref, st_ret, st_ssd, ret_decay_ref, ret_edge_ref,
                         tri_ref, expand_ref, cos_ref, sin_ref,
                         ret_g_ref, conv_w_ref, conv_b_ref, dt_bias_ref, a_log_ref, d_skip_ref,
                         ssm_g_ref):
    tri_tile = tri_ref[1]
    def cols(c0, width):
        return proj_ref[:, c0:c0 + width]

    tail = cols(C_DT, LANES)

    cos_t = cos_ref[...]
    sin_t = sin_ref[...]
    for hd in range(N_HEADS):
        cs = slice(hd * HEAD_W, (hd + 1) * HEAD_W)
        log2_gamma = _ret_log2_gamma(hd)
        rq = cols(C_RQ + cs.start, HEAD_W)
        rk = cols(C_RK + cs.start, HEAD_W)
        q = rq * cos_t + pltpu.roll(rq, HEAD_W // 2, 1) * sin_t
        k = (rk * cos_t + pltpu.roll(rk, HEAD_W // 2, 1) * sin_t) * (HEAD_W ** -0.5)
        v_bf = cols(C_RV + cs.start, HEAD_W).astype(BF16)
        q_bf = q.astype(BF16)
        scores = _dot_nt(q_bf, k.astype(BF16)) * ret_decay_ref[hd]
        st = st_ret[hd]
        o = (_dot(scores.astype(BF16), v_bf)
             + ret_edge_ref[hd, 0] * _dot(q_bf, st.astype(BF16)))
        k_end = k * ret_edge_ref[hd, 1]
        st_ret[hd] = st * (2.0 ** (TILE * log2_gamma)) + _dot_tn(k_end.astype(BF16), v_bf)
        g_act = _silu(cols(C_RG + cs.start, HEAD_W))
        y_ref[:, BRANCH_W + cs.start:BRANCH_W + cs.stop] = (
            _rms(o, HEAD_W) * ret_g_ref[:, cs] * g_act).astype(BF16)

    u = cols(C_XBC, SSM_CONV_CH)
    prev = conv_ref[...]
    row8 = _iota((SUBLANES, SSM_CONV_CH), 0)
    xbc = conv_b_ref[...] + conv_w_ref[SSM_CONV - 1:SSM_CONV, :] * u
    for d in range(1, SSM_CONV):
        u_d = pltpu.roll(u, d, 0)
        head = jnp.where(row8 < d, pltpu.roll(prev, d, 0), u_d[0:SUBLANES, :])
        u_d = jnp.concatenate([head, u_d[SUBLANES:, :]], axis=0)
        xbc = xbc + conv_w_ref[SSM_CONV - 1 - d:SSM_CONV - d, :] * u_d
    conv_ref[...] = u[TILE - SUBLANES:TILE, :]
    xbc = _silu(xbc)
    xs = xbc[:, 0:BRANCH_W]
    dt = _softplus(tail + dt_bias_ref[...])
    log_a = dt * (-LOG2_E * jnp.exp(a_log_ref[...]))
    cum = _dot3(tri_tile, log_a)
    cum_t = cum.T

    expand = expand_ref[...]
    dt_e =_dot(dt.astype(BF16), expand)
    cum_e = _dot3_rhs(cum, expand)
    cum_last_e = cum_e[TILE - 1:TILE, :]
    d_e = _dot3_rhs(jnp.broadcast_to(d_skip_ref[...], (SUBLANES, LANES)), expand)[0:1, :]
    xdt = xs * dt_e
    x_end = (xdt * jnp.exp2(cum_last_e - cum_e)).astype(BF16)
    half = TILE // 2
    tri_half = _iota((half, half), 0) >= _iota((half, half), 1)
    hpg = SSM_HEADS // SSM_GROUPS
    gw = hpg * SSM_HEAD_DIM
    o_parts = []
    for g in range(SSM_GROUPS):
        b_bf = xbc[:, BRANCH_W + g * SSM_STATE:BRANCH_W + (g + 1) * SSM_STATE].astype(BF16)
        c_bf = xbc[:, BRANCH_W + (SSM_GROUPS + g) * SSM_STATE:
                   BRANCH_W + (SSM_GROUPS + g + 1) * SSM_STATE].astype(BF16)
        cb = _dot_nt(c_bf, b_bf)
        st = st_ssd[g]
        inter = _dot(c_bf, st.astype(BF16)) * jnp.exp2(cum_e[:, g * gw:(g + 1) * gw])
        for hh in range(hpg):
            hd = g * hpg + hh
            quad = {}
            for qi in range(2):
                for qj in range(qi + 1):
                    rq_, cq_ = slice(qi * half, (qi + 1) * half), slice(qj * half, (qj + 1) * half)
                    rel = cum[rq_, hd:hd + 1] - cum_t[hd:hd + 1, cq_]
                    w_q = cb[rq_, cq_] * jnp.exp2(rel)
                    quad[qi, qj] = (jnp.where(tri_half, w_q, 0.0) if qi == qj else w_q).astype(BF16)
            w = jnp.concatenate([jnp.concatenate([quad[0, 0], jnp.zeros((half, half), BF16)], axis=1),
                                 jnp.concatenate([quad[1, 0], quad[1, 1]], axis=1)], axis=0)
            o_parts.append(_dot(w, xdt[:, hd * SSM_HEAD_DIM:(hd + 1) * SSM_HEAD_DIM].astype(BF16))
                           + inter[:, hh * SSM_HEAD_DIM:(hh + 1) * SSM_HEAD_DIM])
        st_ssd[g] = (st * jnp.exp2(cum_last_e[:, g * gw:(g + 1) * gw])
                     + _dot_tn(b_bf, x_end[:, g * gw:(g + 1) * gw]))
    o_ssd = jnp.concatenate(o_parts, axis=1)
    y_ssd = (o_ssd + d_e * xs) * _silu(cols(C_MZ, BRANCH_W))
    gn = BRANCH_W // SSM_GROUPS
    for g in range(SSM_GROUPS):
        cs = slice(g * gn, (g + 1) * gn)
        y_ref[:, 2 * BRANCH_W + cs.start:2 * BRANCH_W + cs.stop] = (
            _rms(y_ssd[:, cs], gn) * ssm_g_ref[:, cs]).astype(BF16)


def _layer_kernel(layer, final,
                  x_ref, xn_ref, mod_ref, modn_ref, normg_ref, wt_ref, wt_gla_ref, w_out_ref,
                  cos_ref, sin_ref,
                  lb_logits_ref, hgrn_g_ref, ret_g_ref, conv_w_ref, conv_b_ref, dt_bias_ref,
                  a_log_ref, d_skip_ref, ssm_g_ref, w_gk2_ref, b_gk2_ref, gla_g_ref, final_g_ref,
                  o_ref,
                  proj_ref, h_ref, y_ref, conv_ref, hk_ref, gcum_ref, cum_sc, k_sc,
                  st_hgrn, st_ret, st_ssd, st_gla, ret_decay_ref, ret_edge_ref, tri_ref, expand_ref):
    b_idx = pl.program_id(0)
    t_idx = pl.program_id(1)

    def project_vector_part(x_tile, tile_mod_ref):
        h_ref[...] = _modulated_norm(x_tile, tile_mod_ref, normg_ref)
        _in_proj_vector_part(h_ref, wt_ref, wt_gla_ref, proj_ref)
        _mixer_prep(layer, proj_ref, hk_ref, gcum_ref, tri_ref, lb_logits_ref, w_gk2_ref, b_gk2_ref)

    @pl.when(t_idx == 0)
    def _():
        st_hgrn[...] = jnp.zeros_like(st_hgrn)
        st_ret[...] = jnp.zeros_like(st_ret)
        st_ssd[...] = jnp.zeros_like(st_ssd)
        st_gla[...] = jnp.zeros_like(st_gla)
        conv_ref[...] = jnp.zeros_like(conv_ref)

    @pl.when((t_idx == 0) & (b_idx == 0))
    def _():
        _init_matmul_constants(tri_ref, expand_ref)
        _init_retention_decays(ret_decay_ref, ret_edge_ref)
        project_vector_part(x_ref[...], mod_ref)

    _in_proj_scalar_part(h_ref, wt_ref, proj_ref)
    _vector_decay_mixers(proj_ref, y_ref, hk_ref, gcum_ref, cum_sc, k_sc, st_hgrn, st_gla,
                         hgrn_g_ref, gla_g_ref)
    project_vector_part(xn_ref[...], modn_ref)
    _scalar_decay_mixers(proj_ref, y_ref, conv_ref, st_ret, st_ssd, ret_decay_ref, ret_edge_ref,
                         tri_ref, expand_ref, cos_ref, sin_ref,
                         ret_g_ref, conv_w_ref, conv_b_ref, dt_bias_ref, a_log_ref, d_skip_ref,
                         ssm_g_ref)
    gate = mod_ref[0:1, 2 * D_MODEL:3 * D_MODEL]
    x_new = x_ref[...] + gate * _dot(y_ref[...], w_out_ref[...])
    if final:
        x_new = _rms(x_new, D_MODEL) * final_g_ref[...]
    o_ref[...] = x_new


def _resident(shape):
    nd = len(shape)
    return pl.BlockSpec(shape, lambda b, t, _nd=nd: (0,) * _nd, pipeline_mode=pl.Buffered(1))


def _layer_call(layer, final, x, mod, params):
    batch, seq, _ = x.shape
    n_tiles = seq // TILE
    n_chunks = TILE // CHUNK

    def next_tile(t):
        return jnp.where(t == n_tiles - 1, 0, t + 1)

    def next_batch(b, t):
        return jnp.where(t == n_tiles - 1, jnp.minimum(b + 1, batch - 1), b)

    small = [params[k] for k in (
        "lb_logits", "hgrn_g", "ret_g", "conv_w", "conv_b", "dt_bias", "a_log", "d_skip",
        "ssm_g", "w_gk2", "b_gk2", "gla_g", "final_g")]
    in_specs = [
        pl.BlockSpec((None, TILE, D_MODEL), lambda b, t: (b, t, 0)),
        pl.BlockSpec((None, TILE, D_MODEL), lambda b, t: (next_batch(b, t), next_tile(t), 0)),
        pl.BlockSpec((None, SUBLANES, 3 * D_MODEL), lambda b, t: (b, 0, 0)),
        pl.BlockSpec((None, SUBLANES, 3 * D_MODEL), lambda b, t: (next_batch(b, t), 0, 0)),
        _resident((1, D_MODEL)),
        pl.BlockSpec((None, ORIG_W, D_MODEL), lambda b, t: (layer, 0, 0), pipeline_mode=pl.Buffered(1)),
        pl.BlockSpec((None, GLA_TAIL_ROWS, D_MODEL), lambda b, t: (layer, 0, 0),
                     pipeline_mode=pl.Buffered(1)),
        pl.BlockSpec((None, D_INNER, D_MODEL), lambda b, t: (layer, 0, 0), pipeline_mode=pl.Buffered(1)),
        pl.BlockSpec((TILE, HEAD_W), lambda b, t: (t, 0)),
        pl.BlockSpec((TILE, HEAD_W), lambda b, t: (t, 0)),
    ] + [_resident(a.shape) for a in small]
    scratch = [
        pltpu.VMEM((TILE, PROJ_W), F32),
        pltpu.VMEM((TILE, D_MODEL), BF16),
        pltpu.VMEM((TILE, D_INNER), BF16),
        pltpu.VMEM((SUBLANES, SSM_CONV_CH), F32),
        pltpu.VMEM((TILE, BRANCH_W), F32),
        pltpu.VMEM((TILE, GLA_KEY_W), F32),
        pltpu.VMEM((n_chunks, N_KEY_GROUPS, CHUNK, LANES), F32),
        pltpu.VMEM((n_chunks, N_KEY_GROUPS, CHUNK, LANES), F32),
        pltpu.VMEM((N_HEADS, HEAD_W, HEAD_W), F32),
        pltpu.VMEM((N_HEADS, HEAD_W, HEAD_W), F32),
        pltpu.VMEM((SSM_GROUPS, SSM_STATE, BRANCH_W // SSM_GROUPS), F32),
        pltpu.VMEM((N_HEADS // 2, HEAD_W, LANES), F32),
        pltpu.VMEM((N_HEADS, TILE, TILE), F32),
        pltpu.VMEM((N_HEADS, 2, TILE, HEAD_W), F32),
        pltpu.VMEM((2, TILE, TILE), BF16),
        pltpu.VMEM((LANES, BRANCH_W), BF16),
    ]
    return pl.pallas_call(
        functools.partial(_layer_kernel, layer, final),
        grid=(batch, n_tiles),
        in_specs=in_specs,
        out_specs=pl.BlockSpec((None, TILE, D_MODEL), lambda b, t: (b, t, 0)),
        out_shape=jax.ShapeDtypeStruct(x.shape, F32),
        scratch_shapes=scratch,
        compiler_params=pltpu.CompilerParams(
            dimension_semantics=("arbitrary", "arbitrary"),
            vmem_limit_bytes=VMEM_LIMIT_BYTES),
        name=f"mixer_layer{layer}",
    )(x, x, mod, mod, params["norm_g"], params["wt"], params["wt_gla"], params["w_out"], params["cos"], params["sin"], *small)


def _pad_lanes(v, lane0=0):
    return jnp.zeros((1, LANES), F32).at[0, lane0:lane0 + v.shape[0]].set(v)


def kernel(x, c, w_ada, b_ada, norm_g, w_in, hgrn_lb_logits, hgrn_onorm_g, ret_onorm_g, ssm_conv_w,
           ssm_conv_b, ssm_dt_bias, ssm_a_log, ssm_d, ssm_norm_g, gla_w_gk2, gla_b_gk2, gla_onorm_g,
           w_out, final_g):
    batch, seq, _ = x.shape
    c_pad = jnp.zeros((SUBLANES, D_MODEL), F32).at[:batch].set(c)
    mod_all = _ada_mod(c_pad, w_ada, b_ada)

    inv_freq = ROPE_BASE ** (-np.arange(0, HEAD_W, 2, dtype=np.float64) / HEAD_W)
    ang = np.arange(seq, dtype=np.float64)[:, None] * inv_freq[None, :]
    cos_tab = jnp.asarray(np.concatenate([np.cos(ang), np.cos(ang)], axis=-1), F32)
    sin_tab = jnp.asarray(np.concatenate([-np.sin(ang), np.sin(ang)], axis=-1), F32)

    wt = jnp.swapaxes(w_in, 1, 2).astype(BF16)
    wt_gla = wt[:, ORIG_GQ:, :]
    w_out_bf = w_out.astype(BF16)

    for layer in range(DEPTH):
        w_gk2_p = jnp.zeros((LANES, GLA_KEY_W), F32).at[LR_LANE0:].set(gla_w_gk2[layer])
        params = dict(
            norm_g=norm_g[layer][None, :], wt=wt, wt_gla=wt_gla, w_out=w_out_bf,
            cos=cos_tab, sin=sin_tab,
            lb_logits=hgrn_lb_logits, hgrn_g=hgrn_onorm_g[layer][None, :],
            ret_g=ret_onorm_g[layer][None, :], conv_w=ssm_conv_w[layer],
            conv_b=ssm_conv_b[layer][None, :], dt_bias=_pad_lanes(ssm_dt_bias[layer]),
            a_log=_pad_lanes(ssm_a_log[layer]), d_skip=_pad_lanes(ssm_d[layer]),
            ssm_g=ssm_norm_g[layer][None, :], w_gk2=w_gk2_p, b_gk2=gla_b_gk2[layer][None, :],
            gla_g=gla_onorm_g[layer][None, :], final_g=final_g[None, :])
        mod = jnp.broadcast_to(mod_all[layer][:batch, None, :], (batch, SUBLANES, 3 * D_MODEL))
        x = _layer_call(layer, layer == DEPTH - 1, x, mod, params)
    return x
```

```python
import functools
import math

import numpy as np
import jax
import jax.numpy as jnp
from jax import lax
from jax.experimental import pallas as pl
from jax.experimental.pallas import tpu as pltpu

F32 = jnp.float32
BF16 = jnp.bfloat16

D_MODEL = 1024
DEPTH = 2
BRANCH_W = 512
D_INNER = 4 * BRANCH_W
HEAD_W = 128
N_HEADS = 4
RET_DECAY_EXP0 = 5.0
ROPE_BASE = 10000.0
SSM_HEADS = 8
SSM_HEAD_DIM = 64
SSM_GROUPS = 2
SSM_STATE = 128
SSM_CONV = 4
SSM_CONV_CH = BRANCH_W + 2 * SSM_GROUPS * SSM_STATE
GLA_KEY_W = 256
GLA_LOWRANK = 16
GLA_GATE_TEMP = 16.0
EPS = 1e-6
LOG2_E = math.log2(math.e)

LANES = 128
SUBLANES = 8

C_AQ, C_AF, C_AI, C_AG = 0, 512, 1024, 1536
C_GQ, C_GK, C_GV, C_GG = 2048, 2304, 2560, 3072
C_LR = 3584
VEC_W = C_LR + LANES
C_RQ, C_RK, C_RV, C_RG = 3712, 4224, 4736, 5248
C_MZ, C_XBC = 5760, 6272
C_DT = 7296
PROJ_W = C_DT + LANES
ORIG_RQ = 2048
ORIG_DT = 5632
ORIG_GQ = ORIG_DT + SSM_HEADS
ORIG_LR = ORIG_GQ + 2 * GLA_KEY_W + 2 * BRANCH_W
ORIG_W = ORIG_LR + GLA_LOWRANK
GLA_TAIL_ROWS = ORIG_W - ORIG_GQ
LR_LANE0 = LANES - GLA_LOWRANK

TILE = 256
CHUNK = 64
SUB = 8
N_SUB = CHUNK // SUB
N_KEY_GROUPS = N_HEADS + N_HEADS // 2
VMEM_LIMIT_BYTES = 56 * 1024 * 1024


def _dot(a, b):
    return jnp.dot(a, b, preferred_element_type=F32)


def _dot_nt(a, b):
    return lax.dot_general(a, b, (((1,), (1,)), ((), ())), preferred_element_type=F32)


def _dot_tn(a, b):
    return lax.dot_general(a, b, (((0,), (0,)), ((), ())), preferred_element_type=F32)


def _dot3(m_bf16, x):
    hi = x.astype(BF16)
    r1 = x - hi.astype(F32)
    mid = r1.astype(BF16)
    lo = (r1 - mid.astype(F32)).astype(BF16)
    return _dot(m_bf16, hi) + _dot(m_bf16, mid) + _dot(m_bf16, lo)


def _dot3_rhs(x, m_bf16):
    hi = x.astype(BF16)
    r1 = x - hi.astype(F32)
    mid = r1.astype(BF16)
    lo = (r1 - mid.astype(F32)).astype(BF16)
    return _dot(hi, m_bf16) + _dot(mid, m_bf16) + _dot(lo, m_bf16)


def _silu(x):
    hx = 0.5 * x
    return hx + hx * jnp.tanh(hx)


def _log1p_exp_neg_abs(x):
    return jnp.log(1.0 + jnp.exp(-jnp.abs(x)))


def _log_sigmoid(x):
    return jnp.minimum(x, 0.0) - _log1p_exp_neg_abs(x)


def _softplus(x):
    return jnp.maximum(x, 0.0) + _log1p_exp_neg_abs(x)


def _logaddexp(a, b):
    return jnp.maximum(a, b) + _log1p_exp_neg_abs(a - b)


def _rms(x, width):
    return x * lax.rsqrt(jnp.sum(x * x, axis=-1, keepdims=True) * (1.0 / width) + EPS)


def _iota(shape, dim):
    return lax.broadcasted_iota(jnp.int32, shape, dim)


def _div_pow2(x, d):
    shift = d.bit_length() - 1
    assert 1 << shift == d
    return lax.shift_right_logical(x, shift)


def _lower_tri(n, block):
    r = _iota((n, n), 0)
    c = _iota((n, n), 1)
    keep = (c <= r) & (_div_pow2(r, block) == _div_pow2(c, block))
    return jnp.where(keep, 1.0, 0.0).astype(BF16)


def _ada_kernel(c_ref, w_ref, b_ref, o_ref):
    c_act = _silu(c_ref[...]).astype(BF16)
    o_ref[...] = _dot(c_act, w_ref[...].astype(BF16)) + b_ref[...]


def _ada_mod(c_pad, w_ada, b_ada):
    n_blk = 3
    return pl.pallas_call(
        _ada_kernel,
        grid=(DEPTH, n_blk),
        in_specs=[
            pl.BlockSpec((SUBLANES, D_MODEL), lambda l, j: (0, 0)),
            pl.BlockSpec((None, D_MODEL, D_MODEL), lambda l, j: (l, 0, j)),
            pl.BlockSpec((None, 1, D_MODEL), lambda l, j: (l, 0, j)),
        ],
        out_specs=pl.BlockSpec((None, SUBLANES, D_MODEL), lambda l, j: (l, 0, j)),
        out_shape=jax.ShapeDtypeStruct((DEPTH, SUBLANES, 3 * D_MODEL), F32),
        name="ada_mod",
    )(c_pad, w_ada, b_ada.reshape(DEPTH, 1, 3 * D_MODEL))


HEADS_PER_MATMUL = 2
N_OFF_USED = SUB * (N_SUB * (N_SUB - 1) // 2)
N_OFF = -(-N_OFF_USED // LANES) * LANES


def _key_group_terms(q, k, cum, cum_sc, k_sc, heads_per_group):
    lane_w = LANES // heads_per_group

    def head_lanes(x, i):
        if heads_per_group == 1:
            return x
        lane = _iota(x.shape, 1)
        return jnp.where((lane >= i * lane_w) & (lane < (i + 1) * lane_w), x, 0.0)

    cum_sc[...] = cum
    k_sc[...] = k
    cum_last = cum_sc[CHUNK - 1:CHUNK, :]
    q_exp = q * jnp.exp2(cum)
    k_end = k * jnp.exp2(cum_last - cum)
    dec = jnp.exp2(cum_last)

    qs_parts = [jnp.zeros((SUB, LANES), F32)]
    ks_parts = []
    for i in range(1, N_SUB):
        ref_i = cum_sc[i * SUB - 1:i * SUB, :]
        qs_parts.append(q[i * SUB:(i + 1) * SUB, :] * jnp.exp2(cum[i * SUB:(i + 1) * SUB, :] - ref_i))
        ks_parts.append(k[:i * SUB, :] * jnp.exp2(ref_i - cum[:i * SUB, :]))
    ks_parts.append(jnp.zeros((N_OFF - N_OFF_USED, LANES), F32))
    qs = jnp.concatenate(qs_parts, axis=0)
    ks = jnp.concatenate(ks_parts, axis=0).astype(BF16)

    lane_c = _iota((SUB, CHUNK), 1)
    row_c = _iota((SUB, CHUNK), 0)
    diag_parts = [[] for _ in range(heads_per_group)]
    for i in range(N_SUB):
        acc = [jnp.zeros((SUB, CHUNK), F32) for _ in range(heads_per_group)]
        q_i = q[i * SUB:(i + 1) * SUB, :]
        cum_i = cum[i * SUB:(i + 1) * SUB, :]
        for j in range(SUB):
            s = i * SUB + j
            w = q_i * jnp.exp2(cum_i - cum_sc[s:s + 1, :]) * k_sc[s:s + 1, :]
            for hd in range(heads_per_group):
                a_col = jnp.sum(head_lanes(w, hd), axis=-1, keepdims=True)
                acc[hd] = jnp.where(lane_c == s, a_col, acc[hd])
        causal = (row_c + i * SUB) >= lane_c
        for hd in range(heads_per_group):
            diag_parts[hd].append(jnp.where(causal, acc[hd], 0.0))
    a_diag = [jnp.concatenate(p, axis=0) for p in diag_parts]
    return dict(
        qs=[head_lanes(qs, i).astype(BF16) for i in range(heads_per_group)],
        q_exp=[head_lanes(q_exp, i).astype(BF16) for i in range(heads_per_group)],
        k_end=[head_lanes(k_end, i).astype(BF16) for i in range(heads_per_group)],
        ks=ks, dec=dec, a_diag=a_diag)


def _vector_decay_chunk(groups, v_all, heads_per_group, st_ref):
    n_groups = len(groups)
    n_heads = n_groups * heads_per_group
    rows_all = n_heads * CHUNK
    zeros_bf = jnp.zeros((CHUNK, LANES), BF16)

    lhs_rows = []
    for h in range(n_heads):
        g, i = divmod(h, heads_per_group)
        lhs_rows.append(jnp.concatenate(
            [groups[g]["qs"][i] if gg == g else zeros_bf for gg in range(n_groups)], axis=1))
    ks_cat = jnp.concatenate([grp["ks"] for grp in groups], axis=1)
    p_all = _dot_nt(jnp.concatenate(lhs_rows, axis=0), ks_cat)
    row_blk = _div_pow2(_iota((rows_all, N_OFF), 0) & (CHUNK - 1), SUB)
    col = _iota((rows_all, N_OFF), 1)
    lo = lax.shift_right_logical(row_blk * (row_blk - 1), 1) * SUB
    p_all = jnp.where((col >= lo) & (col < lo + row_blk * SUB), p_all, 0.0)

    a_all = jnp.concatenate([a for grp in groups for a in grp["a_diag"]], axis=0)
    a_all = jnp.concatenate([a_all, jnp.zeros((rows_all, LANES - CHUNK), F32)], axis=1)
    scores = jnp.concatenate([p_all, a_all], axis=1).astype(BF16)
    width = v_all.shape[1]
    v_rows = jnp.concatenate(
        [v_all[:i * SUB, :] for i in range(1, N_SUB)]
        + [jnp.zeros((N_OFF - N_OFF_USED, width), F32), v_all, jnp.zeros((LANES - CHUNK, width), F32)],
        axis=0).astype(BF16)
    r = _dot(scores, v_rows)
    outs = [r[h * CHUNK:(h + 1) * CHUNK, h * HEAD_W:(h + 1) * HEAD_W] for h in range(n_heads)]

    v_bf = v_all.astype(BF16)
    if heads_per_group == 1:
        st = st_ref[...]
        st_bf = st.reshape(n_heads * HEAD_W, LANES).astype(BF16)
        qe_all = jnp.concatenate([grp["q_exp"][0] for grp in groups], axis=0)
        inter = _dot_nt(qe_all, st_bf)
        ke_cat = jnp.concatenate([grp["k_end"][0] for grp in groups], axis=1)
        upd = _dot_tn(v_bf, ke_cat)
        for h in range(n_heads):
            outs[h] = outs[h] + inter[h * CHUNK:(h + 1) * CHUNK, h * HEAD_W:(h + 1) * HEAD_W]
            st_ref[h] = st[h] * groups[h]["dec"] + upd[h * HEAD_W:(h + 1) * HEAD_W, h * LANES:(h + 1) * LANES]
    else:
        for g, grp in enumerate(groups):
            st = st_ref[g]
            qe_rows = jnp.concatenate(grp["q_exp"], axis=0)
            inter = _dot_nt(qe_rows, st.astype(BF16))
            ke_rows = jnp.concatenate(grp["k_end"], axis=0)
            v_g = [v_bf[:, (g * heads_per_group + i) * HEAD_W:(g * heads_per_group + i + 1) * HEAD_W]
                   for i in range(heads_per_group)]
            upd = _dot_tn(jnp.concatenate(v_g, axis=0), ke_rows)
            for i in range(heads_per_group):
                h = g * heads_per_group + i
                outs[h] = outs[h] + inter[i * CHUNK:(i + 1) * CHUNK, :]
            st_ref[g] = st * grp["dec"] + upd
    return outs


def _modulated_norm(x, mod_ref, normg_ref):
    shift = mod_ref[0:1, 0:D_MODEL]
    scale = mod_ref[0:1, D_MODEL:2 * D_MODEL]
    h = (_rms(x, D_MODEL) * normg_ref[...]) * (1.0 + scale) + shift
    return h.astype(BF16)


def _in_proj(h_ref, wt_ref, proj_ref, dst, row0, row1):
    n_blk = 512
    for r0 in range(row0, row1, n_blk):
        r1 = min(r0 + n_blk, row1)
        proj_ref[:, dst + r0 - row0:dst + r1 - row0] = _dot_nt(h_ref[...], wt_ref[r0:r1, :])


def _in_proj_vector_part(h_ref, wt_ref, wt_gla_ref, proj_ref):
    _in_proj(h_ref, wt_ref, proj_ref, C_AQ, 0, ORIG_RQ)
    _in_proj(h_ref, wt_gla_ref, proj_ref, C_GQ, 0, ORIG_LR - ORIG_GQ)
    _in_proj(h_ref, wt_gla_ref, proj_ref, C_LR, GLA_TAIL_ROWS - LANES, GLA_TAIL_ROWS)


def _in_proj_scalar_part(h_ref, wt_ref, proj_ref):
    _in_proj(h_ref, wt_ref, proj_ref, C_RQ, ORIG_RQ, ORIG_DT)
    _in_proj(h_ref, wt_ref, proj_ref, C_DT, ORIG_DT, ORIG_DT + LANES)


def _mixer_prep(layer, proj_ref, hk_ref, gcum_ref, lb_logits_ref, w_gk2_ref, b_gk2_ref):
    tri_chunk = _lower_tri(TILE, CHUNK)

    lg = [lb_logits_ref[i:i + 1, :] for i in range(DEPTH)]
    lg_max = functools.reduce(jnp.maximum, lg)
    lg_exp = [jnp.exp(r - lg_max) for r in lg]
    lg_den = functools.reduce(lambda a, b: a + b, lg_exp)
    lower = jnp.zeros((1, BRANCH_W), F32)
    for i in range(1, layer + 1):
        lower = lower + lg_exp[i] / lg_den
    log_lb = jnp.log(lower)
    log1m_lb = jnp.log1p(-lower)
    proj_ref[:, C_AQ:C_AQ + BRANCH_W] = _silu(proj_ref[:, C_AQ:C_AQ + BRANCH_W])
    log_f = _logaddexp(log_lb, log1m_lb + _log_sigmoid(proj_ref[:, C_AF:C_AF + BRANCH_W]))
    hk_ref[...] = 1.0 - jnp.exp(log_f)
    proj_ref[:, C_AF:C_AF + BRANCH_W] = _dot3(tri_chunk, log_f * LOG2_E)

    low_rank = proj_ref[:, C_LR:C_LR + LANES]
    gk_gate = _dot(low_rank.astype(BF16), w_gk2_ref[...].astype(BF16)) + b_gk2_ref[...]
    gcum_ref[...] = _dot3(tri_chunk, _log_sigmoid(gk_gate) * (LOG2_E / GLA_GATE_TEMP))
    gla_dk = GLA_KEY_W // N_HEADS
    proj_ref[:, C_GQ:C_GQ + GLA_KEY_W] = proj_ref[:, C_GQ:C_GQ + GLA_KEY_W] * (gla_dk ** -0.5)


def _vector_decay_mixers(proj_ref, y_ref, hk_ref, gcum_ref, cum_sc, k_sc, st_hgrn, st_gla,
                         hgrn_g_ref, gla_g_ref):
    def chunk_body(ci):
        r0 = ci * CHUNK
        rows = slice(r0, r0 + CHUNK)
        groups = []
        for hd in range(N_HEADS):
            cs = slice(hd * HEAD_W, (hd + 1) * HEAD_W)
            groups.append(_key_group_terms(
                proj_ref[rows, C_AQ + cs.start:C_AQ + cs.stop], hk_ref[rows, cs],
                proj_ref[rows, C_AF + cs.start:C_AF + cs.stop],
                cum_sc.at[ci, hd], k_sc.at[ci, hd], 1))
        outs = []
        for p0 in range(0, N_HEADS, HEADS_PER_MATMUL):
            outs += _vector_decay_chunk(
                groups[p0:p0 + HEADS_PER_MATMUL],
                proj_ref[rows, C_AI + p0 * HEAD_W:C_AI + (p0 + HEADS_PER_MATMUL) * HEAD_W],
                1, st_hgrn.at[pl.ds(p0, HEADS_PER_MATMUL)])
        for hd, o in enumerate(outs):
            cs = slice(hd * HEAD_W, (hd + 1) * HEAD_W)
            g_act = _silu(proj_ref[rows, C_AG + cs.start:C_AG + cs.stop])
            y_ref[rows, cs] = (_rms(o, HEAD_W) * hgrn_g_ref[:, cs] * g_act).astype(BF16)
        groups = []
        for pr in range(N_HEADS // 2):
            ks_ = slice(pr * LANES, (pr + 1) * LANES)
            groups.append(_key_group_terms(
                proj_ref[rows, C_GQ + ks_.start:C_GQ + ks_.stop],
                proj_ref[rows, C_GK + ks_.start:C_GK + ks_.stop],
                gcum_ref[rows, ks_],
                cum_sc.at[ci, N_HEADS + pr], k_sc.at[ci, N_HEADS + pr], 2))
        outs = []
        for pr in range(N_HEADS // 2):
            outs += _vector_decay_chunk(
                groups[pr:pr + 1],
                proj_ref[rows, C_GV + 2 * pr * HEAD_W:C_GV + 2 * (pr + 1) * HEAD_W],
                2, st_gla.at[pl.ds(pr, 1)])
        for hd, o in enumerate(outs):
            cs = slice(hd * HEAD_W, (hd + 1) * HEAD_W)
            g_act = _silu(proj_ref[rows, C_GG + cs.start:C_GG + cs.stop])
            y_ref[rows, 3 * BRANCH_W + cs.start:3 * BRANCH_W + cs.stop] = (
                _rms(o, HEAD_W) * gla_g_ref[:, cs] * g_act).astype(BF16)

    for ci in range(TILE // CHUNK):
        chunk_body(ci)


def _ret_log2_gamma(hd):
    return math.log1p(-(2.0 ** -(RET_DECAY_EXP0 + hd))) * LOG2_E


def _init_retention_decays(ret_decay_ref, ret_edge_ref):
    t_col = _iota((TILE, TILE), 0)
    s_row = _iota((TILE, TILE), 1)
    t_minus_s = (t_col - s_row).astype(F32)
    t_plus1 = (_iota((TILE, HEAD_W), 0) + 1).astype(F32)
    s_to_end = (TILE - 1 - _iota((TILE, HEAD_W), 0)).astype(F32)
    for hd in range(N_HEADS):
        log2_gamma = _ret_log2_gamma(hd)
        ret_decay_ref[hd] = jnp.where(t_col >= s_row, jnp.exp2(t_minus_s * log2_gamma), 0.0)
        ret_edge_ref[hd, 0] = jnp.exp2(t_plus1 * log2_gamma)
        ret_edge_ref[hd, 1] = jnp.exp2(s_to_end * log2_gamma)


def _scalar_decay_mixers(proj_ref, y_ref, conv_ref, st_ret, st_ssd, ret_decay_ref, ret_edge_ref,
                         cos_ref, sin_ref,
                         ret_g_ref, conv_w_ref, conv_b_ref, dt_bias_ref, a_log_ref, d_skip_ref,
                         ssm_g_ref):
    tri_tile = _lower_tri(TILE, TILE)
    def cols(c0, width):
        return proj_ref[:, c0:c0 + width]

    tail = cols(C_DT, LANES)

    cos_t = cos_ref[...]
    sin_t = sin_ref[...]
    for hd in range(N_HEADS):
        cs = slice(hd * HEAD_W, (hd + 1) * HEAD_W)
        log2_gamma = _ret_log2_gamma(hd)
        rq = cols(C_RQ + cs.start, HEAD_W)
        rk = cols(C_RK + cs.start, HEAD_W)
        q = rq * cos_t + pltpu.roll(rq, HEAD_W // 2, 1) * sin_t
        k = (rk * cos_t + pltpu.roll(rk, HEAD_W // 2, 1) * sin_t) * (HEAD_W ** -0.5)
        v_bf = cols(C_RV + cs.start, HEAD_W).astype(BF16)
        q_bf = q.astype(BF16)
        scores = _dot_nt(q_bf, k.astype(BF16)) * ret_decay_ref[hd]
        st = st_ret[hd]
        o = (_dot(scores.astype(BF16), v_bf)
             + ret_edge_ref[hd, 0] * _dot(q_bf, st.astype(BF16)))
        k_end = k * ret_edge_ref[hd, 1]
        st_ret[hd] = st * (2.0 ** (TILE * log2_gamma)) + _dot_tn(k_end.astype(BF16), v_bf)
        g_act = _silu(cols(C_RG + cs.start, HEAD_W))
        y_ref[:, BRANCH_W + cs.start:BRANCH_W + cs.stop] = (
            _rms(o, HEAD_W) * ret_g_ref[:, cs] * g_act).astype(BF16)

    u = cols(C_XBC, SSM_CONV_CH)
    prev = conv_ref[...]
    row8 = _iota((SUBLANES, SSM_CONV_CH), 0)
    xbc = conv_b_ref[...] + conv_w_ref[SSM_CONV - 1:SSM_CONV, :] * u
    for d in range(1, SSM_CONV):
        u_d = pltpu.roll(u, d, 0)
        head = jnp.where(row8 < d, pltpu.roll(prev, d, 0), u_d[0:SUBLANES, :])
        u_d = jnp.concatenate([head, u_d[SUBLANES:, :]], axis=0)
        xbc = xbc + conv_w_ref[SSM_CONV - 1 - d:SSM_CONV - d, :] * u_d
    conv_ref[...] = u[TILE - SUBLANES:TILE, :]
    xbc = _silu(xbc)
    xs = xbc[:, 0:BRANCH_W]
    dt = _softplus(tail + dt_bias_ref[...])
    log_a = dt * (-LOG2_E * jnp.exp(a_log_ref[...]))
    cum = _dot3(tri_tile, log_a)
    cum_t = cum.T

    e_r = _iota((LANES, BRANCH_W), 0)
    e_c = _iota((LANES, BRANCH_W), 1)
    expand = jnp.where(e_r == _div_pow2(e_c, SSM_HEAD_DIM), 1.0, 0.0).astype(BF16)
    dt_e = _dot(dt.astype(BF16), expand)
    cum_e = _dot3_rhs(cum, expand)
    cum_last_e = cum_e[TILE - 1:TILE, :]
    d_e = _dot3_rhs(jnp.broadcast_to(d_skip_ref[...], (SUBLANES, LANES)), expand)[0:1, :]
    xdt = xs * dt_e
    x_end = (xdt * jnp.exp2(cum_last_e - cum_e)).astype(BF16)
    half = TILE // 2
    tri_half = _iota((half, half), 0) >= _iota((half, half), 1)
    hpg = SSM_HEADS // SSM_GROUPS
    gw = hpg * SSM_HEAD_DIM
    o_parts = []
    for g in range(SSM_GROUPS):
        b_bf = xbc[:, BRANCH_W + g * SSM_STATE:BRANCH_W + (g + 1) * SSM_STATE].astype(BF16)
        c_bf = xbc[:, BRANCH_W + (SSM_GROUPS + g) * SSM_STATE:
                   BRANCH_W + (SSM_GROUPS + g + 1) * SSM_STATE].astype(BF16)
        cb = _dot_nt(c_bf, b_bf)
        st = st_ssd[g]
        inter = _dot(c_bf, st.astype(BF16)) * jnp.exp2(cum_e[:, g * gw:(g + 1) * gw])
        for hh in range(hpg):
            hd = g * hpg + hh
            quad = {}
            for qi in range(2):
                for qj in range(qi + 1):
                    rq_, cq_ = slice(qi * half, (qi + 1) * half), slice(qj * half, (qj + 1) * half)
                    rel = cum[rq_, hd:hd + 1] - cum_t[hd:hd + 1, cq_]
                    w_q = cb[rq_, cq_] * jnp.exp2(rel)
                    quad[qi, qj] = (jnp.where(tri_half, w_q, 0.0) if qi == qj else w_q).astype(BF16)
            w = jnp.concatenate([jnp.concatenate([quad[0, 0], jnp.zeros((half, half), BF16)], axis=1),
                                 jnp.concatenate([quad[1, 0], quad[1, 1]], axis=1)], axis=0)
            o_parts.append(_dot(w, xdt[:, hd * SSM_HEAD_DIM:(hd + 1) * SSM_HEAD_DIM].astype(BF16))
                           + inter[:, hh * SSM_HEAD_DIM:(hh + 1) * SSM_HEAD_DIM])
        st_ssd[g] = (st * jnp.exp2(cum_last_e[:, g * gw:(g + 1) * gw])
                     + _dot_tn(b_bf, x_end[:, g * gw:(g + 1) * gw]))
    o_ssd = jnp.concatenate(o_parts, axis=1)
    y_ssd = (o_ssd + d_e * xs) * _silu(cols(C_MZ, BRANCH_W))
    gn = BRANCH_W // SSM_GROUPS
    for g in range(SSM_GROUPS):
        cs = slice(g * gn, (g + 1) * gn)
        y_ref[:, 2 * BRANCH_W + cs.start:2 * BRANCH_W + cs.stop] = (
            _rms(y_ssd[:, cs], gn) * ssm_g_ref[:, cs]).astype(BF16)


def _layer_kernel(layer, final,
                  x_ref, xn_ref, mod_ref, modn_ref, normg_ref, wt_ref, wt_gla_ref, w_out_ref,
                  cos_ref, sin_ref,
                  lb_logits_ref, hgrn_g_ref, ret_g_ref, conv_w_ref, conv_b_ref, dt_bias_ref,
                  a_log_ref, d_skip_ref, ssm_g_ref, w_gk2_ref, b_gk2_ref, gla_g_ref, final_g_ref,
                  o_ref,
                  proj_ref, h_ref, y_ref, conv_ref, hk_ref, gcum_ref, cum_sc, k_sc,
                  st_hgrn, st_ret, st_ssd, st_gla, ret_decay_ref, ret_edge_ref):
    b_idx = pl.program_id(0)
    t_idx = pl.program_id(1)

    def project_vector_part(x_tile, tile_mod_ref):
        h_ref[...] = _modulated_norm(x_tile, tile_mod_ref, normg_ref)
        _in_proj_vector_part(h_ref, wt_ref, wt_gla_ref, proj_ref)
        _mixer_prep(layer, proj_ref, hk_ref, gcum_ref, lb_logits_ref, w_gk2_ref, b_gk2_ref)

    @pl.when(t_idx == 0)
    def _():
        st_hgrn[...] = jnp.zeros_like(st_hgrn)
        st_ret[...] = jnp.zeros_like(st_ret)
        st_ssd[...] = jnp.zeros_like(st_ssd)
        st_gla[...] = jnp.zeros_like(st_gla)
        conv_ref[...] = jnp.zeros_like(conv_ref)

    @pl.when((t_idx == 0) & (b_idx == 0))
    def _():
        project_vector_part(x_ref[...], mod_ref)
        _init_retention_decays(ret_decay_ref, ret_edge_ref)

    _in_proj_scalar_part(h_ref, wt_ref, proj_ref)
    _vector_decay_mixers(proj_ref, y_ref, hk_ref, gcum_ref, cum_sc, k_sc, st_hgrn, st_gla,
                         hgrn_g_ref, gla_g_ref)
    project_vector_part(xn_ref[...], modn_ref)
    _scalar_decay_mixers(proj_ref, y_ref, conv_ref, st_ret, st_ssd, ret_decay_ref, ret_edge_ref,
                         cos_ref, sin_ref,
                         ret_g_ref, conv_w_ref, conv_b_ref, dt_bias_ref, a_log_ref, d_skip_ref,
                         ssm_g_ref)
    gate = mod_ref[0:1, 2 * D_MODEL:3 * D_MODEL]
    x_new = x_ref[...] + gate * _dot(y_ref[...], w_out_ref[...])
    if final:
        x_new = _rms(x_new, D_MODEL) * final_g_ref[...]
    o_ref[...] = x_new


def _resident(shape):
    nd = len(shape)
    return pl.BlockSpec(shape, lambda b, t, _nd=nd: (0,) * _nd, pipeline_mode=pl.Buffered(1))


def _layer_call(layer, final, x, mod, params):
    batch, seq, _ = x.shape
    n_tiles = seq // TILE
    n_chunks = TILE // CHUNK

    def next_tile(t):
        return jnp.where(t == n_tiles - 1, 0, t + 1)

    def next_batch(b, t):
        return jnp.where(t == n_tiles - 1, jnp.minimum(b + 1, batch - 1), b)

    small = [params[k] for k in (
        "lb_logits", "hgrn_g", "ret_g", "conv_w", "conv_b", "dt_bias", "a_log", "d_skip",
        "ssm_g", "w_gk2", "b_gk2", "gla_g", "final_g")]
    in_specs = [
        pl.BlockSpec((None, TILE, D_MODEL), lambda b, t: (b, t, 0)),
        pl.BlockSpec((None, TILE, D_MODEL), lambda b, t: (next_batch(b, t), next_tile(t), 0)),
        pl.BlockSpec((None, SUBLANES, 3 * D_MODEL), lambda b, t: (b, 0, 0)),
        pl.BlockSpec((None, SUBLANES, 3 * D_MODEL), lambda b, t: (next_batch(b, t), 0, 0)),
        _resident((1, D_MODEL)),
        pl.BlockSpec((None, ORIG_W, D_MODEL), lambda b, t: (layer, 0, 0), pipeline_mode=pl.Buffered(1)),
        pl.BlockSpec((None, GLA_TAIL_ROWS, D_MODEL), lambda b, t: (layer, 0, 0),
                     pipeline_mode=pl.Buffered(1)),
        pl.BlockSpec((None, D_INNER, D_MODEL), lambda b, t: (layer, 0, 0), pipeline_mode=pl.Buffered(1)),
        pl.BlockSpec((TILE, HEAD_W), lambda b, t: (t, 0)),
        pl.BlockSpec((TILE, HEAD_W), lambda b, t: (t, 0)),
    ] + [_resident(a.shape) for a in small]
    scratch = [
        pltpu.VMEM((TILE, PROJ_W), F32),
        pltpu.VMEM((TILE, D_MODEL), BF16),
        pltpu.VMEM((TILE, D_INNER), BF16),
        pltpu.VMEM((SUBLANES, SSM_CONV_CH), F32),
        pltpu.VMEM((TILE, BRANCH_W), F32),
        pltpu.VMEM((TILE, GLA_KEY_W), F32),
        pltpu.VMEM((n_chunks, N_KEY_GROUPS, CHUNK, LANES), F32),
        pltpu.VMEM((n_chunks, N_KEY_GROUPS, CHUNK, LANES), F32),
        pltpu.VMEM((N_HEADS, HEAD_W, HEAD_W), F32),
        pltpu.VMEM((N_HEADS, HEAD_W, HEAD_W), F32),
        pltpu.VMEM((SSM_GROUPS, SSM_STATE, BRANCH_W // SSM_GROUPS), F32),
        pltpu.VMEM((N_HEADS // 2, HEAD_W, LANES), F32),
        pltpu.VMEM((N_HEADS, TILE, TILE), F32),
        pltpu.VMEM((N_HEADS, 2, TILE, HEAD_W), F32),
    ]
    return pl.pallas_call(
        functools.partial(_layer_kernel, layer, final),
        grid=(batch, n_tiles),
        in_specs=in_specs,
        out_specs=pl.BlockSpec((None, TILE, D_MODEL), lambda b, t: (b, t, 0)),
        out_shape=jax.ShapeDtypeStruct(x.shape, F32),
        scratch_shapes=scratch,
        compiler_params=pltpu.CompilerParams(
            dimension_semantics=("arbitrary", "arbitrary"),
            vmem_limit_bytes=VMEM_LIMIT_BYTES),
        name=f"mixer_layer{layer}",
    )(x, x, mod, mod, params["norm_g"], params["wt"], params["wt_gla"], params["w_out"], params["cos"], params["sin"], *small)


def _pad_lanes(v, lane0=0):
    return jnp.zeros((1, LANES), F32).at[0, lane0:lane0 + v.shape[0]].set(v)


def kernel(x, c, w_ada, b_ada, norm_g, w_in, hgrn_lb_logits, hgrn_onorm_g, ret_onorm_g, ssm_conv_w,
           ssm_conv_b, ssm_dt_bias, ssm_a_log, ssm_d, ssm_norm_g, gla_w_gk2, gla_b_gk2, gla_onorm_g,
           w_out, final_g):
    batch, seq, _ = x.shape
    c_pad = jnp.zeros((SUBLANES, D_MODEL), F32).at[:batch].set(c)
    mod_all = _ada_mod(c_pad, w_ada, b_ada)

    inv_freq = ROPE_BASE ** (-np.arange(0, HEAD_W, 2, dtype=np.float64) / HEAD_W)
    ang = np.arange(seq, dtype=np.float64)[:, None] * inv_freq[None, :]
    cos_tab = jnp.asarray(np.concatenate([np.cos(ang), np.cos(ang)], axis=-1), F32)
    sin_tab = jnp.asarray(np.concatenate([-np.sin(ang), np.sin(ang)], axis=-1), F32)

    wt = jnp.swapaxes(w_in, 1, 2).astype(BF16)
    wt_gla = wt[:, ORIG_GQ:, :]
    w_out_bf = w_out.astype(BF16)

    for layer in range(DEPTH):
        w_gk2_p = jnp.zeros((LANES, GLA_KEY_W), F32).at[LR_LANE0:].set(gla_w_gk2[layer])
        params = dict(
            norm_g=norm_g[layer][None, :], wt=wt, wt_gla=wt_gla, w_out=w_out_bf,
            cos=cos_tab, sin=sin_tab,
            lb_logits=hgrn_lb_logits, hgrn_g=hgrn_onorm_g[layer][None, :],
            ret_g=ret_onorm_g[layer][None, :], conv_w=ssm_conv_w[layer],
            conv_b=ssm_conv_b[layer][None, :], dt_bias=_pad_lanes(ssm_dt_bias[layer]),
            a_log=_pad_lanes(ssm_a_log[layer]), d_skip=_pad_lanes(ssm_d[layer]),
            ssm_g=ssm_norm_g[layer][None, :], w_gk2=w_gk2_p, b_gk2=gla_b_gk2[layer][None, :],
            gla_g=gla_onorm_g[layer][None, :], final_g=final_g[None, :])
        mod = jnp.broadcast_to(mod_all[layer][:batch, None, :], (batch, SUBLANES, 3 * D_MODEL))
        x = _layer_call(layer, layer == DEPTH - 1, x, mod, params)
    return x
```

```python
import functools
import math

import numpy as np
import jax
import jax.numpy as jnp
from jax import lax
from jax.experimental import pallas as pl
from jax.experimental.pallas import tpu as pltpu

F32 = jnp.float32
BF16 = jnp.bfloat16

D_MODEL = 1024
DEPTH = 2
BRANCH_W = 512
D_INNER = 4 * BRANCH_W
HEAD_W = 128
N_HEADS = 4
RET_DECAY_EXP0 = 5.0
ROPE_BASE = 10000.0
SSM_HEADS = 8
SSM_HEAD_DIM = 64
SSM_GROUPS = 2
SSM_STATE = 128
SSM_CONV = 4
SSM_CONV_CH = BRANCH_W + 2 * SSM_GROUPS * SSM_STATE
GLA_KEY_W = 256
GLA_LOWRANK = 16
GLA_GATE_TEMP = 16.0
EPS = 1e-6
LOG2_E = math.log2(math.e)

LANES = 128
SUBLANES = 8

C_AQ, C_AF, C_AI, C_AG = 0, 512, 1024, 1536
C_GQ, C_GK, C_GV, C_GG = 2048, 2304, 2560, 3072
C_LR = 3584
VEC_W = C_LR + LANES
C_RQ, C_RK, C_RV, C_RG = 3712, 4224, 4736, 5248
C_MZ, C_XBC = 5760, 6272
C_DT = 7296
PROJ_W = C_DT + LANES
ORIG_RQ = 2048
ORIG_DT = 5632
ORIG_GQ = ORIG_DT + SSM_HEADS
ORIG_LR = ORIG_GQ + 2 * GLA_KEY_W + 2 * BRANCH_W
ORIG_W = ORIG_LR + GLA_LOWRANK
GLA_TAIL_ROWS = ORIG_W - ORIG_GQ
LR_LANE0 = LANES - GLA_LOWRANK

TILE = 256
CHUNK = 64
SUB = 8
N_SUB = CHUNK // SUB
N_KEY_GROUPS = N_HEADS + N_HEADS // 2
VMEM_LIMIT_BYTES = 56 * 1024 * 1024


def _dot(a, b):
    return jnp.dot(a, b, preferred_element_type=F32)


def _dot_nt(a, b):
    return lax.dot_general(a, b, (((1,), (1,)), ((), ())), preferred_element_type=F32)


def _dot_tn(a, b):
    return lax.dot_general(a, b, (((0,), (0,)), ((), ())), preferred_element_type=F32)


def _dot3(m_bf16, x):
    hi = x.astype(BF16)
    r1 = x - hi.astype(F32)
    mid = r1.astype(BF16)
    lo = (r1 - mid.astype(F32)).astype(BF16)
    return _dot(m_bf16, hi) + _dot(m_bf16, mid) + _dot(m_bf16, lo)


def _dot3_rhs(x, m_bf16):
    hi = x.astype(BF16)
    r1 = x - hi.astype(F32)
    mid = r1.astype(BF16)
    lo = (r1 - mid.astype(F32)).astype(BF16)
    return _dot(hi, m_bf16) + _dot(mid, m_bf16) + _dot(lo, m_bf16)


def _silu(x):
    hx = 0.5 * x
    return hx + hx * jnp.tanh(hx)


def _log1p_exp_neg_abs(x):
    return jnp.log(1.0 + jnp.exp(-jnp.abs(x)))


def _log_sigmoid(x):
    return jnp.minimum(x, 0.0) - _log1p_exp_neg_abs(x)


def _softplus(x):
    return jnp.maximum(x, 0.0) + _log1p_exp_neg_abs(x)


def _logaddexp(a, b):
    return jnp.maximum(a, b) + _log1p_exp_neg_abs(a - b)


def _rms(x, width):
    return x * lax.rsqrt(jnp.sum(x * x, axis=-1, keepdims=True) * (1.0 / width) + EPS)


def _iota(shape, dim):
    return lax.broadcasted_iota(jnp.int32, shape, dim)


def _div_pow2(x, d):
    shift = d.bit_length() - 1
    assert 1 << shift == d
    return lax.shift_right_logical(x, shift)


def _lower_tri(n, block):
    r = _iota((n, n), 0)
    c = _iota((n, n), 1)
    keep = (c <= r) & (_div_pow2(r, block) == _div_pow2(c, block))
    return jnp.where(keep, 1.0, 0.0).astype(BF16)


def _ada_kernel(c_ref, w_ref, b_ref, o_ref):
    c_act = _silu(c_ref[...]).astype(BF16)
    o_ref[...] = _dot(c_act, w_ref[...].astype(BF16)) + b_ref[...]


def _ada_mod(c_pad, w_ada, b_ada):
    n_blk = 3
    return pl.pallas_call(
        _ada_kernel,
        grid=(DEPTH, n_blk),
        in_specs=[
            pl.BlockSpec((SUBLANES, D_MODEL), lambda l, j: (0, 0)),
            pl.BlockSpec((None, D_MODEL, D_MODEL), lambda l, j: (l, 0, j)),
            pl.BlockSpec((None, 1, D_MODEL), lambda l, j: (l, 0, j)),
        ],
        out_specs=pl.BlockSpec((None, SUBLANES, D_MODEL), lambda l, j: (l, 0, j)),
        out_shape=jax.ShapeDtypeStruct((DEPTH, SUBLANES, 3 * D_MODEL), F32),
        name="ada_mod",
    )(c_pad, w_ada, b_ada.reshape(DEPTH, 1, 3 * D_MODEL))


HEADS_PER_MATMUL = 2
N_OFF_USED = SUB * (N_SUB * (N_SUB - 1) // 2)
N_OFF = -(-N_OFF_USED // LANES) * LANES


def _key_group_terms(q, k, cum, cum_sc, k_sc, heads_per_group):
    lane_w = LANES // heads_per_group

    def head_lanes(x, i):
        if heads_per_group == 1:
            return x
        lane = _iota(x.shape, 1)
        return jnp.where((lane >= i * lane_w) & (lane < (i + 1) * lane_w), x, 0.0)

    cum_sc[...] = cum
    k_sc[...] = k
    cum_last = cum_sc[CHUNK - 1:CHUNK, :]
    q_exp = q * jnp.exp2(cum)
    k_end = k * jnp.exp2(cum_last - cum)
    dec = jnp.exp2(cum_last)

    qs_parts = [jnp.zeros((SUB, LANES), F32)]
    ks_parts = []
    for i in range(1, N_SUB):
        ref_i = cum_sc[i * SUB - 1:i * SUB, :]
        qs_parts.append(q[i * SUB:(i + 1) * SUB, :] * jnp.exp2(cum[i * SUB:(i + 1) * SUB, :] - ref_i))
        ks_parts.append(k[:i * SUB, :] * jnp.exp2(ref_i - cum[:i * SUB, :]))
    ks_parts.append(jnp.zeros((N_OFF - N_OFF_USED, LANES), F32))
    qs = jnp.concatenate(qs_parts, axis=0)
    ks = jnp.concatenate(ks_parts, axis=0).astype(BF16)

    lane_c = _iota((SUB, CHUNK), 1)
    row_c = _iota((SUB, CHUNK), 0)
    diag_parts = [[] for _ in range(heads_per_group)]
    for i in range(N_SUB):
        acc = [jnp.zeros((SUB, CHUNK), F32) for _ in range(heads_per_group)]
        q_i = q[i * SUB:(i + 1) * SUB, :]
        cum_i = cum[i * SUB:(i + 1) * SUB, :]
        for j in range(SUB):
            s = i * SUB + j
            w = q_i * jnp.exp2(cum_i - cum_sc[s:s + 1, :]) * k_sc[s:s + 1, :]
            for hd in range(heads_per_group):
                a_col = jnp.sum(head_lanes(w, hd), axis=-1, keepdims=True)
                acc[hd] = jnp.where(lane_c == s, a_col, acc[hd])
        causal = (row_c + i * SUB) >= lane_c
        for hd in range(heads_per_group):
            diag_parts[hd].append(jnp.where(causal, acc[hd], 0.0))
    a_diag = [jnp.concatenate(p, axis=0) for p in diag_parts]
    return dict(
        qs=[head_lanes(qs, i).astype(BF16) for i in range(heads_per_group)],
        q_exp=[head_lanes(q_exp, i).astype(BF16) for i in range(heads_per_group)],
        k_end=[head_lanes(k_end, i).astype(BF16) for i in range(heads_per_group)],
        ks=ks, dec=dec, a_diag=a_diag)


def _vector_decay_chunk(groups, v_all, heads_per_group, st_ref):
    n_groups = len(groups)
    n_heads = n_groups * heads_per_group
    rows_all = n_heads * CHUNK
    zeros_bf = jnp.zeros((CHUNK, LANES), BF16)

    lhs_rows = []
    for h in range(n_heads):
        g, i = divmod(h, heads_per_group)
        lhs_rows.append(jnp.concatenate(
            [groups[g]["qs"][i] if gg == g else zeros_bf for gg in range(n_groups)], axis=1))
    ks_cat = jnp.concatenate([grp["ks"] for grp in groups], axis=1)
    p_all = _dot_nt(jnp.concatenate(lhs_rows, axis=0), ks_cat)
    row_blk = _div_pow2(_iota((rows_all, N_OFF), 0) & (CHUNK - 1), SUB)
    col = _iota((rows_all, N_OFF), 1)
    lo = lax.shift_right_logical(row_blk * (row_blk - 1), 1) * SUB
    p_all = jnp.where((col >= lo) & (col < lo + row_blk * SUB), p_all, 0.0)

    a_all = jnp.concatenate([a for grp in groups for a in grp["a_diag"]], axis=0)
    a_all = jnp.concatenate([a_all, jnp.zeros((rows_all, LANES - CHUNK), F32)], axis=1)
    scores = jnp.concatenate([p_all, a_all], axis=1).astype(BF16)
    width = v_all.shape[1]
    v_rows = jnp.concatenate(
        [v_all[:i * SUB, :] for i in range(1, N_SUB)]
        + [jnp.zeros((N_OFF - N_OFF_USED, width), F32), v_all, jnp.zeros((LANES - CHUNK, width), F32)],
        axis=0).astype(BF16)
    r = _dot(scores, v_rows)
    outs = [r[h * CHUNK:(h + 1) * CHUNK, h * HEAD_W:(h + 1) * HEAD_W] for h in range(n_heads)]

    v_bf = v_all.astype(BF16)
    if heads_per_group == 1:
        st = st_ref[...]
        st_bf = st.reshape(n_heads * HEAD_W, LANES).astype(BF16)
        qe_all = jnp.concatenate([grp["q_exp"][0] for grp in groups], axis=0)
        inter = _dot_nt(qe_all, st_bf)
        ke_cat = jnp.concatenate([grp["k_end"][0] for grp in groups], axis=1)
        upd = _dot_tn(v_bf, ke_cat)
        for h in range(n_heads):
            outs[h] = outs[h] + inter[h * CHUNK:(h + 1) * CHUNK, h * HEAD_W:(h + 1) * HEAD_W]
            st_ref[h] = st[h] * groups[h]["dec"] + upd[h * HEAD_W:(h + 1) * HEAD_W, h * LANES:(h + 1) * LANES]
    else:
        for g, grp in enumerate(groups):
            st = st_ref[g]
            qe_rows = jnp.concatenate(grp["q_exp"], axis=0)
            inter = _dot_nt(qe_rows, st.astype(BF16))
            ke_rows = jnp.concatenate(grp["k_end"], axis=0)
            v_g = [v_bf[:, (g * heads_per_group + i) * HEAD_W:(g * heads_per_group + i + 1) * HEAD_W]
                   for i in range(heads_per_group)]
            upd = _dot_tn(jnp.concatenate(v_g, axis=0), ke_rows)
            for i in range(heads_per_group):
                h = g * heads_per_group + i
                outs[h] = outs[h] + inter[i * CHUNK:(i + 1) * CHUNK, :]
            st_ref[g] = st * grp["dec"] + upd
    return outs


def _modulated_norm(x, mod_ref, normg_ref):
    shift = mod_ref[0:1, 0:D_MODEL]
    scale = mod_ref[0:1, D_MODEL:2 * D_MODEL]
    h = (_rms(x, D_MODEL) * normg_ref[...]) * (1.0 + scale) + shift
    return h.astype(BF16)


def _in_proj(h_ref, wt_ref, proj_ref, dst, row0, row1):
    n_blk = 1024
    for r0 in range(row0, row1, n_blk):
        r1 = min(r0 + n_blk, row1)
        proj_ref[:, dst + r0 - row0:dst + r1 - row0] = _dot_nt(h_ref[...], wt_ref[r0:r1, :])


def _in_proj_vector_part(h_ref, wt_ref, wt_gla_ref, proj_ref):
    _in_proj(h_ref, wt_ref, proj_ref, C_AQ, 0, ORIG_RQ)
    _in_proj(h_ref, wt_gla_ref, proj_ref, C_GQ, 0, ORIG_LR - ORIG_GQ)
    _in_proj(h_ref, wt_gla_ref, proj_ref, C_LR, GLA_TAIL_ROWS - LANES, GLA_TAIL_ROWS)


def _in_proj_scalar_part(h_ref, wt_ref, proj_ref):
    _in_proj(h_ref, wt_ref, proj_ref, C_RQ, ORIG_RQ, ORIG_DT)
    _in_proj(h_ref, wt_ref, proj_ref, C_DT, ORIG_DT, ORIG_DT + LANES)


def _mixer_prep(layer, proj_ref, hk_ref, gcum_ref, lb_logits_ref, w_gk2_ref, b_gk2_ref):
    tri_chunk = _lower_tri(TILE, CHUNK)

    lg = [lb_logits_ref[i:i + 1, :] for i in range(DEPTH)]
    lg_max = functools.reduce(jnp.maximum, lg)
    lg_exp = [jnp.exp(r - lg_max) for r in lg]
    lg_den = functools.reduce(lambda a, b: a + b, lg_exp)
    lower = jnp.zeros((1, BRANCH_W), F32)
    for i in range(1, layer + 1):
        lower = lower + lg_exp[i] / lg_den
    log_lb = jnp.log(lower)
    log1m_lb = jnp.log1p(-lower)
    proj_ref[:, C_AQ:C_AQ + BRANCH_W] = _silu(proj_ref[:, C_AQ:C_AQ + BRANCH_W])
    log_f = _logaddexp(log_lb, log1m_lb + _log_sigmoid(proj_ref[:, C_AF:C_AF + BRANCH_W]))
    hk_ref[...] = 1.0 - jnp.exp(log_f)
    proj_ref[:, C_AF:C_AF + BRANCH_W] = _dot3(tri_chunk, log_f * LOG2_E)

    low_rank = proj_ref[:, C_LR:C_LR + LANES]
    gk_gate = _dot(low_rank.astype(BF16), w_gk2_ref[...].astype(BF16)) + b_gk2_ref[...]
    gcum_ref[...] = _dot3(tri_chunk, _log_sigmoid(gk_gate) * (LOG2_E / GLA_GATE_TEMP))
    gla_dk = GLA_KEY_W // N_HEADS
    proj_ref[:, C_GQ:C_GQ + GLA_KEY_W] = proj_ref[:, C_GQ:C_GQ + GLA_KEY_W] * (gla_dk ** -0.5)


def _vector_decay_mixers(proj_ref, y_ref, hk_ref, gcum_ref, cum_sc, k_sc, st_hgrn, st_gla,
                         hgrn_g_ref, gla_g_ref):
    def chunk_body(ci):
        r0 = ci * CHUNK
        rows = slice(r0, r0 + CHUNK)
        groups = []
        for hd in range(N_HEADS):
            cs = slice(hd * HEAD_W, (hd + 1) * HEAD_W)
            groups.append(_key_group_terms(
                proj_ref[rows, C_AQ + cs.start:C_AQ + cs.stop], hk_ref[rows, cs],
                proj_ref[rows, C_AF + cs.start:C_AF + cs.stop],
                cum_sc.at[ci, hd], k_sc.at[ci, hd], 1))
        outs = []
        for p0 in range(0, N_HEADS, HEADS_PER_MATMUL):
            outs += _vector_decay_chunk(
                groups[p0:p0 + HEADS_PER_MATMUL],
                proj_ref[rows, C_AI + p0 * HEAD_W:C_AI + (p0 + HEADS_PER_MATMUL) * HEAD_W],
                1, st_hgrn.at[pl.ds(p0, HEADS_PER_MATMUL)])
        for hd, o in enumerate(outs):
            cs = slice(hd * HEAD_W, (hd + 1) * HEAD_W)
            g_act = _silu(proj_ref[rows, C_AG + cs.start:C_AG + cs.stop])
            y_ref[rows, cs] = (_rms(o, HEAD_W) * hgrn_g_ref[:, cs] * g_act).astype(BF16)
        groups = []
        for pr in range(N_HEADS // 2):
            ks_ = slice(pr * LANES, (pr + 1) * LANES)
            groups.append(_key_group_terms(
                proj_ref[rows, C_GQ + ks_.start:C_GQ + ks_.stop],
                proj_ref[rows, C_GK + ks_.start:C_GK + ks_.stop],
                gcum_ref[rows, ks_],
                cum_sc.at[ci, N_HEADS + pr], k_sc.at[ci, N_HEADS + pr], 2))
        outs = []
        for pr in range(N_HEADS // 2):
            outs += _vector_decay_chunk(
                groups[pr:pr + 1],
                proj_ref[rows, C_GV + 2 * pr * HEAD_W:C_GV + 2 * (pr + 1) * HEAD_W],
                2, st_gla.at[pl.ds(pr, 1)])
        for hd, o in enumerate(outs):
            cs = slice(hd * HEAD_W, (hd + 1) * HEAD_W)
            g_act = _silu(proj_ref[rows, C_GG + cs.start:C_GG + cs.stop])
            y_ref[rows, 3 * BRANCH_W + cs.start:3 * BRANCH_W + cs.stop] = (
                _rms(o, HEAD_W) * gla_g_ref[:, cs] * g_act).astype(BF16)

    for ci in range(TILE // CHUNK):
        chunk_body(ci)


def _ret_log2_gamma(hd):
    return math.log1p(-(2.0 ** -(RET_DECAY_EXP0 + hd))) * LOG2_E


def _init_retention_decays(ret_decay_ref, ret_edge_ref):
    t_col = _iota((TILE, TILE), 0)
    s_row = _iota((TILE, TILE), 1)
    t_minus_s = (t_col - s_row).astype(F32)
    t_plus1 = (_iota((TILE, HEAD_W), 0) + 1).astype(F32)
    s_to_end = (TILE - 1 - _iota((TILE, HEAD_W), 0)).astype(F32)
    for hd in range(N_HEADS):
        log2_gamma = _ret_log2_gamma(hd)
        ret_decay_ref[hd] = jnp.where(t_col >= s_row, jnp.exp2(t_minus_s * log2_gamma), 0.0)
        ret_edge_ref[hd, 0] = jnp.exp2(t_plus1 * log2_gamma)
        ret_edge_ref[hd, 1] = jnp.exp2(s_to_end * log2_gamma)


def _scalar_decay_mixers(proj_ref, y_ref, conv_ref, st_ret, st_ssd, ret_decay_ref, ret_edge_ref,
                         cos_ref, sin_ref,
                         ret_g_ref, conv_w_ref, conv_b_ref, dt_bias_ref, a_log_ref, d_skip_ref,
                         ssm_g_ref):
    tri_tile = _lower_tri(TILE, TILE)
    def cols(c0, width):
        return proj_ref[:, c0:c0 + width]

    tail = cols(C_DT, LANES)

    cos_t = cos_ref[...]
    sin_t = sin_ref[...]
    for hd in range(N_HEADS):
        cs = slice(hd * HEAD_W, (hd + 1) * HEAD_W)
        log2_gamma = _ret_log2_gamma(hd)
        rq = cols(C_RQ + cs.start, HEAD_W)
        rk = cols(C_RK + cs.start, HEAD_W)
        q = rq * cos_t + pltpu.roll(rq, HEAD_W // 2, 1) * sin_t
        k = (rk * cos_t + pltpu.roll(rk, HEAD_W // 2, 1) * sin_t) * (HEAD_W ** -0.5)
        v_bf = cols(C_RV + cs.start, HEAD_W).astype(BF16)
        q_bf = q.astype(BF16)
        scores = _dot_nt(q_bf, k.astype(BF16)) * ret_decay_ref[hd]
        st = st_ret[hd]
        o = (_dot(scores.astype(BF16), v_bf)
             + ret_edge_ref[hd, 0] * _dot(q_bf, st.astype(BF16)))
        k_end = k * ret_edge_ref[hd, 1]
        st_ret[hd] = st * (2.0 ** (TILE * log2_gamma)) + _dot_tn(k_end.astype(BF16), v_bf)
        g_act = _silu(cols(C_RG + cs.start, HEAD_W))
        y_ref[:, BRANCH_W + cs.start:BRANCH_W + cs.stop] = (
            _rms(o, HEAD_W) * ret_g_ref[:, cs] * g_act).astype(BF16)

    u = cols(C_XBC, SSM_CONV_CH)
    prev = conv_ref[...]
    row8 = _iota((SUBLANES, SSM_CONV_CH), 0)
    xbc = conv_b_ref[...] + conv_w_ref[SSM_CONV - 1:SSM_CONV, :] * u
    for d in range(1, SSM_CONV):
        u_d = pltpu.roll(u, d, 0)
        head = jnp.where(row8 < d, pltpu.roll(prev, d, 0), u_d[0:SUBLANES, :])
        u_d = jnp.concatenate([head, u_d[SUBLANES:, :]], axis=0)
        xbc = xbc + conv_w_ref[SSM_CONV - 1 - d:SSM_CONV - d, :] * u_d
    conv_ref[...] = u[TILE - SUBLANES:TILE, :]
    xbc = _silu(xbc)
    xs = xbc[:, 0:BRANCH_W]
    dt = _softplus(tail + dt_bias_ref[...])
    log_a = dt * (-LOG2_E * jnp.exp(a_log_ref[...]))
    cum = _dot3(tri_tile, log_a)
    cum_t = cum.T

    e_r = _iota((LANES, BRANCH_W), 0)
    e_c = _iota((LANES, BRANCH_W), 1)
    expand = jnp.where(e_r == _div_pow2(e_c, SSM_HEAD_DIM), 1.0, 0.0).astype(BF16)
    dt_e = _dot(dt.astype(BF16), expand)
    cum_e = _dot3_rhs(cum, expand)
    cum_last_e = cum_e[TILE - 1:TILE, :]
    d_e = _dot3_rhs(jnp.broadcast_to(d_skip_ref[...], (SUBLANES, LANES)), expand)[0:1, :]
    xdt = xs * dt_e
    x_end = (xdt * jnp.exp2(cum_last_e - cum_e)).astype(BF16)
    half = TILE // 2
    tri_half = _iota((half, half), 0) >= _iota((half, half), 1)
    hpg = SSM_HEADS // SSM_GROUPS
    gw = hpg * SSM_HEAD_DIM
    o_parts = []
    for g in range(SSM_GROUPS):
        b_bf = xbc[:, BRANCH_W + g * SSM_STATE:BRANCH_W + (g + 1) * SSM_STATE].astype(BF16)
        c_bf = xbc[:, BRANCH_W + (SSM_GROUPS + g) * SSM_STATE:
                   BRANCH_W + (SSM_GROUPS + g + 1) * SSM_STATE].astype(BF16)
        cb = _dot_nt(c_bf, b_bf)
        st = st_ssd[g]
        inter = _dot(c_bf, st.astype(BF16)) * jnp.exp2(cum_e[:, g * gw:(g + 1) * gw])
        w_rows = []
        for hh in range(hpg):
            hd = g * hpg + hh
            quad = {}
            for qi in range(2):
                for qj in range(qi + 1):
                    rq_, cq_ = slice(qi * half, (qi + 1) * half), slice(qj * half, (qj + 1) * half)
                    rel = cum[rq_, hd:hd + 1] - cum_t[hd:hd + 1, cq_]
                    w_q = cb[rq_, cq_] * jnp.exp2(rel)
                    quad[qi, qj] = (jnp.where(tri_half, w_q, 0.0) if qi == qj else w_q).astype(BF16)
            w_rows += [jnp.concatenate([quad[0, 0], jnp.zeros((half, half), BF16)], axis=1),
                       jnp.concatenate([quad[1, 0], quad[1, 1]], axis=1)]
        intra = _dot(jnp.concatenate(w_rows, axis=0), xdt[:, g * gw:(g + 1) * gw].astype(BF16))
        for hh in range(hpg):
            lanes = slice(hh * SSM_HEAD_DIM, (hh + 1) * SSM_HEAD_DIM)
            o_parts.append(intra[hh * TILE:(hh + 1) * TILE, lanes] + inter[:, lanes])
        st_ssd[g] = (st * jnp.exp2(cum_last_e[:, g * gw:(g + 1) * gw])
                     + _dot_tn(b_bf, x_end[:, g * gw:(g + 1) * gw]))
    o_ssd = jnp.concatenate(o_parts, axis=1)
    y_ssd = (o_ssd + d_e * xs) * _silu(cols(C_MZ, BRANCH_W))
    gn = BRANCH_W // SSM_GROUPS
    for g in range(SSM_GROUPS):
        cs = slice(g * gn, (g + 1) * gn)
        y_ref[:, 2 * BRANCH_W + cs.start:2 * BRANCH_W + cs.stop] = (
            _rms(y_ssd[:, cs], gn) * ssm_g_ref[:, cs]).astype(BF16)


def _layer_kernel(layer, final,
                  x_ref, xn_ref, mod_ref, modn_ref, normg_ref, wt_ref, wt_gla_ref, w_out_ref,
                  cos_ref, sin_ref,
                  lb_logits_ref, hgrn_g_ref, ret_g_ref, conv_w_ref, conv_b_ref, dt_bias_ref,
                  a_log_ref, d_skip_ref, ssm_g_ref, w_gk2_ref, b_gk2_ref, gla_g_ref, final_g_ref,
                  o_ref,
                  proj_ref, h_ref, y_ref, conv_ref, hk_ref, gcum_ref, cum_sc, k_sc,
                  st_hgrn, st_ret, st_ssd, st_gla, ret_decay_ref, ret_edge_ref):
    b_idx = pl.program_id(0)
    t_idx = pl.program_id(1)

    def project_vector_part(x_tile, tile_mod_ref):
        h_ref[...] = _modulated_norm(x_tile, tile_mod_ref, normg_ref)
        _in_proj_vector_part(h_ref, wt_ref, wt_gla_ref, proj_ref)
        _mixer_prep(layer, proj_ref, hk_ref, gcum_ref, lb_logits_ref, w_gk2_ref, b_gk2_ref)

    @pl.when(t_idx == 0)
    def _():
        st_hgrn[...] = jnp.zeros_like(st_hgrn)
        st_ret[...] = jnp.zeros_like(st_ret)
        st_ssd[...] = jnp.zeros_like(st_ssd)
        st_gla[...] = jnp.zeros_like(st_gla)
        conv_ref[...] = jnp.zeros_like(conv_ref)

    @pl.when((t_idx == 0) & (b_idx == 0))
    def _():
        project_vector_part(x_ref[...], mod_ref)
        _init_retention_decays(ret_decay_ref, ret_edge_ref)

    _in_proj_scalar_part(h_ref, wt_ref, proj_ref)
    _vector_decay_mixers(proj_ref, y_ref, hk_ref, gcum_ref, cum_sc, k_sc, st_hgrn, st_gla,
                         hgrn_g_ref, gla_g_ref)
    project_vector_part(xn_ref[...], modn_ref)
    _scalar_decay_mixers(proj_ref, y_ref, conv_ref, st_ret, st_ssd, ret_decay_ref, ret_edge_ref,
                         cos_ref, sin_ref,
                         ret_g_ref, conv_w_ref, conv_b_ref, dt_bias_ref, a_log_ref, d_skip_ref,
                         ssm_g_ref)
    gate = mod_ref[0:1, 2 * D_MODEL:3 * D_MODEL]
    x_new = x_ref[...] + gate * _dot(y_ref[...], w_out_ref[...])
    if final:
        x_new = _rms(x_new, D_MODEL) * final_g_ref[...]
    o_ref[...] = x_new


def _resident(shape):
    nd = len(shape)
    return pl.BlockSpec(shape, lambda b, t, _nd=nd: (0,) * _nd, pipeline_mode=pl.Buffered(1))


def _layer_call(layer, final, x, mod, params):
    batch, seq, _ = x.shape
    n_tiles = seq // TILE
    n_chunks = TILE // CHUNK

    def next_tile(t):
        return jnp.where(t == n_tiles - 1, 0, t + 1)

    def next_batch(b, t):
        return jnp.where(t == n_tiles - 1, jnp.minimum(b + 1, batch - 1), b)

    small = [params[k] for k in (
        "lb_logits", "hgrn_g", "ret_g", "conv_w", "conv_b", "dt_bias", "a_log", "d_skip",
        "ssm_g", "w_gk2", "b_gk2", "gla_g", "final_g")]
    in_specs = [
        pl.BlockSpec((None, TILE, D_MODEL), lambda b, t: (b, t, 0)),
        pl.BlockSpec((None, TILE, D_MODEL), lambda b, t: (next_batch(b, t), next_tile(t), 0)),
        pl.BlockSpec((None, SUBLANES, 3 * D_MODEL), lambda b, t: (b, 0, 0)),
        pl.BlockSpec((None, SUBLANES, 3 * D_MODEL), lambda b, t: (next_batch(b, t), 0, 0)),
        _resident((1, D_MODEL)),
        pl.BlockSpec((None, ORIG_W, D_MODEL), lambda b, t: (layer, 0, 0), pipeline_mode=pl.Buffered(1)),
        pl.BlockSpec((None, GLA_TAIL_ROWS, D_MODEL), lambda b, t: (layer, 0, 0),
                     pipeline_mode=pl.Buffered(1)),
        pl.BlockSpec((None, D_INNER, D_MODEL), lambda b, t: (layer, 0, 0), pipeline_mode=pl.Buffered(1)),
        pl.BlockSpec((TILE, HEAD_W), lambda b, t: (t, 0)),
        pl.BlockSpec((TILE, HEAD_W), lambda b, t: (t, 0)),
    ] + [_resident(a.shape) for a in small]
    scratch = [
        pltpu.VMEM((TILE, PROJ_W), F32),
        pltpu.VMEM((TILE, D_MODEL), BF16),
        pltpu.VMEM((TILE, D_INNER), BF16),
        pltpu.VMEM((SUBLANES, SSM_CONV_CH), F32),
        pltpu.VMEM((TILE, BRANCH_W), F32),
        pltpu.VMEM((TILE, GLA_KEY_W), F32),
        pltpu.VMEM((n_chunks, N_KEY_GROUPS, CHUNK, LANES), F32),
        pltpu.VMEM((n_chunks, N_KEY_GROUPS, CHUNK, LANES), F32),
        pltpu.VMEM((N_HEADS, HEAD_W, HEAD_W), F32),
        pltpu.VMEM((N_HEADS, HEAD_W, HEAD_W), F32),
        pltpu.VMEM((SSM_GROUPS, SSM_STATE, BRANCH_W // SSM_GROUPS), F32),
        pltpu.VMEM((N_HEADS // 2, HEAD_W, LANES), F32),
        pltpu.VMEM((N_HEADS, TILE, TILE), F32),
        pltpu.VMEM((N_HEADS, 2, TILE, HEAD_W), F32),
    ]
    return pl.pallas_call(
        functools.partial(_layer_kernel, layer, final),
        grid=(batch, n_tiles),
        in_specs=in_specs,
        out_specs=pl.BlockSpec((None, TILE, D_MODEL), lambda b, t: (b, t, 0)),
        out_shape=jax.ShapeDtypeStruct(x.shape, F32),
        scratch_shapes=scratch,
        compiler_params=pltpu.CompilerParams(
            dimension_semantics=("arbitrary", "arbitrary"),
            vmem_limit_bytes=VMEM_LIMIT_BYTES),
        name=f"mixer_layer{layer}",
    )(x, x, mod, mod, params["norm_g"], params["wt"], params["wt_gla"], params["w_out"], params["cos"], params["sin"], *small)


def _pad_lanes(v, lane0=0):
    return jnp.zeros((1, LANES), F32).at[0, lane0:lane0 + v.shape[0]].set(v)


def kernel(x, c, w_ada, b_ada, norm_g, w_in, hgrn_lb_logits, hgrn_onorm_g, ret_onorm_g, ssm_conv_w,
           ssm_conv_b, ssm_dt_bias, ssm_a_log, ssm_d, ssm_norm_g, gla_w_gk2, gla_b_gk2, gla_onorm_g,
           w_out, final_g):
    batch, seq, _ = x.shape
    c_pad = jnp.zeros((SUBLANES, D_MODEL), F32).at[:batch].set(c)
    mod_all = _ada_mod(c_pad, w_ada, b_ada)

    inv_freq = ROPE_BASE ** (-np.arange(0, HEAD_W, 2, dtype=np.float64) / HEAD_W)
    ang = np.arange(seq, dtype=np.float64)[:, None] * inv_freq[None, :]
    cos_tab = jnp.asarray(np.concatenate([np.cos(ang), np.cos(ang)], axis=-1), F32)
    sin_tab = jnp.asarray(np.concatenate([-np.sin(ang), np.sin(ang)], axis=-1), F32)

    wt = jnp.swapaxes(w_in, 1, 2).astype(BF16)
    wt_gla = wt[:, ORIG_GQ:, :]
    w_out_bf = w_out.astype(BF16)

    for layer in range(DEPTH):
        w_gk2_p = jnp.zeros((LANES, GLA_KEY_W), F32).at[LR_LANE0:].set(gla_w_gk2[layer])
        params = dict(
            norm_g=norm_g[layer][None, :], wt=wt, wt_gla=wt_gla, w_out=w_out_bf,
            cos=cos_tab, sin=sin_tab,
            lb_logits=hgrn_lb_logits, hgrn_g=hgrn_onorm_g[layer][None, :],
            ret_g=ret_onorm_g[layer][None, :], conv_w=ssm_conv_w[layer],
            conv_b=ssm_conv_b[layer][None, :], dt_bias=_pad_lanes(ssm_dt_bias[layer]),
            a_log=_pad_lanes(ssm_a_log[layer]), d_skip=_pad_lanes(ssm_d[layer]),
            ssm_g=ssm_norm_g[layer][None, :], w_gk2=w_gk2_p, b_gk2=gla_b_gk2[layer][None, :],
            gla_g=gla_onorm_g[layer][None, :], final_g=final_g[None, :])
        mod = jnp.broadcast_to(mod_all[layer][:batch, None, :], (batch, SUBLANES, 3 * D_MODEL))
        x = _layer_call(layer, layer == DEPTH - 1, x, mod, params)
    return x
```

```python
import functools
import math

import numpy as np
import jax
import jax.numpy as jnp
from jax import lax
from jax.experimental import pallas as pl
from jax.experimental.pallas import tpu as pltpu

F32 = jnp.float32
BF16 = jnp.bfloat16

D_MODEL = 1024
DEPTH = 2
BRANCH_W = 512
D_INNER = 4 * BRANCH_W
HEAD_W = 128
N_HEADS = 4
RET_DECAY_EXP0 = 5.0
ROPE_BASE = 10000.0
SSM_HEADS = 8
SSM_HEAD_DIM = 64
SSM_GROUPS = 2
SSM_STATE = 128
SSM_CONV = 4
SSM_CONV_CH = BRANCH_W + 2 * SSM_GROUPS * SSM_STATE
GLA_KEY_W = 256
GLA_LOWRANK = 16
GLA_GATE_TEMP = 16.0
EPS = 1e-6
LOG2_E = math.log2(math.e)

LANES = 128
SUBLANES = 8

C_AQ, C_AF, C_AI, C_AG = 0, 512, 1024, 1536
C_GQ, C_GK, C_GV, C_GG = 2048, 2304, 2560, 3072
C_LR = 3584
VEC_W = C_LR + LANES
C_RQ, C_RK, C_RV, C_RG = 3712, 4224, 4736, 5248
C_MZ, C_XBC = 5760, 6272
C_DT = 7296
PROJ_W = C_DT + LANES
ORIG_RQ = 2048
ORIG_DT = 5632
ORIG_GQ = ORIG_DT + SSM_HEADS
ORIG_LR = ORIG_GQ + 2 * GLA_KEY_W + 2 * BRANCH_W
ORIG_W = ORIG_LR + GLA_LOWRANK
GLA_TAIL_ROWS = ORIG_W - ORIG_GQ
LR_LANE0 = LANES - GLA_LOWRANK

TILE = 256
CHUNK = 64
SUB = 8
N_SUB = CHUNK // SUB
N_KEY_GROUPS = N_HEADS + N_HEADS // 2
VMEM_LIMIT_BYTES = 56 * 1024 * 1024


def _dot(a, b):
    return jnp.dot(a, b, preferred_element_type=F32)


def _dot_nt(a, b):
    return lax.dot_general(a, b, (((1,), (1,)), ((), ())), preferred_element_type=F32)


def _dot_tn(a, b):
    return lax.dot_general(a, b, (((0,), (0,)), ((), ())), preferred_element_type=F32)


def _dot3(m_bf16, x):
    hi = x.astype(BF16)
    r1 = x - hi.astype(F32)
    mid = r1.astype(BF16)
    lo = (r1 - mid.astype(F32)).astype(BF16)
    return _dot(m_bf16, hi) + _dot(m_bf16, mid) + _dot(m_bf16, lo)


def _dot3_rhs(x, m_bf16):
    hi = x.astype(BF16)
    r1 = x - hi.astype(F32)
    mid = r1.astype(BF16)
    lo = (r1 - mid.astype(F32)).astype(BF16)
    return _dot(hi, m_bf16) + _dot(mid, m_bf16) + _dot(lo, m_bf16)


def _silu(x):
    hx = 0.5 * x
    return hx + hx * jnp.tanh(hx)


def _log1p_exp_neg_abs(x):
    return jnp.log(1.0 + jnp.exp(-jnp.abs(x)))


def _log_sigmoid(x):
    return jnp.minimum(x, 0.0) - _log1p_exp_neg_abs(x)


def _softplus(x):
    return jnp.maximum(x, 0.0) + _log1p_exp_neg_abs(x)


def _logaddexp(a, b):
    return jnp.maximum(a, b) + _log1p_exp_neg_abs(a - b)


def _rms(x, width):
    return x * lax.rsqrt(jnp.sum(x * x, axis=-1, keepdims=True) * (1.0 / width) + EPS)


def _iota(shape, dim):
    return lax.broadcasted_iota(jnp.int32, shape, dim)


def _div_pow2(x, d):
    shift = d.bit_length() - 1
    assert 1 << shift == d
    return lax.shift_right_logical(x, shift)


def _lower_tri(n, block):
    r = _iota((n, n), 0)
    c = _iota((n, n), 1)
    keep = (c <= r) & (_div_pow2(r, block) == _div_pow2(c, block))
    return jnp.where(keep, 1.0, 0.0).astype(BF16)


def _ada_kernel(c_ref, w_ref, b_ref, o_ref):
    c_act = _silu(c_ref[...]).astype(BF16)
    o_ref[...] = _dot(c_act, w_ref[...].astype(BF16)) + b_ref[...]


def _ada_mod(c_pad, w_ada, b_ada):
    n_blk = 3
    return pl.pallas_call(
        _ada_kernel,
        grid=(DEPTH, n_blk),
        in_specs=[
            pl.BlockSpec((SUBLANES, D_MODEL), lambda l, j: (0, 0)),
            pl.BlockSpec((None, D_MODEL, D_MODEL), lambda l, j: (l, 0, j)),
            pl.BlockSpec((None, 1, D_MODEL), lambda l, j: (l, 0, j)),
        ],
        out_specs=pl.BlockSpec((None, SUBLANES, D_MODEL), lambda l, j: (l, 0, j)),
        out_shape=jax.ShapeDtypeStruct((DEPTH, SUBLANES, 3 * D_MODEL), F32),
        name="ada_mod",
    )(c_pad, w_ada, b_ada.reshape(DEPTH, 1, 3 * D_MODEL))


HEADS_PER_MATMUL = 2
N_OFF_USED = SUB * (N_SUB * (N_SUB - 1) // 2)
N_OFF = -(-N_OFF_USED // LANES) * LANES


def _key_group_terms(q, k, cum, cum_sc, k_sc, heads_per_group):
    lane_w = LANES // heads_per_group

    def head_lanes(x, i):
        if heads_per_group == 1:
            return x
        lane = _iota(x.shape, 1)
        return jnp.where((lane >= i * lane_w) & (lane < (i + 1) * lane_w), x, 0.0)

    cum_sc[...] = cum
    k_sc[...] = k
    cum_last = cum_sc[CHUNK - 1:CHUNK, :]
    q_exp = q * jnp.exp2(cum)
    k_end = k * jnp.exp2(cum_last - cum)
    dec = jnp.exp2(cum_last)

    qs_parts = [jnp.zeros((SUB, LANES), F32)]
    ks_parts = []
    for i in range(1, N_SUB):
        ref_i = cum_sc[i * SUB - 1:i * SUB, :]
        qs_parts.append(q[i * SUB:(i + 1) * SUB, :] * jnp.exp2(cum[i * SUB:(i + 1) * SUB, :] - ref_i))
        ks_parts.append(k[:i * SUB, :] * jnp.exp2(ref_i - cum[:i * SUB, :]))
    ks_parts.append(jnp.zeros((N_OFF - N_OFF_USED, LANES), F32))
    qs = jnp.concatenate(qs_parts, axis=0)
    ks = jnp.concatenate(ks_parts, axis=0).astype(BF16)

    lane_c = _iota((SUB, CHUNK), 1)
    row_c = _iota((SUB, CHUNK), 0)
    diag_parts = [[] for _ in range(heads_per_group)]
    for i in range(N_SUB):
        acc = [jnp.zeros((SUB, CHUNK), F32) for _ in range(heads_per_group)]
        q_i = q[i * SUB:(i + 1) * SUB, :]
        cum_i = cum[i * SUB:(i + 1) * SUB, :]
        for j in range(SUB):
            s = i * SUB + j
            w = q_i * jnp.exp2(cum_i - cum_sc[s:s + 1, :]) * k_sc[s:s + 1, :]
            for hd in range(heads_per_group):
                a_col = jnp.sum(head_lanes(w, hd), axis=-1, keepdims=True)
                acc[hd] = jnp.where(lane_c == s, a_col, acc[hd])
        causal = (row_c + i * SUB) >= lane_c
        for hd in range(heads_per_group):
            diag_parts[hd].append(jnp.where(causal, acc[hd], 0.0))
    a_diag = [jnp.concatenate(p, axis=0) for p in diag_parts]
    return dict(
        qs=[head_lanes(qs, i).astype(BF16) for i in range(heads_per_group)],
        q_exp=[head_lanes(q_exp, i).astype(BF16) for i in range(heads_per_group)],
        k_end=[head_lanes(k_end, i).astype(BF16) for i in range(heads_per_group)],
        ks=ks, dec=dec, a_diag=a_diag)


def _vector_decay_chunk(groups, v_all, heads_per_group, st_ref):
    n_groups = len(groups)
    n_heads = n_groups * heads_per_group
    rows_all = n_heads * CHUNK
    zeros_bf = jnp.zeros((CHUNK, LANES), BF16)

    lhs_rows = []
    for h in range(n_heads):
        g, i = divmod(h, heads_per_group)
        lhs_rows.append(jnp.concatenate(
            [groups[g]["qs"][i] if gg == g else zeros_bf for gg in range(n_groups)], axis=1))
    ks_cat = jnp.concatenate([grp["ks"] for grp in groups], axis=1)
    p_all = _dot_nt(jnp.concatenate(lhs_rows, axis=0), ks_cat)
    row_blk = _div_pow2(_iota((rows_all, N_OFF), 0) & (CHUNK - 1), SUB)
    col = _iota((rows_all, N_OFF), 1)
    lo = lax.shift_right_logical(row_blk * (row_blk - 1), 1) * SUB
    p_all = jnp.where((col >= lo) & (col < lo + row_blk * SUB), p_all, 0.0)

    a_all = jnp.concatenate([a for grp in groups for a in grp["a_diag"]], axis=0)
    a_all = jnp.concatenate([a_all, jnp.zeros((rows_all, LANES - CHUNK), F32)], axis=1)
    scores = jnp.concatenate([p_all, a_all], axis=1).astype(BF16)
    width = v_all.shape[1]
    v_rows = jnp.concatenate(
        [v_all[:i * SUB, :] for i in range(1, N_SUB)]
        + [jnp.zeros((N_OFF - N_OFF_USED, width), F32), v_all, jnp.zeros((LANES - CHUNK, width), F32)],
        axis=0).astype(BF16)
    r = _dot(scores, v_rows)
    outs = [r[h * CHUNK:(h + 1) * CHUNK, h * HEAD_W:(h + 1) * HEAD_W] for h in range(n_heads)]

    v_bf = v_all.astype(BF16)
    if heads_per_group == 1:
        st = st_ref[...]
        st_bf = st.reshape(n_heads * HEAD_W, LANES).astype(BF16)
        qe_all = jnp.concatenate([grp["q_exp"][0] for grp in groups], axis=0)
        inter = _dot_nt(qe_all, st_bf)
        ke_cat = jnp.concatenate([grp["k_end"][0] for grp in groups], axis=1)
        upd = _dot_tn(v_bf, ke_cat)
        for h in range(n_heads):
            outs[h] = outs[h] + inter[h * CHUNK:(h + 1) * CHUNK, h * HEAD_W:(h + 1) * HEAD_W]
            st_ref[h] = st[h] * groups[h]["dec"] + upd[h * HEAD_W:(h + 1) * HEAD_W, h * LANES:(h + 1) * LANES]
    else:
        for g, grp in enumerate(groups):
            st = st_ref[g]
            qe_rows = jnp.concatenate(grp["q_exp"], axis=0)
            inter = _dot_nt(qe_rows, st.astype(BF16))
            ke_rows = jnp.concatenate(grp["k_end"], axis=0)
            v_g = [v_bf[:, (g * heads_per_group + i) * HEAD_W:(g * heads_per_group + i + 1) * HEAD_W]
                   for i in range(heads_per_group)]
            upd = _dot_tn(jnp.concatenate(v_g, axis=0), ke_rows)
            for i in range(heads_per_group):
                h = g * heads_per_group + i
                outs[h] = outs[h] + inter[i * CHUNK:(i + 1) * CHUNK, :]
            st_ref[g] = st * grp["dec"] + upd
    return outs


def _modulated_norm(x, mod_ref, normg_ref):
    shift = mod_ref[0:1, 0:D_MODEL]
    scale = mod_ref[0:1, D_MODEL:2 * D_MODEL]
    h = (_rms(x, D_MODEL) * normg_ref[...]) * (1.0 + scale) + shift
    return h.astype(BF16)


def _in_proj(h_ref, wt_ref, proj_ref, dst, row0, row1):
    n_blk = 1024
    for r0 in range(row0, row1, n_blk):
        r1 = min(r0 + n_blk, row1)
        proj_ref[:, dst + r0 - row0:dst + r1 - row0] = _dot_nt(h_ref[...], wt_ref[r0:r1, :])


def _in_proj_vector_part(h_ref, wt_ref, wt_gla_ref, proj_ref):
    _in_proj(h_ref, wt_ref, proj_ref, C_AQ, 0, ORIG_RQ)
    _in_proj(h_ref, wt_gla_ref, proj_ref, C_GQ, 0, ORIG_LR - ORIG_GQ)
    _in_proj(h_ref, wt_gla_ref, proj_ref, C_LR, GLA_TAIL_ROWS - LANES, GLA_TAIL_ROWS)


def _in_proj_scalar_part(h_ref, wt_ref, proj_ref):
    _in_proj(h_ref, wt_ref, proj_ref, C_RQ, ORIG_RQ, ORIG_DT)
    _in_proj(h_ref, wt_ref, proj_ref, C_DT, ORIG_DT, ORIG_DT + LANES)


def _mixer_prep(layer, proj_ref, hk_ref, gcum_ref, lb_logits_ref, w_gk2_ref, b_gk2_ref):
    tri_chunk = _lower_tri(TILE, CHUNK)

    lg = [lb_logits_ref[i:i + 1, :] for i in range(DEPTH)]
    lg_max = functools.reduce(jnp.maximum, lg)
    lg_exp = [jnp.exp(r - lg_max) for r in lg]
    lg_den = functools.reduce(lambda a, b: a + b, lg_exp)
    lower = jnp.zeros((1, BRANCH_W), F32)
    for i in range(1, layer + 1):
        lower = lower + lg_exp[i] / lg_den
    log_lb = jnp.log(lower)
    log1m_lb = jnp.log1p(-lower)
    proj_ref[:, C_AQ:C_AQ + BRANCH_W] = _silu(proj_ref[:, C_AQ:C_AQ + BRANCH_W])
    log_f = _logaddexp(log_lb, log1m_lb + _log_sigmoid(proj_ref[:, C_AF:C_AF + BRANCH_W]))
    hk_ref[...] = 1.0 - jnp.exp(log_f)
    proj_ref[:, C_AF:C_AF + BRANCH_W] = _dot3(tri_chunk, log_f * LOG2_E)

    low_rank = proj_ref[:, C_LR:C_LR + LANES]
    gk_gate = _dot(low_rank.astype(BF16), w_gk2_ref[...].astype(BF16)) + b_gk2_ref[...]
    gcum_ref[...] = _dot3(tri_chunk, _log_sigmoid(gk_gate) * (LOG2_E / GLA_GATE_TEMP))
    gla_dk = GLA_KEY_W // N_HEADS
    proj_ref[:, C_GQ:C_GQ + GLA_KEY_W] = proj_ref[:, C_GQ:C_GQ + GLA_KEY_W] * (gla_dk ** -0.5)


def _vector_decay_mixers(proj_ref, y_ref, hk_ref, gcum_ref, cum_sc, k_sc, st_hgrn, st_gla,
                         hgrn_g_ref, gla_g_ref):
    def chunk_body(ci):
        r0 = ci * CHUNK
        rows = slice(r0, r0 + CHUNK)
        groups = []
        for hd in range(N_HEADS):
            cs = slice(hd * HEAD_W, (hd + 1) * HEAD_W)
            groups.append(_key_group_terms(
                proj_ref[rows, C_AQ + cs.start:C_AQ + cs.stop], hk_ref[rows, cs],
                proj_ref[rows, C_AF + cs.start:C_AF + cs.stop],
                cum_sc.at[ci, hd], k_sc.at[ci, hd], 1))
        outs = []
        for p0 in range(0, N_HEADS, HEADS_PER_MATMUL):
            outs += _vector_decay_chunk(
                groups[p0:p0 + HEADS_PER_MATMUL],
                proj_ref[rows, C_AI + p0 * HEAD_W:C_AI + (p0 + HEADS_PER_MATMUL) * HEAD_W],
                1, st_hgrn.at[pl.ds(p0, HEADS_PER_MATMUL)])
        for hd, o in enumerate(outs):
            cs = slice(hd * HEAD_W, (hd + 1) * HEAD_W)
            g_act = _silu(proj_ref[rows, C_AG + cs.start:C_AG + cs.stop])
            y_ref[rows, cs] = (_rms(o, HEAD_W) * hgrn_g_ref[:, cs] * g_act).astype(BF16)
        groups = []
        for pr in range(N_HEADS // 2):
            ks_ = slice(pr * LANES, (pr + 1) * LANES)
            groups.append(_key_group_terms(
                proj_ref[rows, C_GQ + ks_.start:C_GQ + ks_.stop],
                proj_ref[rows, C_GK + ks_.start:C_GK + ks_.stop],
                gcum_ref[rows, ks_],
                cum_sc.at[ci, N_HEADS + pr], k_sc.at[ci, N_HEADS + pr], 2))
        outs = []
        for pr in range(N_HEADS // 2):
            outs += _vector_decay_chunk(
                groups[pr:pr + 1],
                proj_ref[rows, C_GV + 2 * pr * HEAD_W:C_GV + 2 * (pr + 1) * HEAD_W],
                2, st_gla.at[pl.ds(pr, 1)])
        for hd, o in enumerate(outs):
            cs = slice(hd * HEAD_W, (hd + 1) * HEAD_W)
            g_act = _silu(proj_ref[rows, C_GG + cs.start:C_GG + cs.stop])
            y_ref[rows, 3 * BRANCH_W + cs.start:3 * BRANCH_W + cs.stop] = (
                _rms(o, HEAD_W) * gla_g_ref[:, cs] * g_act).astype(BF16)

    for ci in range(TILE // CHUNK):
        chunk_body(ci)


def _ret_log2_gamma(hd):
    return math.log1p(-(2.0 ** -(RET_DECAY_EXP0 + hd))) * LOG2_E


def _init_retention_decays(ret_decay_ref, ret_edge_ref):
    t_col = _iota((TILE, TILE), 0)
    s_row = _iota((TILE, TILE), 1)
    t_minus_s = (t_col - s_row).astype(F32)
    t_plus1 = (_iota((TILE, HEAD_W), 0) + 1).astype(F32)
    s_to_end = (TILE - 1 - _iota((TILE, HEAD_W), 0)).astype(F32)
    for hd in range(N_HEADS):
        log2_gamma = _ret_log2_gamma(hd)
        ret_decay_ref[hd] = jnp.where(t_col >= s_row, jnp.exp2(t_minus_s * log2_gamma), 0.0)
        ret_edge_ref[hd, 0] = jnp.exp2(t_plus1 * log2_gamma)
        ret_edge_ref[hd, 1] = jnp.exp2(s_to_end * log2_gamma)


def _scalar_decay_mixers(proj_ref, y_ref, conv_ref, st_ret, st_ssd, ret_decay_ref, ret_edge_ref,
                         cos_ref, sin_ref,
                         ret_g_ref, conv_w_ref, conv_b_ref, dt_bias_ref, a_log_ref, d_skip_ref,
                         ssm_g_ref):
    tri_tile = _lower_tri(TILE, TILE)
    def cols(c0, width):
        return proj_ref[:, c0:c0 + width]

    tail = cols(C_DT, LANES)

    cos_t = cos_ref[...]
    sin_t = sin_ref[...]
    hpm = HEADS_PER_MATMUL
    for p0 in range(0, N_HEADS, hpm):
        q_bfs, w_bfs, k_ends, st_decays = [], [], [], []
        for hd in range(p0, p0 + hpm):
            cs = slice(hd * HEAD_W, (hd + 1) * HEAD_W)
            rq = cols(C_RQ + cs.start, HEAD_W)
            rk = cols(C_RK + cs.start, HEAD_W)
            q = rq * cos_t + pltpu.roll(rq, HEAD_W // 2, 1) * sin_t
            k = (rk * cos_t + pltpu.roll(rk, HEAD_W // 2, 1) * sin_t) * (HEAD_W ** -0.5)
            q_bfs.append(q.astype(BF16))
            w_bfs.append((_dot_nt(q_bfs[-1], k.astype(BF16)) * ret_decay_ref[hd]).astype(BF16))
            k_ends.append((k * ret_edge_ref[hd, 1]).astype(BF16))
            st_decays.append(jnp.full((1, HEAD_W), 2.0 ** (TILE * _ret_log2_gamma(hd)), F32))
        pair = p0 // hpm
        v_bf = cols(C_RV + p0 * HEAD_W, hpm * HEAD_W).astype(BF16)
        st = st_ret[pair]
        intra = _dot(jnp.concatenate(w_bfs, axis=0), v_bf)
        inter = _dot(jnp.concatenate(q_bfs, axis=0), st.astype(BF16))
        upd = _dot_tn(jnp.concatenate(k_ends, axis=1), v_bf)
        st_ret[pair] = (st * jnp.concatenate(st_decays, axis=1)
                        + jnp.concatenate([upd[i * HEAD_W:(i + 1) * HEAD_W, i * HEAD_W:(i + 1) * HEAD_W]
                                           for i in range(hpm)], axis=1))
        for i in range(hpm):
            hd = p0 + i
            cs = slice(hd * HEAD_W, (hd + 1) * HEAD_W)
            blk = (slice(i * TILE, (i + 1) * TILE), slice(i * HEAD_W, (i + 1) * HEAD_W))
            o = intra[blk] + ret_edge_ref[hd, 0] * inter[blk]
            g_act = _silu(cols(C_RG + cs.start, HEAD_W))
            y_ref[:, BRANCH_W + cs.start:BRANCH_W + cs.stop] = (
                _rms(o, HEAD_W) * ret_g_ref[:, cs] * g_act).astype(BF16)

    u = cols(C_XBC, SSM_CONV_CH)
    prev = conv_ref[...]
    row8 = _iota((SUBLANES, SSM_CONV_CH), 0)
    xbc = conv_b_ref[...] + conv_w_ref[SSM_CONV - 1:SSM_CONV, :] * u
    for d in range(1, SSM_CONV):
        u_d = pltpu.roll(u, d, 0)
        head = jnp.where(row8 < d, pltpu.roll(prev, d, 0), u_d[0:SUBLANES, :])
        u_d = jnp.concatenate([head, u_d[SUBLANES:, :]], axis=0)
        xbc = xbc + conv_w_ref[SSM_CONV - 1 - d:SSM_CONV - d, :] * u_d
    conv_ref[...] = u[TILE - SUBLANES:TILE, :]
    xbc = _silu(xbc)
    xs = xbc[:, 0:BRANCH_W]
    dt = _softplus(tail + dt_bias_ref[...])
    log_a = dt * (-LOG2_E * jnp.exp(a_log_ref[...]))
    cum = _dot3(tri_tile, log_a)
    cum_t = cum.T

    e_r = _iota((LANES, BRANCH_W), 0)
    e_c = _iota((LANES, BRANCH_W), 1)
    expand = jnp.where(e_r == _div_pow2(e_c, SSM_HEAD_DIM), 1.0, 0.0).astype(BF16)
    dt_e = _dot(dt.astype(BF16), expand)
    cum_e = _dot3_rhs(cum, expand)
    cum_last_e = cum_e[TILE - 1:TILE, :]
    d_e = _dot3_rhs(jnp.broadcast_to(d_skip_ref[...], (SUBLANES, LANES)), expand)[0:1, :]
    xdt = xs * dt_e
    x_end = (xdt * jnp.exp2(cum_last_e - cum_e)).astype(BF16)
    half = TILE // 2
    tri_half = _iota((half, half), 0) >= _iota((half, half), 1)
    hpg = SSM_HEADS // SSM_GROUPS
    gw = hpg * SSM_HEAD_DIM
    o_parts = []
    for g in range(SSM_GROUPS):
        b_bf = xbc[:, BRANCH_W + g * SSM_STATE:BRANCH_W + (g + 1) * SSM_STATE].astype(BF16)
        c_bf = xbc[:, BRANCH_W + (SSM_GROUPS + g) * SSM_STATE:
                   BRANCH_W + (SSM_GROUPS + g + 1) * SSM_STATE].astype(BF16)
        cb = _dot_nt(c_bf, b_bf)
        st = st_ssd[g]
        inter = _dot(c_bf, st.astype(BF16)) * jnp.exp2(cum_e[:, g * gw:(g + 1) * gw])
        w_rows = []
        for hh in range(hpg):
            hd = g * hpg + hh
            quad = {}
            for qi in range(2):
                for qj in range(qi + 1):
                    rq_, cq_ = slice(qi * half, (qi + 1) * half), slice(qj * half, (qj + 1) * half)
                    rel = cum[rq_, hd:hd + 1] - cum_t[hd:hd + 1, cq_]
                    w_q = cb[rq_, cq_] * jnp.exp2(rel)
                    quad[qi, qj] = (jnp.where(tri_half, w_q, 0.0) if qi == qj else w_q).astype(BF16)
            w_rows += [jnp.concatenate([quad[0, 0], jnp.zeros((half, half), BF16)], axis=1),
                       jnp.concatenate([quad[1, 0], quad[1, 1]], axis=1)]
        intra = _dot(jnp.concatenate(w_rows, axis=0), xdt[:, g * gw:(g + 1) * gw].astype(BF16))
        for hh in range(hpg):
            lanes = slice(hh * SSM_HEAD_DIM, (hh + 1) * SSM_HEAD_DIM)
            o_parts.append(intra[hh * TILE:(hh + 1) * TILE, lanes] + inter[:, lanes])
        st_ssd[g] = (st * jnp.exp2(cum_last_e[:, g * gw:(g + 1) * gw])
                     + _dot_tn(b_bf, x_end[:, g * gw:(g + 1) * gw]))
    o_ssd = jnp.concatenate(o_parts, axis=1)
    y_ssd = (o_ssd + d_e * xs) * _silu(cols(C_MZ, BRANCH_W))
    gn = BRANCH_W // SSM_GROUPS
    for g in range(SSM_GROUPS):
        cs = slice(g * gn, (g + 1) * gn)
        y_ref[:, 2 * BRANCH_W + cs.start:2 * BRANCH_W + cs.stop] = (
            _rms(y_ssd[:, cs], gn) * ssm_g_ref[:, cs]).astype(BF16)


def _layer_kernel(layer, final,
                  x_ref, xn_ref, mod_ref, modn_ref, normg_ref, wt_ref, wt_gla_ref, w_out_ref,
                  cos_ref, sin_ref,
                  lb_logits_ref, hgrn_g_ref, ret_g_ref, conv_w_ref, conv_b_ref, dt_bias_ref,
                  a_log_ref, d_skip_ref, ssm_g_ref, w_gk2_ref, b_gk2_ref, gla_g_ref, final_g_ref,
                  o_ref,
                  proj_ref, h_ref, y_ref, conv_ref, hk_ref, gcum_ref, cum_sc, k_sc,
                  st_hgrn, st_ret, st_ssd, st_gla, ret_decay_ref, ret_edge_ref):
    b_idx = pl.program_id(0)
    t_idx = pl.program_id(1)

    def project_vector_part(x_tile, tile_mod_ref):
        h_ref[...] = _modulated_norm(x_tile, tile_mod_ref, normg_ref)
        _in_proj_vector_part(h_ref, wt_ref, wt_gla_ref, proj_ref)
        _mixer_prep(layer, proj_ref, hk_ref, gcum_ref, lb_logits_ref, w_gk2_ref, b_gk2_ref)

    @pl.when(t_idx == 0)
    def _():
        st_hgrn[...] = jnp.zeros_like(st_hgrn)
        st_ret[...] = jnp.zeros_like(st_ret)
        st_ssd[...] = jnp.zeros_like(st_ssd)
        st_gla[...] = jnp.zeros_like(st_gla)
        conv_ref[...] = jnp.zeros_like(conv_ref)

    @pl.when((t_idx == 0) & (b_idx == 0))
    def _():
        project_vector_part(x_ref[...], mod_ref)
        _init_retention_decays(ret_decay_ref, ret_edge_ref)

    _in_proj_scalar_part(h_ref, wt_ref, proj_ref)
    _vector_decay_mixers(proj_ref, y_ref, hk_ref, gcum_ref, cum_sc, k_sc, st_hgrn, st_gla,
                         hgrn_g_ref, gla_g_ref)
    project_vector_part(xn_ref[...], modn_ref)
    _scalar_decay_mixers(proj_ref, y_ref, conv_ref, st_ret, st_ssd, ret_decay_ref, ret_edge_ref,
                         cos_ref, sin_ref,
                         ret_g_ref, conv_w_ref, conv_b_ref, dt_bias_ref, a_log_ref, d_skip_ref,
                         ssm_g_ref)
    gate = mod_ref[0:1, 2 * D_MODEL:3 * D_MODEL]
    x_new = x_ref[...] + gate * _dot(y_ref[...], w_out_ref[...])
    if final:
        x_new = _rms(x_new, D_MODEL) * final_g_ref[...]
    o_ref[...] = x_new


def _resident(shape):
    nd = len(shape)
    return pl.BlockSpec(shape, lambda b, t, _nd=nd: (0,) * _nd, pipeline_mode=pl.Buffered(1))


def _layer_call(layer, final, x, mod, params):
    batch, seq, _ = x.shape
    n_tiles = seq // TILE
    n_chunks = TILE // CHUNK

    def next_tile(t):
        return jnp.where(t == n_tiles - 1, 0, t + 1)

    def next_batch(b, t):
        return jnp.where(t == n_tiles - 1, jnp.minimum(b + 1, batch - 1), b)

    small = [params[k] for k in (
        "lb_logits", "hgrn_g", "ret_g", "conv_w", "conv_b", "dt_bias", "a_log", "d_skip",
        "ssm_g", "w_gk2", "b_gk2", "gla_g", "final_g")]
    in_specs = [
        pl.BlockSpec((None, TILE, D_MODEL), lambda b, t: (b, t, 0)),
        pl.BlockSpec((None, TILE, D_MODEL), lambda b, t: (next_batch(b, t), next_tile(t), 0)),
        pl.BlockSpec((None, SUBLANES, 3 * D_MODEL), lambda b, t: (b, 0, 0)),
        pl.BlockSpec((None, SUBLANES, 3 * D_MODEL), lambda b, t: (next_batch(b, t), 0, 0)),
        _resident((1, D_MODEL)),
        pl.BlockSpec((None, ORIG_W, D_MODEL), lambda b, t: (layer, 0, 0), pipeline_mode=pl.Buffered(1)),
        pl.BlockSpec((None, GLA_TAIL_ROWS, D_MODEL), lambda b, t: (layer, 0, 0),
                     pipeline_mode=pl.Buffered(1)),
        pl.BlockSpec((None, D_INNER, D_MODEL), lambda b, t: (layer, 0, 0), pipeline_mode=pl.Buffered(1)),
        pl.BlockSpec((TILE, HEAD_W), lambda b, t: (t, 0)),
        pl.BlockSpec((TILE, HEAD_W), lambda b, t: (t, 0)),
    ] + [_resident(a.shape) for a in small]
    scratch = [
        pltpu.VMEM((TILE, PROJ_W), F32),
        pltpu.VMEM((TILE, D_MODEL), BF16),
        pltpu.VMEM((TILE, D_INNER), BF16),
        pltpu.VMEM((SUBLANES, SSM_CONV_CH), F32),
        pltpu.VMEM((TILE, BRANCH_W), F32),
        pltpu.VMEM((TILE, GLA_KEY_W), F32),
        pltpu.VMEM((n_chunks, N_KEY_GROUPS, CHUNK, LANES), F32),
        pltpu.VMEM((n_chunks, N_KEY_GROUPS, CHUNK, LANES), F32),
        pltpu.VMEM((N_HEADS, HEAD_W, HEAD_W), F32),
        pltpu.VMEM((N_HEADS // HEADS_PER_MATMUL, HEAD_W, HEADS_PER_MATMUL * HEAD_W), F32),
        pltpu.VMEM((SSM_GROUPS, SSM_STATE, BRANCH_W // SSM_GROUPS), F32),
        pltpu.VMEM((N_HEADS // 2, HEAD_W, LANES), F32),
        pltpu.VMEM((N_HEADS, TILE, TILE), F32),
        pltpu.VMEM((N_HEADS, 2, TILE, HEAD_W), F32),
    ]
    return pl.pallas_call(
        functools.partial(_layer_kernel, layer, final),
        grid=(batch, n_tiles),
        in_specs=in_specs,
        out_specs=pl.BlockSpec((None, TILE, D_MODEL), lambda b, t: (b, t, 0)),
        out_shape=jax.ShapeDtypeStruct(x.shape, F32),
        scratch_shapes=scratch,
        compiler_params=pltpu.CompilerParams(
            dimension_semantics=("arbitrary", "arbitrary"),
            vmem_limit_bytes=VMEM_LIMIT_BYTES),
        name=f"mixer_layer{layer}",
    )(x, x, mod, mod, params["norm_g"], params["wt"], params["wt_gla"], params["w_out"], params["cos"], params["sin"], *small)


def _pad_lanes(v, lane0=0):
    return jnp.zeros((1, LANES), F32).at[0, lane0:lane0 + v.shape[0]].set(v)


def kernel(x, c, w_ada, b_ada, norm_g, w_in, hgrn_lb_logits, hgrn_onorm_g, ret_onorm_g, ssm_conv_w,
           ssm_conv_b, ssm_dt_bias, ssm_a_log, ssm_d, ssm_norm_g, gla_w_gk2, gla_b_gk2, gla_onorm_g,
           w_out, final_g):
    batch, seq, _ = x.shape
    c_pad = jnp.zeros((SUBLANES, D_MODEL), F32).at[:batch].set(c)
    mod_all = _ada_mod(c_pad, w_ada, b_ada)

    inv_freq = ROPE_BASE ** (-np.arange(0, HEAD_W, 2, dtype=np.float64) / HEAD_W)
    ang = np.arange(seq, dtype=np.float64)[:, None] * inv_freq[None, :]
    cos_tab = jnp.asarray(np.concatenate([np.cos(ang), np.cos(ang)], axis=-1), F32)
    sin_tab = jnp.asarray(np.concatenate([-np.sin(ang), np.sin(ang)], axis=-1), F32)

    wt = jnp.swapaxes(w_in, 1, 2).astype(BF16)
    wt_gla = wt[:, ORIG_GQ:, :]
    w_out_bf = w_out.astype(BF16)

    for layer in range(DEPTH):
        w_gk2_p = jnp.zeros((LANES, GLA_KEY_W), F32).at[LR_LANE0:].set(gla_w_gk2[layer])
        params = dict(
            norm_g=norm_g[layer][None, :], wt=wt, wt_gla=wt_gla, w_out=w_out_bf,
            cos=cos_tab, sin=sin_tab,
            lb_logits=hgrn_lb_logits, hgrn_g=hgrn_onorm_g[layer][None, :],
            ret_g=ret_onorm_g[layer][None, :], conv_w=ssm_conv_w[layer],
            conv_b=ssm_conv_b[layer][None, :], dt_bias=_pad_lanes(ssm_dt_bias[layer]),
            a_log=_pad_lanes(ssm_a_log[layer]), d_skip=_pad_lanes(ssm_d[layer]),
            ssm_g=ssm_norm_g[layer][None, :], w_gk2=w_gk2_p, b_gk2=gla_b_gk2[layer][None, :],
            gla_g=gla_onorm_g[layer][None, :], final_g=final_g[None, :])
        mod = jnp.broadcast_to(mod_all[layer][:batch, None, :], (batch, SUBLANES, 3 * D_MODEL))
        x = _layer_call(layer, layer == DEPTH - 1, x, mod, params)
    return x
```

```python
import functools
import math

import numpy as np
import jax
import jax.numpy as jnp
from jax import lax
from jax.experimental import pallas as pl
from jax.experimental.pallas import tpu as pltpu

F32 = jnp.float32
BF16 = jnp.bfloat16

D_MODEL = 1024
DEPTH = 2
BRANCH_W = 512
D_INNER = 4 * BRANCH_W
HEAD_W = 128
N_HEADS = 4
RET_DECAY_EXP0 = 5.0
ROPE_BASE = 10000.0
SSM_HEADS = 8
SSM_HEAD_DIM = 64
SSM_GROUPS = 2
SSM_STATE = 128
SSM_CONV = 4
SSM_CONV_CH = BRANCH_W + 2 * SSM_GROUPS * SSM_STATE
GLA_KEY_W = 256
GLA_LOWRANK = 16
GLA_GATE_TEMP = 16.0
EPS = 1e-6
LOG2_E = math.log2(math.e)

LANES = 128
SUBLANES = 8

C_AQ, C_AF, C_AI, C_AG = 0, 512, 1024, 1536
C_GQ, C_GK, C_GV, C_GG = 2048, 2304, 2560, 3072
C_LR = 3584
VEC_W = C_LR + LANES
C_RQ, C_RK, C_RV, C_RG = 3712, 4224, 4736, 5248
C_MZ, C_XBC = 5760, 6272
C_DT = 7296
PROJ_W = C_DT + LANES
ORIG_RQ = 2048
ORIG_DT = 5632
ORIG_GQ = ORIG_DT + SSM_HEADS
ORIG_LR = ORIG_GQ + 2 * GLA_KEY_W + 2 * BRANCH_W
ORIG_W = ORIG_LR + GLA_LOWRANK
GLA_TAIL_ROWS = ORIG_W - ORIG_GQ
LR_LANE0 = LANES - GLA_LOWRANK

TILE = 256
CHUNK = 64
SUB = 8
N_SUB = CHUNK // SUB
N_KEY_GROUPS = N_HEADS + N_HEADS // 2
VMEM_LIMIT_BYTES = 56 * 1024 * 1024


def _dot(a, b):
    return jnp.dot(a, b, preferred_element_type=F32)


def _dot_nt(a, b):
    return lax.dot_general(a, b, (((1,), (1,)), ((), ())), preferred_element_type=F32)


def _dot_tn(a, b):
    return lax.dot_general(a, b, (((0,), (0,)), ((), ())), preferred_element_type=F32)


def _dot3(m_bf16, x):
    hi = x.astype(BF16)
    r1 = x - hi.astype(F32)
    mid = r1.astype(BF16)
    lo = (r1 - mid.astype(F32)).astype(BF16)
    n = x.shape[1]
    r = _dot(m_bf16, jnp.concatenate([hi, mid, lo], axis=1))
    return r[:, 0:n] + r[:, n:2 * n] + r[:, 2 * n:3 * n]


def _dot3_rhs(x, m_bf16):
    hi = x.astype(BF16)
    r1 = x - hi.astype(F32)
    mid = r1.astype(BF16)
    lo = (r1 - mid.astype(F32)).astype(BF16)
    return _dot(hi, m_bf16) + _dot(mid, m_bf16) + _dot(lo, m_bf16)


def _silu(x):
    hx = 0.5 * x
    return hx + hx * jnp.tanh(hx)


def _log1p_exp_neg_abs(x):
    return jnp.log(1.0 + jnp.exp(-jnp.abs(x)))


def _log_sigmoid(x):
    return jnp.minimum(x, 0.0) - _log1p_exp_neg_abs(x)


def _softplus(x):
    return jnp.maximum(x, 0.0) + _log1p_exp_neg_abs(x)


def _logaddexp(a, b):
    return jnp.maximum(a, b) + _log1p_exp_neg_abs(a - b)


def _rms(x, width):
    return x * lax.rsqrt(jnp.sum(x * x, axis=-1, keepdims=True) * (1.0 / width) + EPS)


def _iota(shape, dim):
    return lax.broadcasted_iota(jnp.int32, shape, dim)


def _div_pow2(x, d):
    shift = d.bit_length() - 1
    assert 1 << shift == d
    return lax.shift_right_logical(x, shift)


def _lower_tri(n, block):
    r = _iota((n, n), 0)
    c = _iota((n, n), 1)
    keep = (c <= r) & (_div_pow2(r, block) == _div_pow2(c, block))
    return jnp.where(keep, 1.0, 0.0).astype(BF16)


def _ada_kernel(c_ref, w_ref, b_ref, o_ref):
    c_act = _silu(c_ref[...]).astype(BF16)
    o_ref[...] = _dot(c_act, w_ref[...].astype(BF16)) + b_ref[...]


def _ada_mod(c_pad, w_ada, b_ada):
    n_blk = 3
    return pl.pallas_call(
        _ada_kernel,
        grid=(DEPTH, n_blk),
        in_specs=[
            pl.BlockSpec((SUBLANES, D_MODEL), lambda l, j: (0, 0)),
            pl.BlockSpec((None, D_MODEL, D_MODEL), lambda l, j: (l, 0, j)),
            pl.BlockSpec((None, 1, D_MODEL), lambda l, j: (l, 0, j)),
        ],
        out_specs=pl.BlockSpec((None, SUBLANES, D_MODEL), lambda l, j: (l, 0, j)),
        out_shape=jax.ShapeDtypeStruct((DEPTH, SUBLANES, 3 * D_MODEL), F32),
        name="ada_mod",
    )(c_pad, w_ada, b_ada.reshape(DEPTH, 1, 3 * D_MODEL))


HEADS_PER_MATMUL = 2
N_OFF_USED = SUB * (N_SUB * (N_SUB - 1) // 2)
N_OFF = -(-N_OFF_USED // LANES) * LANES


def _key_group_terms(q, k, cum, cum_sc, k_sc, heads_per_group):
    lane_w = LANES // heads_per_group

    def head_lanes(x, i):
        if heads_per_group == 1:
            return x
        lane = _iota(x.shape, 1)
        return jnp.where((lane >= i * lane_w) & (lane < (i + 1) * lane_w), x, 0.0)

    cum_sc[...] = cum
    k_sc[...] = k
    cum_last = cum_sc[CHUNK - 1:CHUNK, :]
    q_exp = q * jnp.exp2(cum)
    k_end = k * jnp.exp2(cum_last - cum)
    dec = jnp.exp2(cum_last)

    qs_parts = [jnp.zeros((SUB, LANES), F32)]
    ks_parts = []
    for i in range(1, N_SUB):
        ref_i = cum_sc[i * SUB - 1:i * SUB, :]
        qs_parts.append(q[i * SUB:(i + 1) * SUB, :] * jnp.exp2(cum[i * SUB:(i + 1) * SUB, :] - ref_i))
        ks_parts.append(k[:i * SUB, :] * jnp.exp2(ref_i - cum[:i * SUB, :]))
    ks_parts.append(jnp.zeros((N_OFF - N_OFF_USED, LANES), F32))
    qs = jnp.concatenate(qs_parts, axis=0)
    ks = jnp.concatenate(ks_parts, axis=0).astype(BF16)

    lane_c = _iota((SUB, CHUNK), 1)
    row_c = _iota((SUB, CHUNK), 0)
    diag_parts = [[] for _ in range(heads_per_group)]
    for i in range(N_SUB):
        acc = [jnp.zeros((SUB, CHUNK), F32) for _ in range(heads_per_group)]
        q_i = q[i * SUB:(i + 1) * SUB, :]
        cum_i = cum[i * SUB:(i + 1) * SUB, :]
        for j in range(SUB):
            s = i * SUB + j
            w = q_i * jnp.exp2(cum_i - cum_sc[s:s + 1, :]) * k_sc[s:s + 1, :]
            for hd in range(heads_per_group):
                a_col = jnp.sum(head_lanes(w, hd), axis=-1, keepdims=True)
                acc[hd] = jnp.where(lane_c == s, a_col, acc[hd])
        causal = (row_c + i * SUB) >= lane_c
        for hd in range(heads_per_group):
            diag_parts[hd].append(jnp.where(causal, acc[hd], 0.0))
    a_diag = [jnp.concatenate(p, axis=0) for p in diag_parts]
    return dict(
        qs=[head_lanes(qs, i).astype(BF16) for i in range(heads_per_group)],
        q_exp=[head_lanes(q_exp, i).astype(BF16) for i in range(heads_per_group)],
        k_end=[head_lanes(k_end, i).astype(BF16) for i in range(heads_per_group)],
        ks=ks, dec=dec, a_diag=a_diag)


def _vector_decay_chunk(groups, v_all, heads_per_group, st_ref):
    n_groups = len(groups)
    n_heads = n_groups * heads_per_group
    rows_all = n_heads * CHUNK
    zeros_bf = jnp.zeros((CHUNK, LANES), BF16)

    lhs_rows = []
    for h in range(n_heads):
        g, i = divmod(h, heads_per_group)
        lhs_rows.append(jnp.concatenate(
            [groups[g]["qs"][i] if gg == g else zeros_bf for gg in range(n_groups)], axis=1))
    ks_cat = jnp.concatenate([grp["ks"] for grp in groups], axis=1)
    p_all = _dot_nt(jnp.concatenate(lhs_rows, axis=0), ks_cat)
    row_blk = _div_pow2(_iota((rows_all, N_OFF), 0) & (CHUNK - 1), SUB)
    col = _iota((rows_all, N_OFF), 1)
    lo = lax.shift_right_logical(row_blk * (row_blk - 1), 1) * SUB
    p_all = jnp.where((col >= lo) & (col < lo + row_blk * SUB), p_all, 0.0)

    a_all = jnp.concatenate([a for grp in groups for a in grp["a_diag"]], axis=0)
    a_all = jnp.concatenate([a_all, jnp.zeros((rows_all, LANES - CHUNK), F32)], axis=1)
    scores = jnp.concatenate([p_all, a_all], axis=1).astype(BF16)
    width = v_all.shape[1]
    v_rows = jnp.concatenate(
        [v_all[:i * SUB, :] for i in range(1, N_SUB)]
        + [jnp.zeros((N_OFF - N_OFF_USED, width), F32), v_all, jnp.zeros((LANES - CHUNK, width), F32)],
        axis=0).astype(BF16)
    r = _dot(scores, v_rows)
    outs = [r[h * CHUNK:(h + 1) * CHUNK, h * HEAD_W:(h + 1) * HEAD_W] for h in range(n_heads)]

    v_bf = v_all.astype(BF16)
    if heads_per_group == 1:
        st = st_ref[...]
        st_bf = st.reshape(n_heads * HEAD_W, LANES).astype(BF16)
        qe_all = jnp.concatenate([grp["q_exp"][0] for grp in groups], axis=0)
        inter = _dot_nt(qe_all, st_bf)
        ke_cat = jnp.concatenate([grp["k_end"][0] for grp in groups], axis=1)
        upd = _dot_tn(v_bf, ke_cat)
        for h in range(n_heads):
            outs[h] = outs[h] + inter[h * CHUNK:(h + 1) * CHUNK, h * HEAD_W:(h + 1) * HEAD_W]
            st_ref[h] = st[h] * groups[h]["dec"] + upd[h * HEAD_W:(h + 1) * HEAD_W, h * LANES:(h + 1) * LANES]
    else:
        for g, grp in enumerate(groups):
            st = st_ref[g]
            qe_rows = jnp.concatenate(grp["q_exp"], axis=0)
            inter = _dot_nt(qe_rows, st.astype(BF16))
            ke_rows = jnp.concatenate(grp["k_end"], axis=0)
            v_g = [v_bf[:, (g * heads_per_group + i) * HEAD_W:(g * heads_per_group + i + 1) * HEAD_W]
                   for i in range(heads_per_group)]
            upd = _dot_tn(jnp.concatenate(v_g, axis=0), ke_rows)
            for i in range(heads_per_group):
                h = g * heads_per_group + i
                outs[h] = outs[h] + inter[i * CHUNK:(i + 1) * CHUNK, :]
            st_ref[g] = st * grp["dec"] + upd
    return outs


def _modulated_norm(x, mod_ref, normg_ref):
    shift = mod_ref[0:1, 0:D_MODEL]
    scale = mod_ref[0:1, D_MODEL:2 * D_MODEL]
    h = (_rms(x, D_MODEL) * normg_ref[...]) * (1.0 + scale) + shift
    return h.astype(BF16)


def _in_proj(h_ref, wt_ref, proj_ref, dst, row0, row1):
    n_blk = 1024
    for r0 in range(row0, row1, n_blk):
        r1 = min(r0 + n_blk, row1)
        proj_ref[:, dst + r0 - row0:dst + r1 - row0] = _dot_nt(h_ref[...], wt_ref[r0:r1, :])


def _in_proj_vector_part(h_ref, wt_ref, wt_gla_ref, proj_ref):
    _in_proj(h_ref, wt_ref, proj_ref, C_AQ, 0, ORIG_RQ)
    _in_proj(h_ref, wt_gla_ref, proj_ref, C_GQ, 0, ORIG_LR - ORIG_GQ)
    _in_proj(h_ref, wt_gla_ref, proj_ref, C_LR, GLA_TAIL_ROWS - LANES, GLA_TAIL_ROWS)


def _in_proj_scalar_part(h_ref, wt_ref, proj_ref):
    _in_proj(h_ref, wt_ref, proj_ref, C_RQ, ORIG_RQ, ORIG_DT)
    _in_proj(h_ref, wt_ref, proj_ref, C_DT, ORIG_DT, ORIG_DT + LANES)


def _mixer_prep(layer, proj_ref, hk_ref, gcum_ref, lb_logits_ref, w_gk2_ref, b_gk2_ref):
    tri_chunk = _lower_tri(TILE, CHUNK)

    lg = [lb_logits_ref[i:i + 1, :] for i in range(DEPTH)]
    lg_max = functools.reduce(jnp.maximum, lg)
    lg_exp = [jnp.exp(r - lg_max) for r in lg]
    lg_den = functools.reduce(lambda a, b: a + b, lg_exp)
    lower = jnp.zeros((1, BRANCH_W), F32)
    for i in range(1, layer + 1):
        lower = lower + lg_exp[i] / lg_den
    log_lb = jnp.log(lower)
    log1m_lb = jnp.log1p(-lower)
    proj_ref[:, C_AQ:C_AQ + BRANCH_W] = _silu(proj_ref[:, C_AQ:C_AQ + BRANCH_W])
    log_f = _logaddexp(log_lb, log1m_lb + _log_sigmoid(proj_ref[:, C_AF:C_AF + BRANCH_W]))
    hk_ref[...] = 1.0 - jnp.exp(log_f)
    proj_ref[:, C_AF:C_AF + BRANCH_W] = _dot3(tri_chunk, log_f * LOG2_E)

    low_rank = proj_ref[:, C_LR:C_LR + LANES]
    gk_gate = _dot(low_rank.astype(BF16), w_gk2_ref[...].astype(BF16)) + b_gk2_ref[...]
    gcum_ref[...] = _dot3(tri_chunk, _log_sigmoid(gk_gate) * (LOG2_E / GLA_GATE_TEMP))
    gla_dk = GLA_KEY_W // N_HEADS
    proj_ref[:, C_GQ:C_GQ + GLA_KEY_W] = proj_ref[:, C_GQ:C_GQ + GLA_KEY_W] * (gla_dk ** -0.5)


def _vector_decay_mixers(proj_ref, y_ref, hk_ref, gcum_ref, cum_sc, k_sc, st_hgrn, st_gla,
                         hgrn_g_ref, gla_g_ref):
    def chunk_body(ci):
        r0 = ci * CHUNK
        rows = slice(r0, r0 + CHUNK)
        groups = []
        for hd in range(N_HEADS):
            cs = slice(hd * HEAD_W, (hd + 1) * HEAD_W)
            groups.append(_key_group_terms(
                proj_ref[rows, C_AQ + cs.start:C_AQ + cs.stop], hk_ref[rows, cs],
                proj_ref[rows, C_AF + cs.start:C_AF + cs.stop],
                cum_sc.at[ci, hd], k_sc.at[ci, hd], 1))
        outs = []
        for p0 in range(0, N_HEADS, HEADS_PER_MATMUL):
            outs += _vector_decay_chunk(
                groups[p0:p0 + HEADS_PER_MATMUL],
                proj_ref[rows, C_AI + p0 * HEAD_W:C_AI + (p0 + HEADS_PER_MATMUL) * HEAD_W],
                1, st_hgrn.at[pl.ds(p0, HEADS_PER_MATMUL)])
        for hd, o in enumerate(outs):
            cs = slice(hd * HEAD_W, (hd + 1) * HEAD_W)
            g_act = _silu(proj_ref[rows, C_AG + cs.start:C_AG + cs.stop])
            y_ref[rows, cs] = (_rms(o, HEAD_W) * hgrn_g_ref[:, cs] * g_act).astype(BF16)
        groups = []
        for pr in range(N_HEADS // 2):
            ks_ = slice(pr * LANES, (pr + 1) * LANES)
            groups.append(_key_group_terms(
                proj_ref[rows, C_GQ + ks_.start:C_GQ + ks_.stop],
                proj_ref[rows, C_GK + ks_.start:C_GK + ks_.stop],
                gcum_ref[rows, ks_],
                cum_sc.at[ci, N_HEADS + pr], k_sc.at[ci, N_HEADS + pr], 2))
        outs = []
        for pr in range(N_HEADS // 2):
            outs += _vector_decay_chunk(
                groups[pr:pr + 1],
                proj_ref[rows, C_GV + 2 * pr * HEAD_W:C_GV + 2 * (pr + 1) * HEAD_W],
                2, st_gla.at[pl.ds(pr, 1)])
        for hd, o in enumerate(outs):
            cs = slice(hd * HEAD_W, (hd + 1) * HEAD_W)
            g_act = _silu(proj_ref[rows, C_GG + cs.start:C_GG + cs.stop])
            y_ref[rows, 3 * BRANCH_W + cs.start:3 * BRANCH_W + cs.stop] = (
                _rms(o, HEAD_W) * gla_g_ref[:, cs] * g_act).astype(BF16)

    for ci in range(TILE // CHUNK):
        chunk_body(ci)


def _ret_log2_gamma(hd):
    return math.log1p(-(2.0 ** -(RET_DECAY_EXP0 + hd))) * LOG2_E


def _init_retention_decays(ret_decay_ref, ret_edge_ref):
    t_col = _iota((TILE, TILE), 0)
    s_row = _iota((TILE, TILE), 1)
    t_minus_s = (t_col - s_row).astype(F32)
    t_plus1 = (_iota((TILE, HEAD_W), 0) + 1).astype(F32)
    s_to_end = (TILE - 1 - _iota((TILE, HEAD_W), 0)).astype(F32)
    for hd in range(N_HEADS):
        log2_gamma = _ret_log2_gamma(hd)
        ret_decay_ref[hd] = jnp.where(t_col >= s_row, jnp.exp2(t_minus_s * log2_gamma), 0.0)
        ret_edge_ref[hd, 0] = jnp.exp2(t_plus1 * log2_gamma)
        ret_edge_ref[hd, 1] = jnp.exp2(s_to_end * log2_gamma)


def _scalar_decay_mixers(proj_ref, y_ref, conv_ref, st_ret, st_ssd, ret_decay_ref, ret_edge_ref,
                         cos_ref, sin_ref,
                         ret_g_ref, conv_w_ref, conv_b_ref, dt_bias_ref, a_log_ref, d_skip_ref,
                         ssm_g_ref):
    tri_tile = _lower_tri(TILE, TILE)
    def cols(c0, width):
        return proj_ref[:, c0:c0 + width]

    tail = cols(C_DT, LANES)

    cos_t = cos_ref[...]
    sin_t = sin_ref[...]
    hpm = HEADS_PER_MATMUL
    for p0 in range(0, N_HEADS, hpm):
        q_bfs, w_bfs, k_ends, st_decays = [], [], [], []
        for hd in range(p0, p0 + hpm):
            cs = slice(hd * HEAD_W, (hd + 1) * HEAD_W)
            rq = cols(C_RQ + cs.start, HEAD_W)
            rk = cols(C_RK + cs.start, HEAD_W)
            q = rq * cos_t + pltpu.roll(rq, HEAD_W // 2, 1) * sin_t
            k = (rk * cos_t + pltpu.roll(rk, HEAD_W // 2, 1) * sin_t) * (HEAD_W ** -0.5)
            q_bfs.append(q.astype(BF16))
            w_bfs.append((_dot_nt(q_bfs[-1], k.astype(BF16)) * ret_decay_ref[hd]).astype(BF16))
            k_ends.append((k * ret_edge_ref[hd, 1]).astype(BF16))
            st_decays.append(jnp.full((1, HEAD_W), 2.0 ** (TILE * _ret_log2_gamma(hd)), F32))
        pair = p0 // hpm
        v_bf = cols(C_RV + p0 * HEAD_W, hpm * HEAD_W).astype(BF16)
        st = st_ret[pair]
        intra = _dot(jnp.concatenate(w_bfs, axis=0), v_bf)
        inter = _dot(jnp.concatenate(q_bfs, axis=0), st.astype(BF16))
        upd = _dot_tn(jnp.concatenate(k_ends, axis=1), v_bf)
        st_ret[pair] = (st * jnp.concatenate(st_decays, axis=1)
                        + jnp.concatenate([upd[i * HEAD_W:(i + 1) * HEAD_W, i * HEAD_W:(i + 1) * HEAD_W]
                                           for i in range(hpm)], axis=1))
        for i in range(hpm):
            hd = p0 + i
            cs = slice(hd * HEAD_W, (hd + 1) * HEAD_W)
            blk = (slice(i * TILE, (i + 1) * TILE), slice(i * HEAD_W, (i + 1) * HEAD_W))
            o = intra[blk] + ret_edge_ref[hd, 0] * inter[blk]
            g_act = _silu(cols(C_RG + cs.start, HEAD_W))
            y_ref[:, BRANCH_W + cs.start:BRANCH_W + cs.stop] = (
                _rms(o, HEAD_W) * ret_g_ref[:, cs] * g_act).astype(BF16)

    u = cols(C_XBC, SSM_CONV_CH)
    prev = conv_ref[...]
    row8 = _iota((SUBLANES, SSM_CONV_CH), 0)
    xbc = conv_b_ref[...] + conv_w_ref[SSM_CONV - 1:SSM_CONV, :] * u
    for d in range(1, SSM_CONV):
        u_d = pltpu.roll(u, d, 0)
        head = jnp.where(row8 < d, pltpu.roll(prev, d, 0), u_d[0:SUBLANES, :])
        u_d = jnp.concatenate([head, u_d[SUBLANES:, :]], axis=0)
        xbc = xbc + conv_w_ref[SSM_CONV - 1 - d:SSM_CONV - d, :] * u_d
    conv_ref[...] = u[TILE - SUBLANES:TILE, :]
    xbc = _silu(xbc)
    xs = xbc[:, 0:BRANCH_W]
    dt = _softplus(tail + dt_bias_ref[...])
    log_a = dt * (-LOG2_E * jnp.exp(a_log_ref[...]))
    cum = _dot3(tri_tile, log_a)
    cum_t = cum.T

    e_r = _iota((LANES, BRANCH_W), 0)
    e_c = _iota((LANES, BRANCH_W), 1)
    expand = jnp.where(e_r == _div_pow2(e_c, SSM_HEAD_DIM), 1.0, 0.0).astype(BF16)
    dt_e = _dot(dt.astype(BF16), expand)
    cum_e = _dot3_rhs(cum, expand)
    cum_last_e = cum_e[TILE - 1:TILE, :]
    d_e = _dot3_rhs(jnp.broadcast_to(d_skip_ref[...], (SUBLANES, LANES)), expand)[0:1, :]
    xdt = xs * dt_e
    x_end = (xdt * jnp.exp2(cum_last_e - cum_e)).astype(BF16)
    half = TILE // 2
    tri_half = _iota((half, half), 0) >= _iota((half, half), 1)
    hpg = SSM_HEADS // SSM_GROUPS
    gw = hpg * SSM_HEAD_DIM
    o_parts = []
    for g in range(SSM_GROUPS):
        b_bf = xbc[:, BRANCH_W + g * SSM_STATE:BRANCH_W + (g + 1) * SSM_STATE].astype(BF16)
        c_bf = xbc[:, BRANCH_W + (SSM_GROUPS + g) * SSM_STATE:
                   BRANCH_W + (SSM_GROUPS + g + 1) * SSM_STATE].astype(BF16)
        cb = _dot_nt(c_bf, b_bf)
        st = st_ssd[g]
        inter = _dot(c_bf, st.astype(BF16)) * jnp.exp2(cum_e[:, g * gw:(g + 1) * gw])
        w_rows = []
        for hh in range(hpg):
            hd = g * hpg + hh
            quad = {}
            for qi in range(2):
                for qj in range(qi + 1):
                    rq_, cq_ = slice(qi * half, (qi + 1) * half), slice(qj * half, (qj + 1) * half)
                    rel = cum[rq_, hd:hd + 1] - cum_t[hd:hd + 1, cq_]
                    w_q = cb[rq_, cq_] * jnp.exp2(rel)
                    quad[qi, qj] = (jnp.where(tri_half, w_q, 0.0) if qi == qj else w_q).astype(BF16)
            w_rows += [jnp.concatenate([quad[0, 0], jnp.zeros((half, half), BF16)], axis=1),
                       jnp.concatenate([quad[1, 0], quad[1, 1]], axis=1)]
        intra = _dot(jnp.concatenate(w_rows, axis=0), xdt[:, g * gw:(g + 1) * gw].astype(BF16))
        for hh in range(hpg):
            lanes = slice(hh * SSM_HEAD_DIM, (hh + 1) * SSM_HEAD_DIM)
            o_parts.append(intra[hh * TILE:(hh + 1) * TILE, lanes] + inter[:, lanes])
        st_ssd[g] = (st * jnp.exp2(cum_last_e[:, g * gw:(g + 1) * gw])
                     + _dot_tn(b_bf, x_end[:, g * gw:(g + 1) * gw]))
    o_ssd = jnp.concatenate(o_parts, axis=1)
    y_ssd = (o_ssd + d_e * xs) * _silu(cols(C_MZ, BRANCH_W))
    gn = BRANCH_W // SSM_GROUPS
    for g in range(SSM_GROUPS):
        cs = slice(g * gn, (g + 1) * gn)
        y_ref[:, 2 * BRANCH_W + cs.start:2 * BRANCH_W + cs.stop] = (
            _rms(y_ssd[:, cs], gn) * ssm_g_ref[:, cs]).astype(BF16)


def _layer_kernel(layer, final,
                  x_ref, xn_ref, mod_ref, modn_ref, normg_ref, wt_ref, wt_gla_ref, w_out_ref,
                  cos_ref, sin_ref,
                  lb_logits_ref, hgrn_g_ref, ret_g_ref, conv_w_ref, conv_b_ref, dt_bias_ref,
                  a_log_ref, d_skip_ref, ssm_g_ref, w_gk2_ref, b_gk2_ref, gla_g_ref, final_g_ref,
                  o_ref,
                  proj_ref, h_ref, y_ref, conv_ref, hk_ref, gcum_ref, cum_sc, k_sc,
                  st_hgrn, st_ret, st_ssd, st_gla, ret_decay_ref, ret_edge_ref):
    b_idx = pl.program_id(0)
    t_idx = pl.program_id(1)

    def project_vector_part(x_tile, tile_mod_ref):
        h_ref[...] = _modulated_norm(x_tile, tile_mod_ref, normg_ref)
        _in_proj_vector_part(h_ref, wt_ref, wt_gla_ref, proj_ref)
        _mixer_prep(layer, proj_ref, hk_ref, gcum_ref, lb_logits_ref, w_gk2_ref, b_gk2_ref)

    @pl.when(t_idx == 0)
    def _():
        st_hgrn[...] = jnp.zeros_like(st_hgrn)
        st_ret[...] = jnp.zeros_like(st_ret)
        st_ssd[...] = jnp.zeros_like(st_ssd)
        st_gla[...] = jnp.zeros_like(st_gla)
        conv_ref[...] = jnp.zeros_like(conv_ref)

    @pl.when((t_idx == 0) & (b_idx == 0))
    def _():
        project_vector_part(x_ref[...], mod_ref)
        _init_retention_decays(ret_decay_ref, ret_edge_ref)

    _in_proj_scalar_part(h_ref, wt_ref, proj_ref)
    _vector_decay_mixers(proj_ref, y_ref, hk_ref, gcum_ref, cum_sc, k_sc, st_hgrn, st_gla,
                         hgrn_g_ref, gla_g_ref)
    project_vector_part(xn_ref[...], modn_ref)
    _scalar_decay_mixers(proj_ref, y_ref, conv_ref, st_ret, st_ssd, ret_decay_ref, ret_edge_ref,
                         cos_ref, sin_ref,
                         ret_g_ref, conv_w_ref, conv_b_ref, dt_bias_ref, a_log_ref, d_skip_ref,
                         ssm_g_ref)
    gate = mod_ref[0:1, 2 * D_MODEL:3 * D_MODEL]
    x_new = x_ref[...] + gate * _dot(y_ref[...], w_out_ref[...])
    if final:
        x_new = _rms(x_new, D_MODEL) * final_g_ref[...]
    o_ref[...] = x_new


def _resident(shape):
    nd = len(shape)
    return pl.BlockSpec(shape, lambda b, t, _nd=nd: (0,) * _nd, pipeline_mode=pl.Buffered(1))


def _layer_call(layer, final, x, mod, params):
    batch, seq, _ = x.shape
    n_tiles = seq // TILE
    n_chunks = TILE // CHUNK

    def next_tile(t):
        return jnp.where(t == n_tiles - 1, 0, t + 1)

    def next_batch(b, t):
        return jnp.where(t == n_tiles - 1, jnp.minimum(b + 1, batch - 1), b)

    small = [params[k] for k in (
        "lb_logits", "hgrn_g", "ret_g", "conv_w", "conv_b", "dt_bias", "a_log", "d_skip",
        "ssm_g", "w_gk2", "b_gk2", "gla_g", "final_g")]
    in_specs = [
        pl.BlockSpec((None, TILE, D_MODEL), lambda b, t: (b, t, 0)),
        pl.BlockSpec((None, TILE, D_MODEL), lambda b, t: (next_batch(b, t), next_tile(t), 0)),
        pl.BlockSpec((None, SUBLANES, 3 * D_MODEL), lambda b, t: (b, 0, 0)),
        pl.BlockSpec((None, SUBLANES, 3 * D_MODEL), lambda b, t: (next_batch(b, t), 0, 0)),
        _resident((1, D_MODEL)),
        pl.BlockSpec((None, ORIG_W, D_MODEL), lambda b, t: (layer, 0, 0), pipeline_mode=pl.Buffered(1)),
        pl.BlockSpec((None, GLA_TAIL_ROWS, D_MODEL), lambda b, t: (layer, 0, 0),
                     pipeline_mode=pl.Buffered(1)),
        pl.BlockSpec((None, D_INNER, D_MODEL), lambda b, t: (layer, 0, 0), pipeline_mode=pl.Buffered(1)),
        pl.BlockSpec((TILE, HEAD_W), lambda b, t: (t, 0)),
        pl.BlockSpec((TILE, HEAD_W), lambda b, t: (t, 0)),
    ] + [_resident(a.shape) for a in small]
    scratch = [
        pltpu.VMEM((TILE, PROJ_W), F32),
        pltpu.VMEM((TILE, D_MODEL), BF16),
        pltpu.VMEM((TILE, D_INNER), BF16),
        pltpu.VMEM((SUBLANES, SSM_CONV_CH), F32),
        pltpu.VMEM((TILE, BRANCH_W), F32),
        pltpu.VMEM((TILE, GLA_KEY_W), F32),
        pltpu.VMEM((n_chunks, N_KEY_GROUPS, CHUNK, LANES), F32),
        pltpu.VMEM((n_chunks, N_KEY_GROUPS, CHUNK, LANES), F32),
        pltpu.VMEM((N_HEADS, HEAD_W, HEAD_W), F32),
        pltpu.VMEM((N_HEADS // HEADS_PER_MATMUL, HEAD_W, HEADS_PER_MATMUL * HEAD_W), F32),
        pltpu.VMEM((SSM_GROUPS, SSM_STATE, BRANCH_W // SSM_GROUPS), F32),
        pltpu.VMEM((N_HEADS // 2, HEAD_W, LANES), F32),
        pltpu.VMEM((N_HEADS, TILE, TILE), F32),
        pltpu.VMEM((N_HEADS, 2, TILE, HEAD_W), F32),
    ]
    return pl.pallas_call(
        functools.partial(_layer_kernel, layer, final),
        grid=(batch, n_tiles),
        in_specs=in_specs,
        out_specs=pl.BlockSpec((None, TILE, D_MODEL), lambda b, t: (b, t, 0)),
        out_shape=jax.ShapeDtypeStruct(x.shape, F32),
        scratch_shapes=scratch,
        compiler_params=pltpu.CompilerParams(
            dimension_semantics=("arbitrary", "arbitrary"),
            vmem_limit_bytes=VMEM_LIMIT_BYTES),
        name=f"mixer_layer{layer}",
    )(x, x, mod, mod, params["norm_g"], params["wt"], params["wt_gla"], params["w_out"], params["cos"], params["sin"], *small)


def _pad_lanes(v, lane0=0):
    return jnp.zeros((1, LANES), F32).at[0, lane0:lane0 + v.shape[0]].set(v)


def kernel(x, c, w_ada, b_ada, norm_g, w_in, hgrn_lb_logits, hgrn_onorm_g, ret_onorm_g, ssm_conv_w,
           ssm_conv_b, ssm_dt_bias, ssm_a_log, ssm_d, ssm_norm_g, gla_w_gk2, gla_b_gk2, gla_onorm_g,
           w_out, final_g):
    batch, seq, _ = x.shape
    c_pad = jnp.zeros((SUBLANES, D_MODEL), F32).at[:batch].set(c)
    mod_all = _ada_mod(c_pad, w_ada, b_ada)

    inv_freq = ROPE_BASE ** (-np.arange(0, HEAD_W, 2, dtype=np.float64) / HEAD_W)
    ang = np.arange(seq, dtype=np.float64)[:, None] * inv_freq[None, :]
    cos_tab = jnp.asarray(np.concatenate([np.cos(ang), np.cos(ang)], axis=-1), F32)
    sin_tab = jnp.asarray(np.concatenate([-np.sin(ang), np.sin(ang)], axis=-1), F32)

    wt = jnp.swapaxes(w_in, 1, 2).astype(BF16)
    wt_gla = wt[:, ORIG_GQ:, :]
    w_out_bf = w_out.astype(BF16)

    for layer in range(DEPTH):
        w_gk2_p = jnp.zeros((LANES, GLA_KEY_W), F32).at[LR_LANE0:].set(gla_w_gk2[layer])
        params = dict(
            norm_g=norm_g[layer][None, :], wt=wt, wt_gla=wt_gla, w_out=w_out_bf,
            cos=cos_tab, sin=sin_tab,
            lb_logits=hgrn_lb_logits, hgrn_g=hgrn_onorm_g[layer][None, :],
            ret_g=ret_onorm_g[layer][None, :], conv_w=ssm_conv_w[layer],
            conv_b=ssm_conv_b[layer][None, :], dt_bias=_pad_lanes(ssm_dt_bias[layer]),
            a_log=_pad_lanes(ssm_a_log[layer]), d_skip=_pad_lanes(ssm_d[layer]),
            ssm_g=ssm_norm_g[layer][None, :], w_gk2=w_gk2_p, b_gk2=gla_b_gk2[layer][None, :],
            gla_g=gla_onorm_g[layer][None, :], final_g=final_g[None, :])
        mod = jnp.broadcast_to(mod_all[layer][:batch, None, :], (batch, SUBLANES, 3 * D_MODEL))
        x = _layer_call(layer, layer == DEPTH - 1, x, mod, params)
    return x
```
